```python
import math
import jax
import jax.numpy as jnp
from jax import lax
import numpy as np

D_MODEL = 1024
BATCH = 8
SEQ = 8192
DEPTH = 2

CHUNK = 64
Q_BLOCK = 128

N_MIXERS = 4
W_GROUP = D_MODEL // N_MIXERS

RWKV_HEAD = 64
RWKV_HEADS = W_GROUP // RWKV_HEAD
RWKV_W_RANK = 32
RWKV_A_RANK = 32
RWKV_G_RANK = 64
RWKV_LN_EPS = 64e-5

S5_GROUP = 16
S5_GROUPS = W_GROUP // S5_GROUP
S5_STATE = 64
S5_DT_MIN = 1e-3
S5_DT_MAX = 1e-1

MLA_HEADS = 4
MLA_NOPE = 64
MLA_ROPE = 32
MLA_QK = MLA_NOPE + MLA_ROPE
MLA_V = W_GROUP // MLA_HEADS
MLA_Q_RANK = 256
MLA_KV_RANK = 128
ROPE_THETA = 10000.0

LRU_BLOCKS = 4
LRU_BLOCK = W_GROUP // LRU_BLOCKS
LRU_C = 8.0
CONV_WIDTH = 4

N_GROUPS = 4
EXPERTS_PER_GROUP = 8
N_EXPERTS = N_GROUPS * EXPERTS_PER_GROUP
TOP_K = 2
D_FF_EXPERT = 512
MOE_BLOCK = 256

NORM_EPS = 1e-6

RWKV_SPLITS = (W_GROUP, W_GROUP, W_GROUP, RWKV_W_RANK, RWKV_A_RANK, RWKV_G_RANK)
RWKV_IN = sum(RWKV_SPLITS)
IN_SPLITS = (RWKV_IN, W_GROUP, MLA_Q_RANK, MLA_KV_RANK, MLA_ROPE, W_GROUP, W_GROUP)
D_IN = sum(IN_SPLITS)

kernel_name = 'hybrid_chunk_causal_streaming_trunk'


def _split(t, sizes):
    idx = [int(i) for i in np.cumsum(sizes)[:-1]]
    return jnp.split(t, idx, axis=-1)


def rms_norm(x, g, eps=NORM_EPS):
    xf = x.astype(jnp.float32)
    y = xf * lax.rsqrt(jnp.mean(xf * xf, axis=-1, keepdims=True) + eps)
    return (y * g.astype(jnp.float32)).astype(x.dtype)


def rope(t, pos):
    half = MLA_ROPE // 2
    inv_freq = jnp.power(ROPE_THETA, -jnp.arange(half, dtype=jnp.float32) * 2.0 / MLA_ROPE)
    ang = pos.astype(jnp.float32)[..., None] * inv_freq
    cos = jnp.cos(ang)[:, :, None, :]
    sin = jnp.sin(ang)[:, :, None, :]
    tf = t.astype(jnp.float32)
    t1, t2 = tf[..., :half], tf[..., half:]
    return jnp.concatenate([t1 * cos - t2 * sin, t1 * sin + t2 * cos], axis=-1).astype(t.dtype)


def rwkv7_mixer(z, mu, w0, w2, a0, a2, g2, k_k, k_a, r_k, ln_w, ln_b):
    Bsz, S, _ = z.shape
    z = z.astype(jnp.float32)
    z_prev = jnp.pad(z, ((0, 0), (1, 0), (0, 0)))[:, :-1]
    z = z + (z_prev - z) * mu
    r, k, v, w_lo, a_lo, g_lo = _split(z, RWKV_SPLITS)
    w = -jax.nn.softplus(-(w0 + jnp.tanh(w_lo) @ w2)) - 0.5
    decay = jnp.exp(-jnp.exp(w))
    a = jax.nn.sigmoid(a0 + a_lo @ a2)
    g = jax.nn.sigmoid(g_lo) @ g2

    def heads(t):
        return t.reshape(Bsz, S, RWKV_HEADS, RWKV_HEAD)

    kk = heads(k * k_k)
    kk = kk / jnp.maximum(jnp.linalg.norm(kk, axis=-1, keepdims=True), 1e-12)
    k = k * (1.0 + (a - 1.0) * k_a)
    r_h, k_h, v_h, w_h, a_h = heads(r), heads(k), heads(v), heads(decay), heads(a)

    def step(state, inp):
        r_t, w_t, k_t, v_t, kk_t, a_t = inp
        sa = jnp.einsum('bhvk,bhk->bhv', state, -kk_t)
        state = (state * w_t[:, :, None, :]
                 + sa[..., None] * (kk_t * a_t)[:, :, None, :]
                 + v_t[..., None] * k_t[:, :, None, :])
        return state, jnp.einsum('bhvk,bhk->bhv', state, r_t)

    def tm(t):
        return jnp.swapaxes(t, 0, 1)

    s0 = jnp.zeros((Bsz, RWKV_HEADS, RWKV_HEAD, RWKV_HEAD), jnp.float32)
    _, y = lax.scan(step, s0, (tm(r_h), tm(w_h), tm(k_h), tm(v_h), tm(kk), tm(a_h)))
    y = tm(y)
    mean = jnp.mean(y, axis=-1, keepdims=True)
    var = jnp.mean(jnp.square(y - mean), axis=-1, keepdims=True)
    y = ((y - mean) * lax.rsqrt(var + RWKV_LN_EPS) * ln_w.reshape(RWKV_HEADS, RWKV_HEAD)
         + ln_b.reshape(RWKV_HEADS, RWKV_HEAD))
    bonus = jnp.sum(r_h * k_h * r_k, axis=-1, keepdims=True) * v_h
    return (y + bonus).reshape(Bsz, S, W_GROUP) * g


def s5_mixer(u, lam_re, lam_im, b_re, b_im, c_re, c_im, d, log_dt, glu_w, glu_b):
    Bsz, S, _ = u.shape
    f32 = jnp.float32
    lam_re, lam_im, b_re, b_im, c_re, c_im = (t.astype(f32) for t in (lam_re, lam_im, b_re, b_im, c_re, c_im))
    uf = u.astype(f32)
    dt = jnp.exp(log_dt.astype(f32))[:, None]
    mag = jnp.exp(lam_re * dt)
    a_re = mag * jnp.cos(lam_im * dt)
    a_im = mag * jnp.sin(lam_im * dt)
    den = lam_re * lam_re + lam_im * lam_im
    q_re = ((a_re - 1.0) * lam_re + a_im * lam_im) / den
    q_im = (a_im * lam_re - (a_re - 1.0) * lam_im) / den
    bb_re = q_re[..., None] * b_re - q_im[..., None] * b_im
    bb_im = q_re[..., None] * b_im + q_im[..., None] * b_re
    ug = uf.reshape(Bsz, S, S5_GROUPS, S5_GROUP)
    bu_re = jnp.einsum('bsgi,gpi->bsgp', ug, bb_re)
    bu_im = jnp.einsum('bsgi,gpi->bsgp', ug, bb_im)
    shape = (1, S, S5_GROUPS, S5_STATE)
    ar = jnp.broadcast_to(a_re, shape)
    ai = jnp.broadcast_to(a_im, shape)

    def combine(e1, e2):
        a1r, a1i, b1r, b1i = e1
        a2r, a2i, b2r, b2i = e2
        return (a2r * a1r - a2i * a1i, a2r * a1i + a2i * a1r,
                a2r * b1r - a2i * b1i + b2r, a2r * b1i + a2i * b1r + b2i)

    _, _, h_re, h_im = lax.associative_scan(combine, (ar, ai, bu_re, bu_im), axis=1)
    y = jnp.einsum('bsgp,gip->bsgi', h_re, c_re) - jnp.einsum('bsgp,gip->bsgi', h_im, c_im)
    y = y.reshape(Bsz, S, W_GROUP) + d * uf
    y = jax.nn.gelu(y)
    return y * jax.nn.sigmoid(y @ glu_w + glu_b)


def mla_mixer(q_a, kv_a, k_pe, pos, q_norm_g, w_uq, kv_norm_g, w_ukv, q_head_g, k_head_g):
    Bsz, S, _ = q_a.shape
    q = (rms_norm(q_a, q_norm_g) @ w_uq).reshape(Bsz, S, MLA_HEADS, MLA_QK)
    kv = (rms_norm(kv_a, kv_norm_g) @ w_ukv).reshape(Bsz, S, MLA_HEADS, MLA_NOPE + MLA_V)
    k_nope, v = kv[..., :MLA_NOPE], kv[..., MLA_NOPE:]
    k = jnp.concatenate([k_nope, jnp.broadcast_to(k_pe[:, :, None, :], (Bsz, S, MLA_HEADS, MLA_ROPE))], axis=-1)
    q = rms_norm(q, q_head_g)
    k = rms_norm(k, k_head_g)
    q = jnp.concatenate([q[..., :MLA_NOPE], rope(q[..., MLA_NOPE:], pos)], axis=-1)
    k = jnp.concatenate([k[..., :MLA_NOPE], rope(k[..., MLA_NOPE:], pos)], axis=-1)
    n_blk = S // Q_BLOCK
    q_blocks = q.reshape(Bsz, n_blk, Q_BLOCK, MLA_HEADS, MLA_QK).transpose(1, 0, 2, 3, 4)
    key_chunk = jnp.arange(S) // CHUNK
    scale = MLA_QK ** -0.5

    def attend(args):
        qb, j = args
        s = jnp.einsum('bqhd,bkhd->bhqk', qb, k).astype(jnp.float32) * scale
        q_chunk = (j * Q_BLOCK + jnp.arange(Q_BLOCK)) // CHUNK
        mask = key_chunk[None, :] <= q_chunk[:, None]
        s = jnp.where(mask, s, -jnp.inf)
        p = jax.nn.softmax(s, axis=-1).astype(v.dtype)
        return jnp.einsum('bhqk,bkhd->bqhd', p, v)

    o = lax.map(attend, (q_blocks, jnp.arange(n_blk)))
    return o.transpose(1, 0, 2, 3, 4).reshape(Bsz, S, MLA_HEADS * MLA_V)


def rglru_mixer(x_in, gate, conv_w, conv_b, w_a, b_a, w_x, b_x, lam):
    Bsz, S, W = x_in.shape
    xc = lax.conv_general_dilated(x_in, conv_w[:, None, :], window_strides=(1,),
                                  padding=[(CONV_WIDTH - 1, 0)],
                                  dimension_numbers=('NWC', 'WIO', 'NWC'),
                                  feature_group_count=W) + conv_b
    xb = xc.reshape(Bsz, S, LRU_BLOCKS, LRU_BLOCK)
    r = jax.nn.sigmoid(jnp.einsum('bsnk,nkj->bsnj', xb, w_a).reshape(Bsz, S, W) + b_a)
    i = jax.nn.sigmoid(jnp.einsum('bsnk,nkj->bsnj', xb, w_x).reshape(Bsz, S, W) + b_x)
    log_a = (-LRU_C * r * jax.nn.softplus(-lam)).astype(jnp.float32)
    a = jnp.exp(log_a)
    b = jnp.sqrt(-jnp.expm1(2.0 * log_a)) * (i * xc).astype(jnp.float32)

    def combine(e1, e2):
        a1, b1 = e1
        a2, b2 = e2
        return (a1 * a2, a2 * b1 + b2)

    _, h = lax.associative_scan(combine, (a, b), axis=1)
    return h.astype(x_in.dtype) * jax.nn.gelu(gate)


def hier_moe(h, w_group, b_group, w_expert, b_expert, w1, w3, w2):
    Bsz, S, D = h.shape
    T = Bsz * S
    xt = h.reshape(T, D)
    p_group = jax.nn.softmax((xt @ w_group + b_group).astype(jnp.float32), axis=-1)
    g_sel = jnp.argmax(p_group, axis=-1).astype(jnp.int32)
    g_prob = jnp.max(p_group, axis=-1)
    logit_e = (xt @ w_expert + b_expert).astype(jnp.float32).reshape(T, N_GROUPS, EXPERTS_PER_GROUP)
    logit_e = jnp.take_along_axis(logit_e, g_sel[:, None, None], axis=1)[:, 0]
    p_e = jax.nn.softmax(logit_e, axis=-1)
    top_p, top_i = lax.top_k(p_e, TOP_K)
    gate = (g_prob[:, None] * top_p / jnp.sum(top_p, axis=-1, keepdims=True)).reshape(-1)
    expert = (g_sel[:, None] * EXPERTS_PER_GROUP + top_i.astype(jnp.int32)).reshape(-1)
    tok = jnp.repeat(jnp.arange(T, dtype=jnp.int32), TOP_K)

    n_assign = T * TOP_K
    n_blocks = -(-n_assign // MOE_BLOCK) + N_EXPERTS
    n_rows = n_blocks * MOE_BLOCK
    order = jnp.argsort(expert)
    e_s, tok_s, w_s = expert[order], tok[order], gate[order]
    counts = jax.ops.segment_sum(jnp.ones_like(e_s), e_s, num_segments=N_EXPERTS)
    start = jnp.cumsum(counts) - counts
    pcounts = (counts + MOE_BLOCK - 1) // MOE_BLOCK * MOE_BLOCK
    pend = jnp.cumsum(pcounts)
    pstart = pend - pcounts
    dest = pstart[e_s] + (jnp.arange(n_assign, dtype=jnp.int32) - start[e_s])
    row_tok = jnp.zeros((n_rows,), jnp.int32).at[dest].set(tok_s)
    row_w = jnp.zeros((n_rows,), w_s.dtype).at[dest].set(w_s)
    blk_start = jnp.arange(n_blocks, dtype=jnp.int32) * MOE_BLOCK
    blk_expert = jnp.minimum(jnp.searchsorted(pend, blk_start, side='right'), N_EXPERTS - 1)

    def expert_block(args):
        toks, e = args
        xb = xt[toks]
        hid = jax.nn.silu(xb @ w1[e]) * (xb @ w3[e])
        return hid @ w2[e]

    yb = lax.map(expert_block, (row_tok.reshape(n_blocks, MOE_BLOCK), blk_expert))
    contrib = yb.reshape(n_rows, D) * row_w[:, None].astype(yb.dtype)
    y = jnp.zeros((T, D), yb.dtype).at[row_tok].add(contrib)
    return y.reshape(Bsz, S, D)


def setup_inputs(seed: int = 0) -> dict:
    key = jax.random.key(seed)
    ks = iter(jax.random.split(key, 64))
    L, D, W = DEPTH, D_MODEL, W_GROUP

    def nrm(shape, scale):
        return jax.random.normal(next(ks), shape, jnp.float32) * scale

    def unif(shape, lo, hi):
        return jax.random.uniform(next(ks), shape, jnp.float32, lo, hi)

    x = nrm((BATCH, SEQ, D), 1.0)
    c = nrm((BATCH, D), 1.0)
    pos_offset = jax.random.randint(next(ks), (BATCH,), 0, 65536, dtype=jnp.int32)
    ada_w = nrm((L, D, 6 * D), 0.3 * D ** -0.5)
    ada_b = nrm((L, 6 * D), 0.02)
    norm1_g = 1.0 + nrm((L, D), 0.02)
    w_in = nrm((L, D, D_IN), D ** -0.5)
    rwkv_mu = unif((L, RWKV_IN), 0.0, 1.0)
    rwkv_w0 = jnp.linspace(-6.0, -1.0, W, dtype=jnp.float32)[None, :] + nrm((L, W), 0.1)
    rwkv_w2 = nrm((L, RWKV_W_RANK, W), 0.1 * RWKV_W_RANK ** -0.5)
    rwkv_a0 = nrm((L, W), 0.1)
    rwkv_a2 = nrm((L, RWKV_A_RANK, W), 0.1 * RWKV_A_RANK ** -0.5)
    rwkv_g2 = nrm((L, RWKV_G_RANK, W), RWKV_G_RANK ** -0.5)
    rwkv_k_k = 0.85 + nrm((L, W), 0.02)
    rwkv_k_a = 1.0 + nrm((L, W), 0.02)
    rwkv_r_k = nrm((L, RWKV_HEADS, RWKV_HEAD), 0.1)
    rwkv_ln_w = 1.0 + nrm((L, W), 0.02)
    rwkv_ln_b = nrm((L, W), 0.02)
    s5_lambda_re = -0.5 + nrm((L, S5_GROUPS, S5_STATE), 0.01)
    s5_lambda_im = (math.pi * jnp.arange(S5_STATE, dtype=jnp.float32))[None, None, :] + nrm((L, S5_GROUPS, S5_STATE), 0.01)
    s5_b_re = nrm((L, S5_GROUPS, S5_STATE, S5_GROUP), (2 * S5_GROUP) ** -0.5)
    s5_b_im = nrm((L, S5_GROUPS, S5_STATE, S5_GROUP), (2 * S5_GROUP) ** -0.5)
    s5_c_re = nrm((L, S5_GROUPS, S5_GROUP, S5_STATE), 0.7)
    s5_c_im = nrm((L, S5_GROUPS, S5_GROUP, S5_STATE), 0.7)
    s5_d = nrm((L, W), 0.5)
    s5_log_dt = unif((L, S5_GROUPS), math.log(S5_DT_MIN), math.log(S5_DT_MAX))
    s5_glu_w = nrm((L, W, W), W ** -0.5)
    s5_glu_b = nrm((L, W), 0.02)
    mla_q_norm_g = 1.0 + nrm((L, MLA_Q_RANK), 0.02)
    mla_w_uq = nrm((L, MLA_Q_RANK, MLA_HEADS * MLA_QK), MLA_Q_RANK ** -0.5)
    mla_kv_norm_g = 1.0 + nrm((L, MLA_KV_RANK), 0.02)
    mla_w_ukv = nrm((L, MLA_KV_RANK, MLA_HEADS * (MLA_NOPE + MLA_V)), MLA_KV_RANK ** -0.5)
    mla_q_head_g = 1.0 + nrm((L, MLA_QK), 0.02)
    mla_k_head_g = 1.0 + nrm((L, MLA_QK), 0.02)
    lru_conv_w = nrm((L, CONV_WIDTH, W), CONV_WIDTH ** -0.5)
    lru_conv_b = nrm((L, W), 0.02)
    lru_w_a = nrm((L, LRU_BLOCKS, LRU_BLOCK, LRU_BLOCK), LRU_BLOCK ** -0.5)
    lru_b_a = nrm((L, W), 0.02)
    lru_w_x = nrm((L, LRU_BLOCKS, LRU_BLOCK, LRU_BLOCK), LRU_BLOCK ** -0.5)
    lru_b_x = nrm((L, W), 0.02)
    a_c = unif((L, W), 0.9, 0.999) ** (1.0 / LRU_C)
    lru_lambda = jnp.log(a_c) - jnp.log1p(-a_c)
    branch_norm_g = 1.0 + nrm((L, 3, W), 0.02)
    w_out = nrm((L, D, D), D ** -0.5)
    norm2_g = 1.0 + nrm((L, D), 0.02)
    moe_w_group = nrm((L, D, N_GROUPS), D ** -0.5)
    moe_b_group = nrm((L, N_GROUPS), 0.01)
    moe_w_expert = nrm((L, D, N_EXPERTS), D ** -0.5)
    moe_b_expert = nrm((L, N_EXPERTS), 0.01)
    moe_w1 = nrm((L, N_EXPERTS, D, D_FF_EXPERT), D ** -0.5)
    moe_w3 = nrm((L, N_EXPERTS, D, D_FF_EXPERT), D ** -0.5)
    moe_w2 = nrm((L, N_EXPERTS, D_FF_EXPERT, D), D_FF_EXPERT ** -0.5)
    return {'x': x, 'c': c, 'pos_offset': pos_offset, 'ada_w': ada_w, 'ada_b': ada_b,
            'norm1_g': norm1_g, 'w_in': w_in, 'rwkv_mu': rwkv_mu, 'rwkv_w0': rwkv_w0,
            'rwkv_w2': rwkv_w2, 'rwkv_a0': rwkv_a0, 'rwkv_a2': rwkv_a2, 'rwkv_g2': rwkv_g2,
            'rwkv_k_k': rwkv_k_k, 'rwkv_k_a': rwkv_k_a, 'rwkv_r_k': rwkv_r_k,
            'rwkv_ln_w': rwkv_ln_w, 'rwkv_ln_b': rwkv_ln_b, 's5_lambda_re': s5_lambda_re,
            's5_lambda_im': s5_lambda_im, 's5_b_re': s5_b_re, 's5_b_im': s5_b_im,
            's5_c_re': s5_c_re, 's5_c_im': s5_c_im, 's5_d': s5_d, 's5_log_dt': s5_log_dt,
            's5_glu_w': s5_glu_w, 's5_glu_b': s5_glu_b, 'mla_q_norm_g': mla_q_norm_g,
            'mla_w_uq': mla_w_uq, 'mla_kv_norm_g': mla_kv_norm_g, 'mla_w_ukv': mla_w_ukv,
            'mla_q_head_g': mla_q_head_g, 'mla_k_head_g': mla_k_head_g,
            'lru_conv_w': lru_conv_w, 'lru_conv_b': lru_conv_b, 'lru_w_a': lru_w_a,
            'lru_b_a': lru_b_a, 'lru_w_x': lru_w_x, 'lru_b_x': lru_b_x, 'lru_lambda': lru_lambda,
            'branch_norm_g': branch_norm_g, 'w_out': w_out, 'norm2_g': norm2_g,
            'moe_w_group': moe_w_group, 'moe_b_group': moe_b_group,
            'moe_w_expert': moe_w_expert, 'moe_b_expert': moe_b_expert,
            'moe_w1': moe_w1, 'moe_w3': moe_w3, 'moe_w2': moe_w2}


def reference(x, c, pos_offset, ada_w, ada_b, norm1_g, w_in, rwkv_mu, rwkv_w0, rwkv_w2,
              rwkv_a0, rwkv_a2, rwkv_g2, rwkv_k_k, rwkv_k_a, rwkv_r_k, rwkv_ln_w, rwkv_ln_b,
              s5_lambda_re, s5_lambda_im, s5_b_re, s5_b_im, s5_c_re, s5_c_im, s5_d, s5_log_dt,
              s5_glu_w, s5_glu_b, mla_q_norm_g, mla_w_uq, mla_kv_norm_g, mla_w_ukv,
              mla_q_head_g, mla_k_head_g, lru_conv_w, lru_conv_b, lru_w_a, lru_b_a, lru_w_x,
              lru_b_x, lru_lambda, branch_norm_g, w_out, norm2_g, moe_w_group, moe_b_group,
              moe_w_expert, moe_b_expert, moe_w1, moe_w3, moe_w2):
    Bsz, S, _ = x.shape
    pos = pos_offset[:, None] + jnp.arange(S, dtype=jnp.int32)[None, :]
    cond = jax.nn.silu(c)
    for l in range(DEPTH):
        mod = cond @ ada_w[l] + ada_b[l]
        sh1, sc1, gt1, sh2, sc2, gt2 = [m[:, None, :] for m in jnp.split(mod, 6, axis=-1)]
        h = rms_norm(x, norm1_g[l]) * (1.0 + sc1) + sh1
        z = h @ w_in[l]
        z_rwkv, z_s5, q_a, kv_a, k_pe, z_lru, z_gate = _split(z, IN_SPLITS)
        y_a = rwkv7_mixer(z_rwkv, rwkv_mu[l], rwkv_w0[l], rwkv_w2[l], rwkv_a0[l], rwkv_a2[l],
                          rwkv_g2[l], rwkv_k_k[l], rwkv_k_a[l], rwkv_r_k[l], rwkv_ln_w[l], rwkv_ln_b[l])
        y_b = s5_mixer(z_s5, s5_lambda_re[l], s5_lambda_im[l], s5_b_re[l], s5_b_im[l], s5_c_re[l],
                       s5_c_im[l], s5_d[l], s5_log_dt[l], s5_glu_w[l], s5_glu_b[l])
        y_c = mla_mixer(q_a, kv_a, k_pe, pos, mla_q_norm_g[l], mla_w_uq[l], mla_kv_norm_g[l],
                        mla_w_ukv[l], mla_q_head_g[l], mla_k_head_g[l])
        y_d = rglru_mixer(z_lru, z_gate, lru_conv_w[l], lru_conv_b[l], lru_w_a[l], lru_b_a[l],
                          lru_w_x[l], lru_b_x[l], lru_lambda[l])
        mix = jnp.concatenate([y_a.astype(h.dtype),
                               rms_norm(y_b.astype(h.dtype), branch_norm_g[l, 0]),
                               rms_norm(y_c.astype(h.dtype), branch_norm_g[l, 1]),
                               rms_norm(y_d.astype(h.dtype), branch_norm_g[l, 2])], axis=-1)
        x = x + gt1 * (mix @ w_out[l])
        h2 = rms_norm(x, norm2_g[l]) * (1.0 + sc2) + sh2
        x = x + gt2 * hier_moe(h2, moe_w_group[l], moe_b_group[l], moe_w_expert[l],
                               moe_b_expert[l], moe_w1[l], moe_w3[l], moe_w2[l])
    return x
```

```python
import functools
import math

import numpy as np
import jax
import jax.numpy as jnp
from jax import lax
from jax.experimental import pallas as pl
from jax.experimental.pallas import tpu as pltpu

F32 = jnp.float32
BF16 = jnp.bfloat16
HI = lax.Precision.HIGHEST

N_MIX = 4
RWKV_HEAD = 64
RWKV_W_RANK, RWKV_A_RANK, RWKV_G_RANK = 32, 32, 64
RWKV_LN_EPS = 64e-5
S5_GROUP, S5_STATE = 16, 64
MLA_HEADS, MLA_NOPE, MLA_ROPE, MLA_V = 4, 64, 32, 64
MLA_QK = MLA_NOPE + MLA_ROPE
ROPE_THETA = 10000.0
LRU_C = 8.0
CONV_WIDTH = 4
ATTN_CHUNK = 64
N_GROUPS, EXPERTS_PER_GROUP, TOP_K = 4, 8, 2
N_EXPERTS = N_GROUPS * EXPERTS_PER_GROUP
MOE_BLOCK = 256
NORM_EPS = 1e-6

LANE = 128
SUBLANE = 8
RWKV_CHUNK = 64
VMEM_LIMIT = 48 * 1024 * 1024


def _cp(sem, vmem=VMEM_LIMIT):
    return pltpu.CompilerParams(dimension_semantics=sem, vmem_limit_bytes=vmem)


def _dot(a, b):
    return jnp.dot(a, b, preferred_element_type=F32)


def _dot_hi(a, b):
    return jnp.dot(a, b, precision=HI, preferred_element_type=F32)


def _dot_nt(a, b):
    return lax.dot_general(a, b, (((1,), (1,)), ((), ())), preferred_element_type=F32)


def _dot_nt_hi(a, b):
    return lax.dot_general(a, b, (((1,), (1,)), ((), ())), precision=HI, preferred_element_type=F32)


def _softplus(x):
    return jnp.maximum(x, 0.0) + jnp.log1p(jnp.exp(-jnp.abs(x)))


def _rms(x, g, eps=NORM_EPS):
    return x * lax.rsqrt(jnp.mean(x * x, axis=-1, keepdims=True) + eps) * g


def _mod_kernel(c_ref, w_ref, b_ref, o_ref):
    c = c_ref[...]
    cond = c * jax.nn.sigmoid(c)
    o_ref[0] = _dot_hi(cond, w_ref[0]) + b_ref[0]


def _adaln_mod(c, ada_w, ada_b):
    L, D, D6 = ada_w.shape
    B = c.shape[0]
    nj = D6 // D
    return pl.pallas_call(
        _mod_kernel,
        grid=(L, nj),
        in_specs=[pl.BlockSpec((B, D), lambda l, j: (0, 0)),
                  pl.BlockSpec((1, D, D), lambda l, j: (l, 0, j)),
                  pl.BlockSpec((1, 1, D), lambda l, j: (l, 0, j))],
        out_specs=pl.BlockSpec((1, B, D), lambda l, j: (l, 0, j)),
        out_shape=jax.ShapeDtypeStruct((L, B, D6), F32),
        compiler_params=_cp(("parallel", "parallel")),
        name="adaln_mod",
    )(c, ada_w, ada_b.reshape(L, 1, D6))


_RWKV_IN = 3 * 256 + RWKV_W_RANK + RWKV_A_RANK + RWKV_G_RANK
_IN_COLS = (_RWKV_IN, 256, 256, 128, LANE, 256, 256)
_IN_OFFS = tuple(int(v) for v in np.cumsum((0,) + _IN_COLS))
_KPE_LANE0 = MLA_NOPE


def _pad_w_in(w_in):
    D = w_in.shape[0]
    o = np.cumsum((0, _RWKV_IN, 256, 256, 128, MLA_ROPE, 256, 256))
    pieces = [w_in[:, o[i]:o[i + 1]] for i in range(7)]
    kpe = jnp.zeros((D, LANE), w_in.dtype).at[:, _KPE_LANE0:_KPE_LANE0 + MLA_ROPE].set(pieces[4])
    pieces[4] = kpe
    return jnp.concatenate(pieces, axis=1).astype(BF16)


def _inproj_kernel(x_ref, sc_ref, sh_ref, g_ref, w_ref, *outs):
    x = x_ref[0]
    h = _rms(x, g_ref[...]) * (1.0 + sc_ref[0]) + sh_ref[0]
    z = _dot(h.astype(BF16), w_ref[...])
    for i, o_ref in enumerate(outs):
        o_ref[0] = z[:, _IN_OFFS[i]:_IN_OFFS[i + 1]]


def _inproj(x, sc, sh, g, w_pad, tm):
    B, S, D = x.shape
    row = lambda b, s: (b, s, 0)
    vec = lambda b, s: (b, 0, 0)
    return pl.pallas_call(
        _inproj_kernel,
        grid=(B, S // tm),
        in_specs=[pl.BlockSpec((1, tm, D), row),
                  pl.BlockSpec((1, 1, D), vec),
                  pl.BlockSpec((1, 1, D), vec),
                  pl.BlockSpec((1, D), lambda b, s: (0, 0)),
                  pl.BlockSpec(w_pad.shape, lambda b, s: (0, 0))],
        out_specs=[pl.BlockSpec((1, tm, n), row) for n in _IN_COLS],
        out_shape=[jax.ShapeDtypeStruct((B, S, n), F32) for n in _IN_COLS],
        compiler_params=_cp(("parallel", "parallel")),
        name="inproj",
    )(x, sc, sh, g, w_pad)


def _rwkv_kernel(z_ref, mu_ref, w0_ref, a0_ref, wl_ref, kk_ref, ka_ref, rk_ref, lnw_ref, lnb_ref,
                 hsum_ref, tri_ref, o_ref, zprev_ref, h_ref, y_ref):
    W = 256
    NH = W // RWKV_HEAD
    L = RWKV_CHUNK
    TL = z_ref.shape[1]

    @pl.when(pl.program_id(1) == 0)
    def _():
        zprev_ref[...] = jnp.zeros_like(zprev_ref)
        h_ref[...] = jnp.zeros_like(h_ref)

    z = z_ref[0]
    rows = lax.broadcasted_iota(jnp.int32, z.shape, 0)
    zp = jnp.where(rows == 0, zprev_ref[...], pltpu.roll(z, 1, axis=0))
    zprev_ref[...] = z[TL - 1:TL, :]
    zs = z + (zp - z) * mu_ref[...]
    r, k, v, lo = zs[:, 0:W], zs[:, W:2 * W], zs[:, 2 * W:3 * W], zs[:, 3 * W:3 * W + LANE]
    lane = lax.broadcasted_iota(jnp.int32, lo.shape, 1)
    act = jnp.where(lane < RWKV_W_RANK, jnp.tanh(lo),
                    jnp.where(lane < RWKV_W_RANK + RWKV_A_RANK, lo, jax.nn.sigmoid(lo)))
    up = _dot(act.astype(BF16), wl_ref[...])
    w = -_softplus(-(w0_ref[...] + up[:, 0:W])) - 0.5
    ld = -jnp.exp(w)
    a = jax.nn.sigmoid(a0_ref[...] + up[:, W:2 * W])
    g = up[:, 2 * W:3 * W]
    hsum = hsum_ref[...]
    kk = k * kk_ref[...]
    kk = kk / jnp.maximum(jnp.sqrt(_dot_hi(kk * kk, hsum)), 1e-12)
    k2 = k * (1.0 + (a - 1.0) * ka_ref[...])
    kka = kk * a

    tri_incl = tri_ref[0] > 0.0
    tri_strict = tri_ref[1] > 0.0
    eye_w = (lax.broadcasted_iota(jnp.int32, (W, W), 0) == lax.broadcasted_iota(jnp.int32, (W, W), 1)).astype(F32)

    for j in range(TL // L):
        sl = slice(j * L, (j + 1) * L)
        ld_c = ld[sl]
        cs = _dot_hi(tri_ref[0], ld_c)
        cs_l = cs[L - 1:L, :]
        p_in = jnp.exp(cs)
        p_ex = jnp.exp(cs - ld_c)
        p_inv = jnp.exp(-cs)
        p_end = jnp.exp(cs_l - cs)
        pl_col = jnp.exp(_dot_nt_hi(eye_w, jnp.broadcast_to(cs_l, (RWKV_HEAD, W))))
        bt = -kk[sl] * p_ex
        rt = r[sl] * p_in
        a_s = kka[sl] * p_inv
        k_s = k2[sl] * p_inv
        a_e = kka[sl] * p_end
        k_e = k2[sl] * p_end
        v_c = v[sl]
        for h in range(NH):
            hs = slice(h * RWKV_HEAD, (h + 1) * RWKV_HEAD)
            x = jnp.concatenate([bt[:, hs], rt[:, hs]], axis=0).astype(BF16)
            ga = _dot_nt(x, a_s[:, hs].astype(BF16))
            gk = _dot_nt(x, k_s[:, hs].astype(BF16))
            m_ba = jnp.where(tri_strict, ga[:L], 0.0)
            m_ra = jnp.where(tri_incl, ga[L:], 0.0)
            m_bk = jnp.where(tri_strict, gk[:L], 0.0)
            m_rk = jnp.where(tri_incl, gk[L:], 0.0)
            h0 = h_ref[h]
            xh = _dot_hi(jnp.concatenate([bt[:, hs], rt[:, hs]], axis=0), h0)
            v_h = v_c[:, hs]
            u = xh[:L] + _dot(m_bk.astype(BF16), v_h.astype(BF16))
            n_pow = m_ba
            u = u + _dot_hi(n_pow, u)
            for _ in range(5):
                n_pow = _dot_hi(n_pow, n_pow)
                u = u + _dot_hi(n_pow, u)
            y_h = xh[L:] + _dot(m_ra.astype(BF16), u.astype(BF16)) + _dot(m_rk.astype(BF16), v_h.astype(BF16))
            y_ref[sl, hs] = y_h
            ak = jnp.concatenate([a_e[:, hs], k_e[:, hs]], axis=0)
            uv = jnp.concatenate([u, v_h], axis=0)
            ak_t = _dot_nt_hi(eye_w[:RWKV_HEAD, :RWKV_HEAD], ak)
            h_ref[h] = pl_col[hs, :] * h0 + _dot_hi(ak_t, uv)

    y = y_ref[...]
    inv_n = 1.0 / RWKV_HEAD
    mean = _dot_hi(y, hsum) * inv_n
    yc = y - mean
    var = _dot_hi(yc * yc, hsum) * inv_n
    yn = yc * lax.rsqrt(var + RWKV_LN_EPS) * lnw_ref[...] + lnb_ref[...]
    bonus = _dot_hi(r * k2 * rk_ref[...], hsum) * v
    o_ref[0] = ((yn + bonus) * g).astype(o_ref.dtype)


def _rwkv(z, mu, w0, w2, a0, a2, g2, k_k, k_a, r_k, ln_w, ln_b, tl):
    B, S, _ = z.shape
    W = 256
    L = RWKV_CHUNK
    wl = jnp.zeros((LANE, 3 * W), F32)
    wl = wl.at[0:RWKV_W_RANK, 0:W].set(w2)
    wl = wl.at[RWKV_W_RANK:RWKV_W_RANK + RWKV_A_RANK, W:2 * W].set(a2)
    wl = wl.at[RWKV_W_RANK + RWKV_A_RANK:LANE, 2 * W:3 * W].set(g2).astype(BF16)
    hid = np.arange(W) // RWKV_HEAD
    hsum = jnp.asarray((hid[:, None] == hid[None, :]).astype(np.float32))
    t = np.arange(L)
    tri = jnp.asarray(np.stack([(t[None, :] <= t[:, None]), (t[None, :] < t[:, None])]).astype(np.float32))
    row2 = lambda v_: v_.reshape(1, -1)
    const = lambda shape: pl.BlockSpec(shape, lambda b, s: (0,) * len(shape))
    return pl.pallas_call(
        _rwkv_kernel,
        grid=(B, S // tl),
        in_specs=[pl.BlockSpec((1, tl, _RWKV_IN), lambda b, s: (b, s, 0)),
                  const((1, _RWKV_IN)), const((1, W)), const((1, W)), const((LANE, 3 * W)),
                  const((1, W)), const((1, W)), const((1, W)), const((1, W)), const((1, W)),
                  const((W, W)), const((2, L, L))],
        out_specs=pl.BlockSpec((1, tl, W), lambda b, s: (b, s, 0)),
        out_shape=jax.ShapeDtypeStruct((B, S, W), BF16),
        scratch_shapes=[pltpu.VMEM((1, _RWKV_IN), F32),
                        pltpu.VMEM((W // RWKV_HEAD, RWKV_HEAD, RWKV_HEAD), F32),
                        pltpu.VMEM((tl, W), F32)],
        compiler_params=_cp(("parallel", "arbitrary")),
        name="rwkv7",
    )(z, row2(mu), row2(w0), row2(a0), wl, row2(k_k), row2(k_a), row2(r_k), row2(ln_w), row2(ln_b), hsum, tri)


def _s5_kernel(u_ref, wb_ref, are_ref, aim_ref, wc_ref, d_ref, gw_ref, gb_ref, ng_ref, o_ref,
               hre_ref, him_ref, sre_ref, sim_ref):
    nb = SUBLANE
    rows = u_ref.shape[0]
    P = are_ref.shape[1]

    @pl.when(pl.program_id(0) == 0)
    def _():
        sre_ref[...] = jnp.zeros_like(sre_ref)
        sim_ref[...] = jnp.zeros_like(sim_ref)

    u = u_ref[...]
    bu = _dot(u.astype(BF16), wb_ref[...])
    hre_ref[...] = bu[:, :P]
    him_ref[...] = bu[:, P:]
    a_re = jnp.broadcast_to(are_ref[...], (nb, P))
    a_im = jnp.broadcast_to(aim_ref[...], (nb, P))

    def step(t, carry):
        h_re, h_im = carry
        i = pl.multiple_of(t * nb, nb)
        n_re = a_re * h_re - a_im * h_im + hre_ref[pl.ds(i, nb), :]
        n_im = a_re * h_im + a_im * h_re + him_ref[pl.ds(i, nb), :]
        hre_ref[pl.ds(i, nb), :] = n_re
        him_ref[pl.ds(i, nb), :] = n_im
        return n_re, n_im

    h_re, h_im = lax.fori_loop(0, rows // nb, step, (sre_ref[...], sim_ref[...]))
    sre_ref[...] = h_re
    sim_ref[...] = h_im
    y = _dot(hre_ref[...].astype(BF16), wc_ref[0]) + _dot(him_ref[...].astype(BF16), wc_ref[1])
    y = jax.nn.gelu(y + d_ref[...] * u)
    y = y * jax.nn.sigmoid(_dot(y.astype(BF16), gw_ref[...]) + gb_ref[...])
    o_ref[...] = _rms(y, ng_ref[...]).astype(o_ref.dtype)


def _s5(u_tm, lam_re, lam_im, b_re, b_im, c_re, c_im, d, log_dt, glu_w, glu_b, norm_g, ts):
    rows_total, W = u_tm.shape
    G, P = lam_re.shape
    dt = jnp.exp(log_dt)[:, None]
    mag = jnp.exp(lam_re * dt)
    a_re = mag * jnp.cos(lam_im * dt)
    a_im = mag * jnp.sin(lam_im * dt)
    den = lam_re * lam_re + lam_im * lam_im
    q_re = ((a_re - 1.0) * lam_re + a_im * lam_im) / den
    q_im = (a_im * lam_re - (a_re - 1.0) * lam_im) / den
    bb_re = q_re[..., None] * b_re - q_im[..., None] * b_im
    bb_im = q_re[..., None] * b_im + q_im[..., None] * b_re
    eye = jnp.eye(G, dtype=F32)
    wb_re = jnp.einsum('gpi,gh->gihp', bb_re, eye).reshape(W, G * P)
    wb_im = jnp.einsum('gpi,gh->gihp', bb_im, eye).reshape(W, G * P)
    wb = jnp.concatenate([wb_re, wb_im], axis=1).astype(BF16)
    wc_re = jnp.einsum('gip,gh->gphi', c_re, eye).reshape(G * P, W)
    wc_im = jnp.einsum('gip,gh->gphi', -c_im, eye).reshape(G * P, W)
    wc = jnp.stack([wc_re, wc_im]).astype(BF16)
    GP = G * P
    rows = ts * SUBLANE
    const = lambda shape: pl.BlockSpec(shape, lambda s: (0,) * len(shape))
    return pl.pallas_call(
        _s5_kernel,
        grid=(rows_total // rows,),
        in_specs=[pl.BlockSpec((rows, W), lambda s: (s, 0)),
                  const((W, 2 * GP)), const((1, GP)), const((1, GP)), const((2, GP, W)),
                  const((1, W)), const((W, W)), const((1, W)), const((1, W))],
        out_specs=pl.BlockSpec((rows, W), lambda s: (s, 0)),
        out_shape=jax.ShapeDtypeStruct((rows_total, W), BF16),
        scratch_shapes=[pltpu.VMEM((rows, GP), F32), pltpu.VMEM((rows, GP), F32),
                        pltpu.VMEM((SUBLANE, GP), F32), pltpu.VMEM((SUBLANE, GP), F32)],
        compiler_params=_cp(("arbitrary",)),
        name="s5",
    )(u_tm, wb, a_re.reshape(1, GP), a_im.reshape(1, GP), wc, d.reshape(1, W), glu_w.astype(BF16),
      glu_b.reshape(1, W), norm_g.reshape(1, W))


def _lru_kernel(x_ref, gate_ref, cw_ref, cb_ref, wax_ref, bax_ref, lam_ref, ng_ref, o_ref,
                xc_ref, a_ref, b_ref, hs_ref):
    nb = SUBLANE
    rows, W = x_ref.shape
    halo = (CONV_WIDTH - 1) * nb

    @pl.when(pl.program_id(0) == 0)
    def _():
        xc_ref[...] = jnp.zeros_like(xc_ref)
        hs_ref[...] = jnp.zeros_like(hs_ref)

    x = x_ref[...]
    xf = jnp.concatenate([xc_ref[...], x], axis=0)
    xc_ref[...] = x[rows - halo:, :]
    xc = cb_ref[...]
    for j in range(CONV_WIDTH):
        xc = xc + cw_ref[j:j + 1, :] * xf[j * nb:j * nb + rows, :]
    gates = _dot(xc.astype(BF16), wax_ref[...]) + bax_ref[...]
    r = jax.nn.sigmoid(gates[:, :W])
    i = jax.nn.sigmoid(gates[:, W:])
    log_a = -LRU_C * r * _softplus(-lam_ref[...])
    a = jnp.exp(log_a)
    a_ref[...] = a
    b_ref[...] = jnp.sqrt(1.0 - jnp.exp(2.0 * log_a)) * (i * xc)

    def step(t, h):
        i0 = pl.multiple_of(t * nb, nb)
        h = a_ref[pl.ds(i0, nb), :] * h + b_ref[pl.ds(i0, nb), :]
        b_ref[pl.ds(i0, nb), :] = h
        return h

    hs_ref[...] = lax.fori_loop(0, rows // nb, step, hs_ref[...])
    y = b_ref[...] * jax.nn.gelu(gate_ref[...])
    o_ref[...] = _rms(y, ng_ref[...]).astype(o_ref.dtype)


def _lru(x_tm, gate_tm, conv_w, conv_b, w_a, b_a, w_x, b_x, lam, norm_g, ts):
    rows_total, W = x_tm.shape
    nblk, bs, _ = w_a.shape
    eye = jnp.eye(nblk, dtype=F32)
    bd = lambda w_: jnp.einsum('nkj,nm->nkmj', w_, eye).reshape(W, W)
    wax = jnp.concatenate([bd(w_a), bd(w_x)], axis=1).astype(BF16)
    bax = jnp.concatenate([b_a, b_x]).reshape(1, 2 * W)
    rows = ts * SUBLANE
    const = lambda shape: pl.BlockSpec(shape, lambda s: (0,) * len(shape))
    return pl.pallas_call(
        _lru_kernel,
        grid=(rows_total // rows,),
        in_specs=[pl.BlockSpec((rows, W), lambda s: (s, 0)), pl.BlockSpec((rows, W), lambda s: (s, 0)),
                  const((CONV_WIDTH, W)), const((1, W)), const((W, 2 * W)), const((1, 2 * W)),
                  const((1, W)), const((1, W))],
        out_specs=pl.BlockSpec((rows, W), lambda s: (s, 0)),
        out_shape=jax.ShapeDtypeStruct((rows_total, W), BF16),
        scratch_shapes=[pltpu.VMEM(((CONV_WIDTH - 1) * SUBLANE, W), F32), pltpu.VMEM((rows, W), F32),
                        pltpu.VMEM((rows, W), F32), pltpu.VMEM((SUBLANE, W), F32)],
        compiler_params=_cp(("arbitrary",)),
        name="rglru",
    )(x_tm, gate_tm, conv_w, conv_b.reshape(1, W), wax, bax, lam.reshape(1, W), norm_g.reshape(1, W))


def _mla_prep_kernel(pos_ref, qa_ref, kva_ref, kpe_ref, gq_ref, gkv_ref, wq_ref, wk_ref, wv_ref,
                     gqh_ref, gkh_ref, freq_ref, q_ref, k_ref, v_ref):
    tm = qa_ref.shape[1]
    b = pl.program_id(0)
    s0 = pl.program_id(1) * tm
    pos = (pos_ref[b] + s0 + lax.broadcasted_iota(jnp.int32, (tm, LANE), 0)).astype(F32)
    ang = pos * freq_ref[...]
    cos_t = jnp.cos(ang)
    sin_t = jnp.sin(ang)
    lane = lax.broadcasted_iota(jnp.int32, (tm, LANE), 1)
    half = MLA_ROPE // 2
    first = (lane >= MLA_NOPE) & (lane < MLA_NOPE + half)
    second = (lane >= MLA_NOPE + half) & (lane < MLA_QK)

    def rope(t):
        rot = jnp.where(first, -pltpu.roll(t, LANE - half, axis=1),
                        jnp.where(second, pltpu.roll(t, half, axis=1), 0.0))
        return t * cos_t + rot * sin_t

    def head_norm(t, g):
        ms = jnp.sum(t * t, axis=-1, keepdims=True) * (1.0 / MLA_QK)
        return t * lax.rsqrt(ms + NORM_EPS) * g

    qn = _rms(qa_ref[0], gq_ref[...]).astype(BF16)
    kvn = _rms(kva_ref[0], gkv_ref[...]).astype(BF16)
    q_all = _dot(qn, wq_ref[...])
    k_all = _dot(kvn, wk_ref[...])
    v_all = _dot(kvn, wv_ref[...])
    kpe = kpe_ref[0]
    scale = MLA_QK ** -0.5
    for h in range(MLA_HEADS):
        hs = slice(h * LANE, (h + 1) * LANE)
        q_ref[0, h] = (rope(head_norm(q_all[:, hs], gqh_ref[...])) * scale).astype(q_ref.dtype)
        k_ref[0, h] = rope(head_norm(k_all[:, hs] + kpe, gkh_ref[...])).astype(k_ref.dtype)
        v_ref[0, h] = v_all[:, hs].astype(v_ref.dtype)


def _mla_prep(q_a, kv_a, kpe, pos_offset, q_norm_g, w_uq, kv_norm_g, w_ukv, q_head_g, k_head_g, tm):
    B, S, QR = q_a.shape
    KVR = kv_a.shape[2]
    H = MLA_HEADS
    wq = jnp.zeros((QR, H, LANE), F32).at[:, :, :MLA_QK].set(w_uq.reshape(QR, H, MLA_QK)).reshape(QR, H * LANE)
    wkv = w_ukv.reshape(KVR, H, MLA_NOPE + MLA_V)
    wk = jnp.zeros((KVR, H, LANE), F32).at[:, :, :MLA_NOPE].set(wkv[:, :, :MLA_NOPE]).reshape(KVR, H * LANE)
    wv = jnp.zeros((KVR, H, LANE), F32).at[:, :, :MLA_V].set(wkv[:, :, MLA_NOPE:]).reshape(KVR, H * LANE)
    pad_g = lambda g: jnp.zeros((1, LANE), F32).at[0, :MLA_QK].set(g)
    half = MLA_ROPE // 2
    inv_freq = np.power(np.float32(ROPE_THETA), -np.arange(half, dtype=np.float32) * np.float32(2.0) / np.float32(MLA_ROPE))
    freq = np.zeros((1, LANE), np.float32)
    freq[0, MLA_NOPE:MLA_NOPE + half] = inv_freq
    freq[0, MLA_NOPE + half:MLA_QK] = inv_freq
    const = lambda shape: pl.BlockSpec(shape, lambda b, s, p: (0,) * len(shape))
    row = lambda n: pl.BlockSpec((1, tm, n), lambda b, s, p: (b, s, 0))
    hout = pl.BlockSpec((1, H, tm, LANE), lambda b, s, p: (b, 0, s, 0))
    grid_spec = pltpu.PrefetchScalarGridSpec(
        num_scalar_prefetch=1, grid=(B, S // tm),
        in_specs=[row(QR), row(KVR), row(LANE), const((1, QR)), const((1, KVR)),
                  const((QR, H * LANE)), const((KVR, H * LANE)), const((KVR, H * LANE)),
                  const((1, LANE)), const((1, LANE)), const((1, LANE))],
        out_specs=[hout, hout, hout])
    return pl.pallas_call(
        _mla_prep_kernel,
        grid_spec=grid_spec,
        out_shape=[jax.ShapeDtypeStruct((B, H, S, LANE), BF16)] * 3,
        compiler_params=_cp(("parallel", "parallel")),
        name="mla_prep",
    )(pos_offset, q_a, kv_a, kpe, q_norm_g.reshape(1, QR), kv_norm_g.reshape(1, KVR),
      wq.astype(BF16), wk.astype(BF16), wv.astype(BF16), pad_g(q_head_g), pad_g(k_head_g), jnp.asarray(freq))


def _attn_kernel(q_ref, k_ref, v_ref, ng_ref, o_ref, m_ref, l_ref, acc_ref):
    H = q_ref.shape[1]
    tq = q_ref.shape[2]
    qi = pl.program_id(1)
    ki = pl.program_id(2)

    @pl.when(ki == 0)
    def _():
        m_ref[...] = jnp.full_like(m_ref, -1e30)
        l_ref[...] = jnp.zeros_like(l_ref)
        acc_ref[...] = jnp.zeros_like(acc_ref)

    def update(masked):
        if masked:
            qc = lax.broadcasted_iota(jnp.int32, (tq, tq), 0) // ATTN_CHUNK
            kc = lax.broadcasted_iota(jnp.int32, (tq, tq), 1) // ATTN_CHUNK
            visible = kc <= qc
        for h in range(H):
            s = _dot_nt(q_ref[0, h], k_ref[0, h])
            if masked:
                s = jnp.where(visible, s, -1e30)
            m_old = m_ref[h]
            m_new = jnp.maximum(m_old, jnp.max(s, axis=-1, keepdims=True))
            alpha = jnp.exp(m_old - m_new)
            p = jnp.exp(s - m_new)
            l_ref[h] = alpha * l_ref[h] + jnp.sum(p, axis=-1, keepdims=True)
            acc_ref[h] = alpha * acc_ref[h] + _dot(p.astype(BF16), v_ref[0, h])
            m_ref[h] = m_new

    @pl.when(ki < qi)
    def _():
        update(False)

    @pl.when(ki == qi)
    def _():
        update(True)
        o = jnp.concatenate([(acc_ref[h] / l_ref[h])[:, :MLA_V] for h in range(H)], axis=-1)
        o_ref[0] = _rms(o, ng_ref[...]).astype(o_ref.dtype)


def _attention(q, k, v, norm_g, tq):
    B, H, S, _ = q.shape
    W = H * MLA_V
    nq = S // tq
    qspec = pl.BlockSpec((1, H, tq, LANE), lambda b, i, j: (b, 0, i, 0))
    kspec = pl.BlockSpec((1, H, tq, LANE), lambda b, i, j: (b, 0, jnp.minimum(i, j), 0))
    return pl.pallas_call(
        _attn_kernel,
        grid=(B, nq, nq),
        in_specs=[qspec, kspec, kspec, pl.BlockSpec((1, W), lambda b, i, j: (0, 0))],
        out_specs=pl.BlockSpec((1, tq, W), lambda b, i, j: (b, i, 0)),
        out_shape=jax.ShapeDtypeStruct((B, S, W), BF16),
        scratch_shapes=[pltpu.VMEM((H, tq, 1), F32), pltpu.VMEM((H, tq, 1), F32),
                        pltpu.VMEM((H, tq, LANE), F32)],
        compiler_params=_cp(("parallel", "parallel", "arbitrary")),
        name="mla_attn",
    )(q, k, v, norm_g.reshape(1, W))


def _outproj_kernel(x_ref, ya_ref, yb_ref, yc_ref, yd_ref, wo_ref, gt_ref, sc_ref, sh_ref, g_ref,
                    wr_ref, br_ref, xo_ref, h_ref, lg_ref):
    acc = _dot(ya_ref[0], wo_ref[0])
    acc = acc + _dot(yb_ref[0], wo_ref[1])
    acc = acc + _dot(yc_ref[0], wo_ref[2])
    acc = acc + _dot(yd_ref[0], wo_ref[3])
    x = x_ref[0] + gt_ref[0] * acc
    xo_ref[0] = x
    h = _rms(x, g_ref[...]) * (1.0 + sc_ref[0]) + sh_ref[0]
    h_ref[0] = h
    lg_ref[0] = _dot_hi(h, wr_ref[...]) + br_ref[...]


def _outproj(x, ya, yb, yc, yd, w_out, gt, sc, sh, g, w_router, b_router, tm):
    B, S, D = x.shape
    W = D // N_MIX
    row = lambda n: pl.BlockSpec((1, tm, n), lambda b, s: (b, s, 0))
    vec = pl.BlockSpec((1, 1, D), lambda b, s: (b, 0, 0))
    const = lambda shape: pl.BlockSpec(shape, lambda b, s: (0,) * len(shape))
    return pl.pallas_call(
        _outproj_kernel,
        grid=(B, S // tm),
        in_specs=[row(D), row(W), row(W), row(W), row(W), const((N_MIX, W, D)), vec, vec, vec,
                  const((1, D)), const((D, LANE)), const((1, LANE))],
        out_specs=[row(D), row(D), row(LANE)],
        out_shape=[jax.ShapeDtypeStruct((B, S, D), F32), jax.ShapeDtypeStruct((B, S, D), F32),
                   jax.ShapeDtypeStruct((B, S, LANE), F32)],
        compiler_params=_cp(("parallel", "parallel")),
        name="outproj",
    )(x, ya, yb, yc, yd, w_out.reshape(N_MIX, W, D).astype(BF16), gt, sc, sh, g, w_router, b_router)


def _moe_kernel(be_ref, nused_ref, tok_ref, dst_ref, w_ref, h_hbm, w1_ref, w3_ref, w2_ref, o_hbm,
                xbuf, ybuf, sems):
    i = pl.program_id(0)
    R = MOE_BLOCK

    @pl.when(i == 0)
    def _():
        ybuf[...] = jnp.zeros_like(ybuf)
        pad = pltpu.make_async_copy(ybuf, o_hbm.at[pl.ds(o_hbm.shape[0] - R, R)], sems.at[1])
        pad.start()
        pad.wait()

    @pl.when(i < nused_ref[0])
    def _():
        def gather(r):
            return pltpu.make_async_copy(h_hbm.at[pl.ds(tok_ref[0, 0, r], 1)], xbuf.at[pl.ds(r, 1)], sems.at[0])

        def scatter(r):
            return pltpu.make_async_copy(ybuf.at[pl.ds(r, 1)], o_hbm.at[pl.ds(dst_ref[0, 0, r], 1)], sems.at[1])

        def start_g(r, c):
            gather(r).start()
            return c

        def wait_g(r, c):
            gather(r).wait()
            return c

        lax.fori_loop(0, R, start_g, 0)
        lax.fori_loop(0, R, wait_g, 0)
        xb = xbuf[...].astype(BF16)
        hid = _dot(xb, w1_ref[0])
        hid = hid * jax.nn.sigmoid(hid) * _dot(xb, w3_ref[0])
        y = _dot(hid.astype(BF16), w2_ref[0])
        eye_r = (lax.broadcasted_iota(jnp.int32, (R, R), 0) == lax.broadcasted_iota(jnp.int32, (R, R), 1)).astype(F32)
        wcol = _dot_nt_hi(eye_r, jnp.broadcast_to(w_ref[0], (LANE, R)))
        ybuf[...] = y * wcol[:, 0:1]

        def start_s(r, c):
            scatter(r).start()
            return c

        def wait_s(r, c):
            scatter(r).wait()
            return c

        lax.fori_loop(0, R, start_s, 0)
        lax.fori_loop(0, R, wait_s, 0)


def _moe_experts(h2, blk_expert, n_used, row_tok, row_dst, row_w, w1, w3, w2, n_out_rows):
    T, D = h2.shape
    E, _, F = w1.shape
    nb = blk_expert.shape[0]
    R = MOE_BLOCK
    blk = lambda i, be, nu: (i, 0, 0)
    wsel = lambda i, be, nu: (be[i], 0, 0)
    grid_spec = pltpu.PrefetchScalarGridSpec(
        num_scalar_prefetch=2, grid=(nb,),
        in_specs=[pl.BlockSpec((1, 1, R), blk, memory_space=pltpu.SMEM),
                  pl.BlockSpec((1, 1, R), blk, memory_space=pltpu.SMEM),
                  pl.BlockSpec((1, 1, R), blk),
                  pl.BlockSpec(memory_space=pl.ANY),
                  pl.BlockSpec((1, D, F), wsel), pl.BlockSpec((1, D, F), wsel), pl.BlockSpec((1, F, D), wsel)],
        out_specs=pl.BlockSpec(memory_space=pl.ANY),
        scratch_shapes=[pltpu.VMEM((R, D), F32), pltpu.VMEM((R, D), F32), pltpu.SemaphoreType.DMA((2,))])
    return pl.pallas_call(
        _moe_kernel,
        grid_spec=grid_spec,
        out_shape=jax.ShapeDtypeStruct((n_out_rows, D), F32),
        compiler_params=_cp(("arbitrary",)),
        name="moe_experts",
    )(blk_expert, n_used, row_tok.reshape(nb, 1, R), row_dst.reshape(nb, 1, R), row_w.reshape(nb, 1, R),
      h2, w1, w3, w2)


def _route(logits, T):
    p_group = jax.nn.softmax(logits[:, :N_GROUPS], axis=-1)
    g_sel = jnp.argmax(p_group, axis=-1).astype(jnp.int32)
    g_prob = jnp.max(p_group, axis=-1)
    logit_e = logits[:, N_GROUPS:N_GROUPS + N_EXPERTS].reshape(T, N_GROUPS, EXPERTS_PER_GROUP)
    logit_e = jnp.take_along_axis(logit_e, g_sel[:, None, None], axis=1)[:, 0]
    p_e = jax.nn.softmax(logit_e, axis=-1)
    top_p, top_i = lax.top_k(p_e, TOP_K)
    gate = (g_prob[:, None] * top_p / jnp.sum(top_p, axis=-1, keepdims=True)).reshape(-1)
    expert = (g_sel[:, None] * EXPERTS_PER_GROUP + top_i.astype(jnp.int32)).reshape(-1)
    n_assign = T * TOP_K
    n_blocks = -(-n_assign // MOE_BLOCK) + N_EXPERTS
    n_rows = n_blocks * MOE_BLOCK
    order = jnp.argsort(expert).astype(jnp.int32)
    e_s = expert[order]
    counts = jnp.sum((expert[:, None] == jnp.arange(N_EXPERTS, dtype=jnp.int32)[None, :]).astype(jnp.int32), axis=0)
    start = jnp.cumsum(counts) - counts
    pcounts = (counts + MOE_BLOCK - 1) // MOE_BLOCK * MOE_BLOCK
    pend = jnp.cumsum(pcounts)
    pstart = pend - pcounts
    dest = pstart[e_s] + (jnp.arange(n_assign, dtype=jnp.int32) - start[e_s])
    row_tok = jnp.zeros((n_rows,), jnp.int32).at[dest].set(order // TOP_K)
    pad_dst = n_assign + (jnp.arange(n_rows, dtype=jnp.int32) % MOE_BLOCK)
    row_dst = pad_dst.at[dest].set(order)
    row_w = jnp.zeros((n_rows,), F32).at[dest].set(gate[order])
    blk_start = jnp.arange(n_blocks, dtype=jnp.int32) * MOE_BLOCK
    blk_expert = jnp.minimum(jnp.searchsorted(pend, blk_start, side='right'), N_EXPERTS - 1).astype(jnp.int32)
    n_used = (pend[-1] // MOE_BLOCK).astype(jnp.int32).reshape(1)
    return blk_expert, n_used, row_tok, row_dst, row_w


def _combine_kernel(x_ref, y_ref, gt_ref, o_ref):
    D = x_ref.shape[2]
    y = y_ref[...]
    o_ref[0] = x_ref[0] + gt_ref[0] * (y[:, :D] + y[:, D:])


def _combine(x, y2, gt, tm):
    B, S, D = x.shape
    nt = S // tm
    return pl.pallas_call(
        _combine_kernel,
        grid=(B, nt),
        in_specs=[pl.BlockSpec((1, tm, D), lambda b, s: (b, s, 0)),
                  pl.BlockSpec((tm, TOP_K * D), lambda b, s: (b * nt + s, 0)),
                  pl.BlockSpec((1, 1, D), lambda b, s: (b, 0, 0))],
        out_specs=pl.BlockSpec((1, tm, D), lambda b, s: (b, s, 0)),
        out_shape=jax.ShapeDtypeStruct((B, S, D), F32),
        compiler_params=_cp(("parallel", "parallel")),
        name="moe_combine",
    )(x, y2, gt)


def _tiles(S):
    big = S >= 2048
    return dict(tm=512 if big else 256, tl=256, ts_s5=64, ts_lru=128 if big else 64,
                tq=1024 if big else 256)


def kernel(x, c, pos_offset, ada_w, ada_b, norm1_g, w_in, rwkv_mu, rwkv_w0, rwkv_w2, rwkv_a0, rwkv_a2, rwkv_g2, rwkv_k_k, rwkv_k_a, rwkv_r_k, rwkv_ln_w, rwkv_ln_b, s5_lambda_re, s5_lambda_im, s5_b_re, s5_b_im, s5_c_re, s5_c_im, s5_d, s5_log_dt, s5_glu_w, s5_glu_b, mla_q_norm_g, mla_w_uq, mla_kv_norm_g, mla_w_ukv, mla_q_head_g, mla_k_head_g, lru_conv_w, lru_conv_b, lru_w_a, lru_b_a, lru_w_x, lru_b_x, lru_lambda, branch_norm_g, w_out, norm2_g, moe_w_group, moe_b_group, moe_w_expert, moe_b_expert, moe_w1, moe_w3, moe_w2):
    B, S, D = x.shape
    depth = ada_w.shape[0]
    T = B * S
    W = D // N_MIX
    assert B == SUBLANE, "time-major scans put the batch on the 8 sublanes"
    tiles = _tiles(S)
    mod = _adaln_mod(c, ada_w, ada_b)

    def to_tm(t):
        return jnp.swapaxes(t, 0, 1).reshape(S * B, t.shape[-1])

    def from_tm(t):
        return jnp.swapaxes(t.reshape(S, B, t.shape[-1]), 0, 1)

    n_assign = T * TOP_K
    for l in range(depth):
        sh1, sc1, gt1, sh2, sc2, gt2 = [mod[l, :, None, i * D:(i + 1) * D] for i in range(6)]
        z_rwkv, z_s5, q_a, kv_a, kpe, z_lru, z_gate = _inproj(
            x, sc1, sh1, norm1_g[l].reshape(1, D), _pad_w_in(w_in[l]), tiles['tm'])
        y_a = _rwkv(z_rwkv, rwkv_mu[l], rwkv_w0[l], rwkv_w2[l], rwkv_a0[l], rwkv_a2[l], rwkv_g2[l],
                    rwkv_k_k[l], rwkv_k_a[l], rwkv_r_k[l].reshape(-1), rwkv_ln_w[l], rwkv_ln_b[l], tiles['tl'])
        y_b = from_tm(_s5(to_tm(z_s5), s5_lambda_re[l], s5_lambda_im[l], s5_b_re[l], s5_b_im[l], s5_c_re[l],
                          s5_c_im[l], s5_d[l], s5_log_dt[l], s5_glu_w[l], s5_glu_b[l], branch_norm_g[l, 0],
                          tiles['ts_s5']))
        q, k, v = _mla_prep(q_a, kv_a, kpe, pos_offset, mla_q_norm_g[l], mla_w_uq[l], mla_kv_norm_g[l],
                            mla_w_ukv[l], mla_q_head_g[l], mla_k_head_g[l], tiles['tm'])
        y_c = _attention(q, k, v, branch_norm_g[l, 1], tiles['tq'])
        y_d = from_tm(_lru(to_tm(z_lru), to_tm(z_gate), lru_conv_w[l], lru_conv_b[l], lru_w_a[l], lru_b_a[l],
                           lru_w_x[l], lru_b_x[l], lru_lambda[l], branch_norm_g[l, 2], tiles['ts_lru']))
        w_router = jnp.zeros((D, LANE), F32).at[:, :N_GROUPS].set(moe_w_group[l])
        w_router = w_router.at[:, N_GROUPS:N_GROUPS + N_EXPERTS].set(moe_w_expert[l])
        b_router = jnp.zeros((1, LANE), F32).at[0, :N_GROUPS].set(moe_b_group[l])
        b_router = b_router.at[0, N_GROUPS:N_GROUPS + N_EXPERTS].set(moe_b_expert[l])
        x, h2, logits = _outproj(x, y_a, y_b, y_c, y_d, w_out[l], gt1, sc2, sh2, norm2_g[l].reshape(1, D),
                                 w_router, b_router, tiles['tm'])
        blk_expert, n_used, row_tok, row_dst, row_w = _route(logits.reshape(T, LANE), T)
        y2 = _moe_experts(h2.reshape(T, D), blk_expert, n_used, row_tok, row_dst, row_w,
                          moe_w1[l].astype(BF16), moe_w3[l].astype(BF16), moe_w2[l].astype(BF16),
                          n_assign + MOE_BLOCK)
        x = _combine(x, y2.reshape((n_assign + MOE_BLOCK) // TOP_K, TOP_K * D), gt2, tiles['tm'])
    return x
```

```python
import functools
import math

import numpy as np
import jax
import jax.numpy as jnp
from jax import lax
from jax.experimental import pallas as pl
from jax.experimental.pallas import tpu as pltpu

F32 = jnp.float32
BF16 = jnp.bfloat16
HI = lax.Precision.HIGHEST

N_MIX = 4
RWKV_HEAD = 64
RWKV_W_RANK, RWKV_A_RANK, RWKV_G_RANK = 32, 32, 64
RWKV_LN_EPS = 64e-5
S5_GROUP, S5_STATE = 16, 64
MLA_HEADS, MLA_NOPE, MLA_ROPE, MLA_V = 4, 64, 32, 64
MLA_QK = MLA_NOPE + MLA_ROPE
ROPE_THETA = 10000.0
LRU_C = 8.0
CONV_WIDTH = 4
ATTN_CHUNK = 64
N_GROUPS, EXPERTS_PER_GROUP, TOP_K = 4, 8, 2
N_EXPERTS = N_GROUPS * EXPERTS_PER_GROUP
MOE_BLOCK = 256
NORM_EPS = 1e-6

LANE = 128
SUBLANE = 8
RWKV_CHUNK = 64
VMEM_LIMIT = 48 * 1024 * 1024


def _cp(sem, vmem=VMEM_LIMIT):
    return pltpu.CompilerParams(dimension_semantics=sem, vmem_limit_bytes=vmem)


def _dot(a, b):
    return jnp.dot(a, b, preferred_element_type=F32)


def _dot_hi(a, b):
    return jnp.dot(a, b, precision=HI, preferred_element_type=F32)


def _dot_nt(a, b):
    return lax.dot_general(a, b, (((1,), (1,)), ((), ())), preferred_element_type=F32)


def _dot_nt_hi(a, b):
    return lax.dot_general(a, b, (((1,), (1,)), ((), ())), precision=HI, preferred_element_type=F32)


def _softplus(x):
    return jnp.maximum(x, 0.0) + jnp.log1p(jnp.exp(-jnp.abs(x)))


def _rms(x, g, eps=NORM_EPS):
    return x * lax.rsqrt(jnp.mean(x * x, axis=-1, keepdims=True) + eps) * g


def _mod_kernel(c_ref, w_ref, b_ref, o_ref):
    c = c_ref[...]
    cond = c * jax.nn.sigmoid(c)
    o_ref[0] = _dot_hi(cond, w_ref[0]) + b_ref[0]


def _adaln_mod(c, ada_w, ada_b):
    L, D, D6 = ada_w.shape
    B = c.shape[0]
    nj = D6 // D
    return pl.pallas_call(
        _mod_kernel,
        grid=(L, nj),
        in_specs=[pl.BlockSpec((B, D), lambda l, j: (0, 0)),
                  pl.BlockSpec((1, D, D), lambda l, j: (l, 0, j)),
                  pl.BlockSpec((1, 1, D), lambda l, j: (l, 0, j))],
        out_specs=pl.BlockSpec((1, B, D), lambda l, j: (l, 0, j)),
        out_shape=jax.ShapeDtypeStruct((L, B, D6), F32),
        compiler_params=_cp(("parallel", "parallel")),
        name="adaln_mod",
    )(c, ada_w, ada_b.reshape(L, 1, D6))


_RWKV_IN = 3 * 256 + RWKV_W_RANK + RWKV_A_RANK + RWKV_G_RANK
_IN_COLS = (_RWKV_IN, 256, 256, 128, LANE, 256, 256)
_IN_OFFS = tuple(int(v) for v in np.cumsum((0,) + _IN_COLS))
_KPE_LANE0 = MLA_NOPE


def _pad_w_in(w_in):
    D = w_in.shape[0]
    o = np.cumsum((0, _RWKV_IN, 256, 256, 128, MLA_ROPE, 256, 256))
    pieces = [w_in[:, o[i]:o[i + 1]] for i in range(7)]
    kpe = jnp.zeros((D, LANE), w_in.dtype).at[:, _KPE_LANE0:_KPE_LANE0 + MLA_ROPE].set(pieces[4])
    pieces[4] = kpe
    return jnp.concatenate(pieces, axis=1).astype(BF16)


def _inproj_kernel(x_ref, sc_ref, sh_ref, g_ref, w_ref, *outs):
    x = x_ref[0]
    h = _rms(x, g_ref[...]) * (1.0 + sc_ref[0]) + sh_ref[0]
    z = _dot(h.astype(BF16), w_ref[...])
    for i, o_ref in enumerate(outs):
        o_ref[0] = z[:, _IN_OFFS[i]:_IN_OFFS[i + 1]]


def _inproj(x, sc, sh, g, w_pad, tm):
    B, S, D = x.shape
    row = lambda b, s: (b, s, 0)
    vec = lambda b, s: (b, 0, 0)
    return pl.pallas_call(
        _inproj_kernel,
        grid=(B, S // tm),
        in_specs=[pl.BlockSpec((1, tm, D), row),
                  pl.BlockSpec((1, 1, D), vec),
                  pl.BlockSpec((1, 1, D), vec),
                  pl.BlockSpec((1, D), lambda b, s: (0, 0)),
                  pl.BlockSpec(w_pad.shape, lambda b, s: (0, 0))],
        out_specs=[pl.BlockSpec((1, tm, n), row) for n in _IN_COLS],
        out_shape=[jax.ShapeDtypeStruct((B, S, n), F32) for n in _IN_COLS],
        compiler_params=_cp(("parallel", "parallel")),
        name="inproj",
    )(x, sc, sh, g, w_pad)


def _split_bf16(x, parts):
    out = []
    for _ in range(parts - 1):
        hi = x.astype(BF16)
        out.append(hi)
        x = x - hi.astype(F32)
    out.append(x.astype(BF16))
    return out


def _rwkv_kernel(z_ref, mu_ref, w0_ref, a0_ref, wl_ref, kk_ref, ka_ref, rk_ref, lnw_ref, lnb_ref,
                 hsum_ref, tri_ref, o_ref, zprev_ref, g_ref, y_ref):
    W = 256
    NH = W // RWKV_HEAD
    L = RWKV_CHUNK
    TL = z_ref.shape[1]

    @pl.when(pl.program_id(1) == 0)
    def _():
        zprev_ref[...] = jnp.zeros_like(zprev_ref)
        g_ref[...] = jnp.zeros_like(g_ref)

    hsum = hsum_ref[...]

    def head_sum(t):
        hi, lo_ = _split_bf16(t, 2)
        return _dot(hi, hsum) + _dot(lo_, hsum)

    z = z_ref[0]
    rows = lax.broadcasted_iota(jnp.int32, z.shape, 0)
    zp = jnp.where(rows == 0, zprev_ref[...], pltpu.roll(z, 1, axis=0))
    zprev_ref[...] = z[TL - 1:TL, :]
    zs = z + (zp - z) * mu_ref[...]
    r, k, v, lo = zs[:, 0:W], zs[:, W:2 * W], zs[:, 2 * W:3 * W], zs[:, 3 * W:3 * W + LANE]
    lane = lax.broadcasted_iota(jnp.int32, lo.shape, 1)
    act = jnp.where(lane < RWKV_W_RANK, jnp.tanh(lo),
                    jnp.where(lane < RWKV_W_RANK + RWKV_A_RANK, lo, jax.nn.sigmoid(lo)))
    up = _dot(act.astype(BF16), wl_ref[...])
    w = -_softplus(-(w0_ref[...] + up[:, 0:W])) - 0.5
    ld = -jnp.exp(w)
    a = jax.nn.sigmoid(a0_ref[...] + up[:, W:2 * W])
    g = up[:, 2 * W:3 * W]
    kk = k * kk_ref[...]
    kk = kk / jnp.maximum(jnp.sqrt(head_sum(kk * kk)), 1e-12)
    k2 = k * (1.0 + (a - 1.0) * ka_ref[...])
    kka = kk * a

    tri_b = tri_ref[0]
    tri_incl = tri_b > 0
    tri_strict = tri_ref[1] > 0
    eye = (lax.broadcasted_iota(jnp.int32, (RWKV_HEAD, RWKV_HEAD), 0)
           == lax.broadcasted_iota(jnp.int32, (RWKV_HEAD, RWKV_HEAD), 1)).astype(BF16)

    for j in range(TL // L):
        sl = slice(j * L, (j + 1) * L)
        ld_c = ld[sl]
        cs = sum(_dot(tri_b, p) for p in _split_bf16(ld_c, 3))
        cs_l = cs[L - 1:L, :]
        p_in = jnp.exp(cs)
        p_ex = jnp.exp(cs - ld_c)
        p_inv = jnp.exp(-cs)
        p_end = jnp.exp(cs_l - cs)
        p_l = jnp.exp(cs_l)
        bt = -kk[sl] * p_ex
        rt = r[sl] * p_in
        a_s = (kka[sl] * p_inv).astype(BF16)
        k_s = (k2[sl] * p_inv).astype(BF16)
        a_e = (kka[sl] * p_end).astype(BF16)
        k_e = (k2[sl] * p_end).astype(BF16)
        v_c = v[sl].astype(BF16)
        for h in range(NH):
            hs = slice(h * RWKV_HEAD, (h + 1) * RWKV_HEAD)
            x = jnp.concatenate([bt[:, hs], rt[:, hs]], axis=0).astype(BF16)
            ga = _dot_nt(x, a_s[:, hs])
            gk = _dot_nt(x, k_s[:, hs])
            m_ba = jnp.where(tri_strict, ga[:L], 0.0).astype(BF16)
            m_ra = jnp.where(tri_incl, ga[L:], 0.0).astype(BF16)
            m_bk = jnp.where(tri_strict, gk[:L], 0.0).astype(BF16)
            m_rk = jnp.where(tri_incl, gk[L:], 0.0).astype(BF16)
            g0 = g_ref[h]
            g_hi, g_lo = _split_bf16(g0, 2)
            xh = _dot_nt(x, g_hi) + _dot_nt(x, g_lo)
            v_h = v_c[:, hs]
            u = xh[:L] + _dot(m_bk, v_h)
            n_pow = m_ba
            u = u + _dot(n_pow, u.astype(BF16))
            for _ in range(5):
                n_pow = _dot(n_pow, n_pow).astype(BF16)
                u = u + _dot(n_pow, u.astype(BF16))
            u_b = u.astype(BF16)
            y_ref[sl, hs] = xh[L:] + _dot(m_ra, u_b) + _dot(m_rk, v_h)
            ak = jnp.concatenate([a_e[:, hs], k_e[:, hs]], axis=0)
            uv = jnp.concatenate([u_b, v_h], axis=0)
            uv_t = _dot_nt(eye, uv).astype(BF16)
            g_ref[h] = g0 * p_l[:, hs] + _dot(uv_t, ak)

    y = y_ref[...]
    inv_n = 1.0 / RWKV_HEAD
    mean = head_sum(y) * inv_n
    yc = y - mean
    var = head_sum(yc * yc) * inv_n
    yn = yc * lax.rsqrt(var + RWKV_LN_EPS) * lnw_ref[...] + lnb_ref[...]
    bonus = head_sum(r * k2 * rk_ref[...]) * v
    o_ref[0] = ((yn + bonus) * g).astype(o_ref.dtype)


def _rwkv(z, mu, w0, w2, a0, a2, g2, k_k, k_a, r_k, ln_w, ln_b, tl):
    B, S, _ = z.shape
    W = 256
    L = RWKV_CHUNK
    wl = jnp.zeros((LANE, 3 * W), F32)
    wl = wl.at[0:RWKV_W_RANK, 0:W].set(w2)
    wl = wl.at[RWKV_W_RANK:RWKV_W_RANK + RWKV_A_RANK, W:2 * W].set(a2)
    wl = wl.at[RWKV_W_RANK + RWKV_A_RANK:LANE, 2 * W:3 * W].set(g2).astype(BF16)
    hid = np.arange(W) // RWKV_HEAD
    hsum = jnp.asarray((hid[:, None] == hid[None, :]).astype(np.float32), dtype=BF16)
    t = np.arange(L)
    tri = jnp.asarray(np.stack([(t[None, :] <= t[:, None]), (t[None, :] < t[:, None])]).astype(np.float32), dtype=BF16)
    row2 = lambda v_: v_.reshape(1, -1)
    const = lambda shape: pl.BlockSpec(shape, lambda b, s: (0,) * len(shape))
    return pl.pallas_call(
        _rwkv_kernel,
        grid=(B, S // tl),
        in_specs=[pl.BlockSpec((1, tl, _RWKV_IN), lambda b, s: (b, s, 0)),
                  const((1, _RWKV_IN)), const((1, W)), const((1, W)), const((LANE, 3 * W)),
                  const((1, W)), const((1, W)), const((1, W)), const((1, W)), const((1, W)),
                  const((W, W)), const((2, L, L))],
        out_specs=pl.BlockSpec((1, tl, W), lambda b, s: (b, s, 0)),
        out_shape=jax.ShapeDtypeStruct((B, S, W), BF16),
        scratch_shapes=[pltpu.VMEM((1, _RWKV_IN), F32),
                        pltpu.VMEM((W // RWKV_HEAD, RWKV_HEAD, RWKV_HEAD), F32),
                        pltpu.VMEM((tl, W), F32)],
        compiler_params=_cp(("parallel", "arbitrary")),
        name="rwkv7",
    )(z, row2(mu), row2(w0), row2(a0), wl, row2(k_k), row2(k_a), row2(r_k), row2(ln_w), row2(ln_b), hsum, tri)


def _s5_kernel(u_ref, wb_ref, are_ref, aim_ref, wc_ref, d_ref, gw_ref, gb_ref, ng_ref, o_ref,
               hre_ref, him_ref, sre_ref, sim_ref):
    nb = SUBLANE
    rows = u_ref.shape[0]
    P = are_ref.shape[1]

    @pl.when(pl.program_id(0) == 0)
    def _():
        sre_ref[...] = jnp.zeros_like(sre_ref)
        sim_ref[...] = jnp.zeros_like(sim_ref)

    u = u_ref[...]
    bu = _dot(u.astype(BF16), wb_ref[...])
    hre_ref[...] = bu[:, :P]
    him_ref[...] = bu[:, P:]
    a_re = jnp.broadcast_to(are_ref[...], (nb, P))
    a_im = jnp.broadcast_to(aim_ref[...], (nb, P))

    def step(t, carry):
        h_re, h_im = carry
        i = pl.multiple_of(t * nb, nb)
        n_re = a_re * h_re - a_im * h_im + hre_ref[pl.ds(i, nb), :]
        n_im = a_re * h_im + a_im * h_re + him_ref[pl.ds(i, nb), :]
        hre_ref[pl.ds(i, nb), :] = n_re
        him_ref[pl.ds(i, nb), :] = n_im
        return n_re, n_im

    h_re, h_im = lax.fori_loop(0, rows // nb, step, (sre_ref[...], sim_ref[...]))
    sre_ref[...] = h_re
    sim_ref[...] = h_im
    y = _dot(hre_ref[...].astype(BF16), wc_ref[0]) + _dot(him_ref[...].astype(BF16), wc_ref[1])
    y = jax.nn.gelu(y + d_ref[...] * u)
    y = y * jax.nn.sigmoid(_dot(y.astype(BF16), gw_ref[...]) + gb_ref[...])
    o_ref[...] = _rms(y, ng_ref[...]).astype(o_ref.dtype)


def _s5(u_tm, lam_re, lam_im, b_re, b_im, c_re, c_im, d, log_dt, glu_w, glu_b, norm_g, ts):
    rows_total, W = u_tm.shape
    G, P = lam_re.shape
    dt = jnp.exp(log_dt)[:, None]
    mag = jnp.exp(lam_re * dt)
    a_re = mag * jnp.cos(lam_im * dt)
    a_im = mag * jnp.sin(lam_im * dt)
    den = lam_re * lam_re + lam_im * lam_im
    q_re = ((a_re - 1.0) * lam_re + a_im * lam_im) / den
    q_im = (a_im * lam_re - (a_re - 1.0) * lam_im) / den
    bb_re = q_re[..., None] * b_re - q_im[..., None] * b_im
    bb_im = q_re[..., None] * b_im + q_im[..., None] * b_re
    eye = jnp.eye(G, dtype=F32)
    wb_re = jnp.einsum('gpi,gh->gihp', bb_re, eye).reshape(W, G * P)
    wb_im = jnp.einsum('gpi,gh->gihp', bb_im, eye).reshape(W, G * P)
    wb = jnp.concatenate([wb_re, wb_im], axis=1).astype(BF16)
    wc_re = jnp.einsum('gip,gh->gphi', c_re, eye).reshape(G * P, W)
    wc_im = jnp.einsum('gip,gh->gphi', -c_im, eye).reshape(G * P, W)
    wc = jnp.stack([wc_re, wc_im]).astype(BF16)
    GP = G * P
    rows = ts * SUBLANE
    const = lambda shape: pl.BlockSpec(shape, lambda s: (0,) * len(shape))
    return pl.pallas_call(
        _s5_kernel,
        grid=(rows_total // rows,),
        in_specs=[pl.BlockSpec((rows, W), lambda s: (s, 0)),
                  const((W, 2 * GP)), const((1, GP)), const((1, GP)), const((2, GP, W)),
                  const((1, W)), const((W, W)), const((1, W)), const((1, W))],
        out_specs=pl.BlockSpec((rows, W), lambda s: (s, 0)),
        out_shape=jax.ShapeDtypeStruct((rows_total, W), BF16),
        scratch_shapes=[pltpu.VMEM((rows, GP), F32), pltpu.VMEM((rows, GP), F32),
                        pltpu.VMEM((SUBLANE, GP), F32), pltpu.VMEM((SUBLANE, GP), F32)],
        compiler_params=_cp(("arbitrary",)),
        name="s5",
    )(u_tm, wb, a_re.reshape(1, GP), a_im.reshape(1, GP), wc, d.reshape(1, W), glu_w.astype(BF16),
      glu_b.reshape(1, W), norm_g.reshape(1, W))


def _lru_kernel(x_ref, gate_ref, cw_ref, cb_ref, wax_ref, bax_ref, lam_ref, ng_ref, o_ref,
                xc_ref, a_ref, b_ref, hs_ref):
    nb = SUBLANE
    rows, W = x_ref.shape
    halo = (CONV_WIDTH - 1) * nb

    @pl.when(pl.program_id(0) == 0)
    def _():
        xc_ref[...] = jnp.zeros_like(xc_ref)
        hs_ref[...] = jnp.zeros_like(hs_ref)

    x = x_ref[...]
    xf = jnp.concatenate([xc_ref[...], x], axis=0)
    xc_ref[...] = x[rows - halo:, :]
    xc = cb_ref[...]
    for j in range(CONV_WIDTH):
        xc = xc + cw_ref[j:j + 1, :] * xf[j * nb:j * nb + rows, :]
    gates = _dot(xc.astype(BF16), wax_ref[...]) + bax_ref[...]
    r = jax.nn.sigmoid(gates[:, :W])
    i = jax.nn.sigmoid(gates[:, W:])
    log_a = -LRU_C * r * _softplus(-lam_ref[...])
    a = jnp.exp(log_a)
    a_ref[...] = a
    b_ref[...] = jnp.sqrt(1.0 - jnp.exp(2.0 * log_a)) * (i * xc)

    def step(t, h):
        i0 = pl.multiple_of(t * nb, nb)
        h = a_ref[pl.ds(i0, nb), :] * h + b_ref[pl.ds(i0, nb), :]
        b_ref[pl.ds(i0, nb), :] = h
        return h

    hs_ref[...] = lax.fori_loop(0, rows // nb, step, hs_ref[...])
    y = b_ref[...] * jax.nn.gelu(gate_ref[...])
    o_ref[...] = _rms(y, ng_ref[...]).astype(o_ref.dtype)


def _lru(x_tm, gate_tm, conv_w, conv_b, w_a, b_a, w_x, b_x, lam, norm_g, ts):
    rows_total, W = x_tm.shape
    nblk, bs, _ = w_a.shape
    eye = jnp.eye(nblk, dtype=F32)
    bd = lambda w_: jnp.einsum('nkj,nm->nkmj', w_, eye).reshape(W, W)
    wax = jnp.concatenate([bd(w_a), bd(w_x)], axis=1).astype(BF16)
    bax = jnp.concatenate([b_a, b_x]).reshape(1, 2 * W)
    rows = ts * SUBLANE
    const = lambda shape: pl.BlockSpec(shape, lambda s: (0,) * len(shape))
    return pl.pallas_call(
        _lru_kernel,
        grid=(rows_total // rows,),
        in_specs=[pl.BlockSpec((rows, W), lambda s: (s, 0)), pl.BlockSpec((rows, W), lambda s: (s, 0)),
                  const((CONV_WIDTH, W)), const((1, W)), const((W, 2 * W)), const((1, 2 * W)),
                  const((1, W)), const((1, W))],
        out_specs=pl.BlockSpec((rows, W), lambda s: (s, 0)),
        out_shape=jax.ShapeDtypeStruct((rows_total, W), BF16),
        scratch_shapes=[pltpu.VMEM(((CONV_WIDTH - 1) * SUBLANE, W), F32), pltpu.VMEM((rows, W), F32),
                        pltpu.VMEM((rows, W), F32), pltpu.VMEM((SUBLANE, W), F32)],
        compiler_params=_cp(("arbitrary",)),
        name="rglru",
    )(x_tm, gate_tm, conv_w, conv_b.reshape(1, W), wax, bax, lam.reshape(1, W), norm_g.reshape(1, W))


def _mla_prep_kernel(pos_ref, qa_ref, kva_ref, kpe_ref, gq_ref, gkv_ref, wq_ref, wk_ref, wv_ref,
                     gqh_ref, gkh_ref, freq_ref, q_ref, k_ref, v_ref):
    tm = qa_ref.shape[1]
    b = pl.program_id(0)
    s0 = pl.program_id(1) * tm
    pos = (pos_ref[b] + s0 + lax.broadcasted_iota(jnp.int32, (tm, LANE), 0)).astype(F32)
    ang = pos * freq_ref[...]
    cos_t = jnp.cos(ang)
    sin_t = jnp.sin(ang)
    lane = lax.broadcasted_iota(jnp.int32, (tm, LANE), 1)
    half = MLA_ROPE // 2
    first = (lane >= MLA_NOPE) & (lane < MLA_NOPE + half)
    second = (lane >= MLA_NOPE + half) & (lane < MLA_QK)

    def rope(t):
        rot = jnp.where(first, -pltpu.roll(t, LANE - half, axis=1),
                        jnp.where(second, pltpu.roll(t, half, axis=1), 0.0))
        return t * cos_t + rot * sin_t

    def head_norm(t, g):
        ms = jnp.sum(t * t, axis=-1, keepdims=True) * (1.0 / MLA_QK)
        return t * lax.rsqrt(ms + NORM_EPS) * g

    qn = _rms(qa_ref[0], gq_ref[...]).astype(BF16)
    kvn = _rms(kva_ref[0], gkv_ref[...]).astype(BF16)
    q_all = _dot(qn, wq_ref[...])
    k_all = _dot(kvn, wk_ref[...])
    v_all = _dot(kvn, wv_ref[...])
    kpe = kpe_ref[0]
    scale = MLA_QK ** -0.5
    for h in range(MLA_HEADS):
        hs = slice(h * LANE, (h + 1) * LANE)
        q_ref[0, h] = (rope(head_norm(q_all[:, hs], gqh_ref[...])) * scale).astype(q_ref.dtype)
        k_ref[0, h] = rope(head_norm(k_all[:, hs] + kpe, gkh_ref[...])).astype(k_ref.dtype)
        v_ref[0, h] = v_all[:, hs].astype(v_ref.dtype)


def _mla_prep(q_a, kv_a, kpe, pos_offset, q_norm_g, w_uq, kv_norm_g, w_ukv, q_head_g, k_head_g, tm):
    B, S, QR = q_a.shape
    KVR = kv_a.shape[2]
    H = MLA_HEADS
    wq = jnp.zeros((QR, H, LANE), F32).at[:, :, :MLA_QK].set(w_uq.reshape(QR, H, MLA_QK)).reshape(QR, H * LANE)
    wkv = w_ukv.reshape(KVR, H, MLA_NOPE + MLA_V)
    wk = jnp.zeros((KVR, H, LANE), F32).at[:, :, :MLA_NOPE].set(wkv[:, :, :MLA_NOPE]).reshape(KVR, H * LANE)
    wv = jnp.zeros((KVR, H, LANE), F32).at[:, :, :MLA_V].set(wkv[:, :, MLA_NOPE:]).reshape(KVR, H * LANE)
    pad_g = lambda g: jnp.zeros((1, LANE), F32).at[0, :MLA_QK].set(g)
    half = MLA_ROPE // 2
    inv_freq = np.power(np.float32(ROPE_THETA), -np.arange(half, dtype=np.float32) * np.float32(2.0) / np.float32(MLA_ROPE))
    freq = np.zeros((1, LANE), np.float32)
    freq[0, MLA_NOPE:MLA_NOPE + half] = inv_freq
    freq[0, MLA_NOPE + half:MLA_QK] = inv_freq
    const = lambda shape: pl.BlockSpec(shape, lambda b, s, p: (0,) * len(shape))
    row = lambda n: pl.BlockSpec((1, tm, n), lambda b, s, p: (b, s, 0))
    hout = pl.BlockSpec((1, H, tm, LANE), lambda b, s, p: (b, 0, s, 0))
    grid_spec = pltpu.PrefetchScalarGridSpec(
        num_scalar_prefetch=1, grid=(B, S // tm),
        in_specs=[row(QR), row(KVR), row(LANE), const((1, QR)), const((1, KVR)),
                  const((QR, H * LANE)), const((KVR, H * LANE)), const((KVR, H * LANE)),
                  const((1, LANE)), const((1, LANE)), const((1, LANE))],
        out_specs=[hout, hout, hout])
    return pl.pallas_call(
        _mla_prep_kernel,
        grid_spec=grid_spec,
        out_shape=[jax.ShapeDtypeStruct((B, H, S, LANE), BF16)] * 3,
        compiler_params=_cp(("parallel", "parallel")),
        name="mla_prep",
    )(pos_offset, q_a, kv_a, kpe, q_norm_g.reshape(1, QR), kv_norm_g.reshape(1, KVR),
      wq.astype(BF16), wk.astype(BF16), wv.astype(BF16), pad_g(q_head_g), pad_g(k_head_g), jnp.asarray(freq))


def _attn_kernel(q_ref, k_ref, v_ref, ng_ref, o_ref, m_ref, l_ref, acc_ref):
    H = q_ref.shape[1]
    tq = q_ref.shape[2]
    qi = pl.program_id(1)
    ki = pl.program_id(2)

    @pl.when(ki == 0)
    def _():
        m_ref[...] = jnp.full_like(m_ref, -1e30)
        l_ref[...] = jnp.zeros_like(l_ref)
        acc_ref[...] = jnp.zeros_like(acc_ref)

    def update(masked):
        if masked:
            qc = lax.broadcasted_iota(jnp.int32, (tq, tq), 0) // ATTN_CHUNK
            kc = lax.broadcasted_iota(jnp.int32, (tq, tq), 1) // ATTN_CHUNK
            visible = kc <= qc
        for h in range(H):
            s = _dot_nt(q_ref[0, h], k_ref[0, h])
            if masked:
                s = jnp.where(visible, s, -1e30)
            m_old = m_ref[h]
            m_new = jnp.maximum(m_old, jnp.max(s, axis=-1, keepdims=True))
            alpha = jnp.exp(m_old - m_new)
            p = jnp.exp(s - m_new)
            l_ref[h] = alpha * l_ref[h] + jnp.sum(p, axis=-1, keepdims=True)
            acc_ref[h] = alpha * acc_ref[h] + _dot(p.astype(BF16), v_ref[0, h])
            m_ref[h] = m_new

    @pl.when(ki < qi)
    def _():
        update(False)

    @pl.when(ki == qi)
    def _():
        update(True)
        o = jnp.concatenate([(acc_ref[h] / l_ref[h])[:, :MLA_V] for h in range(H)], axis=-1)
        o_ref[0] = _rms(o, ng_ref[...]).astype(o_ref.dtype)


def _attention(q, k, v, norm_g, tq):
    B, H, S, _ = q.shape
    W = H * MLA_V
    nq = S // tq
    qspec = pl.BlockSpec((1, H, tq, LANE), lambda b, i, j: (b, 0, i, 0))
    kspec = pl.BlockSpec((1, H, tq, LANE), lambda b, i, j: (b, 0, jnp.minimum(i, j), 0))
    return pl.pallas_call(
        _attn_kernel,
        grid=(B, nq, nq),
        in_specs=[qspec, kspec, kspec, pl.BlockSpec((1, W), lambda b, i, j: (0, 0))],
        out_specs=pl.BlockSpec((1, tq, W), lambda b, i, j: (b, i, 0)),
        out_shape=jax.ShapeDtypeStruct((B, S, W), BF16),
        scratch_shapes=[pltpu.VMEM((H, tq, 1), F32), pltpu.VMEM((H, tq, 1), F32),
                        pltpu.VMEM((H, tq, LANE), F32)],
        compiler_params=_cp(("parallel", "parallel", "arbitrary")),
        name="mla_attn",
    )(q, k, v, norm_g.reshape(1, W))


def _outproj_kernel(x_ref, ya_ref, yb_ref, yc_ref, yd_ref, wo_ref, gt_ref, sc_ref, sh_ref, g_ref,
                    wr_ref, br_ref, xo_ref, h_ref, lg_ref):
    acc = _dot(ya_ref[0], wo_ref[0])
    acc = acc + _dot(yb_ref[0], wo_ref[1])
    acc = acc + _dot(yc_ref[0], wo_ref[2])
    acc = acc + _dot(yd_ref[0], wo_ref[3])
    x = x_ref[0] + gt_ref[0] * acc
    xo_ref[0] = x
    h = _rms(x, g_ref[...]) * (1.0 + sc_ref[0]) + sh_ref[0]
    h_ref[0] = h
    lg_ref[0] = _dot_hi(h, wr_ref[...]) + br_ref[...]


def _outproj(x, ya, yb, yc, yd, w_out, gt, sc, sh, g, w_router, b_router, tm):
    B, S, D = x.shape
    W = D // N_MIX
    row = lambda n: pl.BlockSpec((1, tm, n), lambda b, s: (b, s, 0))
    vec = pl.BlockSpec((1, 1, D), lambda b, s: (b, 0, 0))
    const = lambda shape: pl.BlockSpec(shape, lambda b, s: (0,) * len(shape))
    return pl.pallas_call(
        _outproj_kernel,
        grid=(B, S // tm),
        in_specs=[row(D), row(W), row(W), row(W), row(W), const((N_MIX, W, D)), vec, vec, vec,
                  const((1, D)), const((D, LANE)), const((1, LANE))],
        out_specs=[row(D), row(D), row(LANE)],
        out_shape=[jax.ShapeDtypeStruct((B, S, D), F32), jax.ShapeDtypeStruct((B, S, D), F32),
                   jax.ShapeDtypeStruct((B, S, LANE), F32)],
        compiler_params=_cp(("parallel", "parallel")),
        name="outproj",
    )(x, ya, yb, yc, yd, w_out.reshape(N_MIX, W, D).astype(BF16), gt, sc, sh, g, w_router, b_router)


_MOE_PAD_ROWS = MOE_BLOCK // TOP_K


def _moe_kernel(be_ref, nused_ref, tok_ref, tokn_ref, dst_ref, w_ref, h_hbm, w1_ref, w3_ref, w2_ref, o_hbm,
                xbuf, ybuf, gsem, ssem):
    i = pl.program_id(0)
    R = MOE_BLOCK
    nused = nused_ref[0]
    slot = lax.rem(i, 2)
    other = 1 - slot

    def gather_rows(idx_ref, s):
        def body(r, c):
            pltpu.make_async_copy(h_hbm.at[pl.ds(idx_ref[0, 0, r], 1)], xbuf.at[s, pl.ds(r, 1)], gsem.at[s]).start()
            return c
        lax.fori_loop(0, R, body, 0, unroll=8)

    def scatter_rows(s):
        def body(r, c):
            a = dst_ref[0, 0, r]
            pltpu.make_async_copy(ybuf.at[s, pl.ds(r, 1)],
                                  o_hbm.at[a & 1, pl.ds(lax.shift_right_logical(a, 1), 1)], ssem.at[s]).start()
            return c
        lax.fori_loop(0, R, body, 0, unroll=8)

    def wait_gather(s):
        pltpu.make_async_copy(h_hbm.at[pl.ds(0, R)], xbuf.at[s], gsem.at[s]).wait()

    def wait_scatter(s):
        pltpu.make_async_copy(ybuf.at[s], o_hbm.at[0, pl.ds(0, R)], ssem.at[s]).wait()

    @pl.when(i == 0)
    def _():
        T = o_hbm.shape[1] - _MOE_PAD_ROWS
        ybuf[1] = jnp.zeros(ybuf.shape[1:], ybuf.dtype)
        for j in range(TOP_K):
            pad = pltpu.make_async_copy(ybuf.at[1, pl.ds(0, _MOE_PAD_ROWS)],
                                        o_hbm.at[j, pl.ds(T, _MOE_PAD_ROWS)], ssem.at[1])
            pad.start()
            pad.wait()
        gather_rows(tok_ref, 0)

    @pl.when(i < nused)
    def _():
        @pl.when(i + 1 < nused)
        def _():
            gather_rows(tokn_ref, other)

        wait_gather(slot)

        @pl.when(i >= 2)
        def _():
            wait_scatter(slot)

        xb = xbuf[slot].astype(BF16)
        hid = _dot(xb, w1_ref[0])
        hid = hid * jax.nn.sigmoid(hid) * _dot(xb, w3_ref[0])
        y = _dot(hid.astype(BF16), w2_ref[0])
        eye_r = (lax.broadcasted_iota(jnp.int32, (R, R), 0) == lax.broadcasted_iota(jnp.int32, (R, R), 1)).astype(F32)
        wcol = _dot_nt_hi(eye_r, jnp.broadcast_to(w_ref[0], (LANE, R)))
        ybuf[slot] = y * wcol[:, 0:1]
        scatter_rows(slot)

        @pl.when(i == nused - 1)
        def _():
            wait_scatter(slot)

            @pl.when(i >= 1)
            def _():
                wait_scatter(other)


def _moe_experts(h2, blk_expert, n_used, row_tok, row_dst, row_w, w1, w3, w2):
    T, D = h2.shape
    E, _, F = w1.shape
    nb = blk_expert.shape[0]
    R = MOE_BLOCK
    blk = lambda i, be, nu: (i, 0, 0)
    nxt = lambda i, be, nu: (jnp.minimum(i + 1, nb - 1), 0, 0)
    wsel = lambda i, be, nu: (be[i], 0, 0)
    row_tok = row_tok.reshape(nb, 1, R)
    grid_spec = pltpu.PrefetchScalarGridSpec(
        num_scalar_prefetch=2, grid=(nb,),
        in_specs=[pl.BlockSpec((1, 1, R), blk, memory_space=pltpu.SMEM),
                  pl.BlockSpec((1, 1, R), nxt, memory_space=pltpu.SMEM),
                  pl.BlockSpec((1, 1, R), blk, memory_space=pltpu.SMEM),
                  pl.BlockSpec((1, 1, R), blk),
                  pl.BlockSpec(memory_space=pl.ANY),
                  pl.BlockSpec((1, D, F), wsel), pl.BlockSpec((1, D, F), wsel), pl.BlockSpec((1, F, D), wsel)],
        out_specs=pl.BlockSpec(memory_space=pl.ANY),
        scratch_shapes=[pltpu.VMEM((2, R, D), F32), pltpu.VMEM((2, R, D), F32),
                        pltpu.SemaphoreType.DMA((2,)), pltpu.SemaphoreType.DMA((2,))])
    return pl.pallas_call(
        _moe_kernel,
        grid_spec=grid_spec,
        out_shape=jax.ShapeDtypeStruct((TOP_K, T + _MOE_PAD_ROWS, D), F32),
        compiler_params=_cp(("arbitrary",)),
        name="moe_experts",
    )(blk_expert, n_used, row_tok, row_tok, row_dst.reshape(nb, 1, R), row_w.reshape(nb, 1, R), h2, w1, w3, w2)


def _route(logits, T):
    p_group = jax.nn.softmax(logits[:, :N_GROUPS], axis=-1)
    g_sel = jnp.argmax(p_group, axis=-1).astype(jnp.int32)
    g_prob = jnp.max(p_group, axis=-1)
    logit_e = logits[:, N_GROUPS:N_GROUPS + N_EXPERTS].reshape(T, N_GROUPS, EXPERTS_PER_GROUP)
    logit_e = jnp.take_along_axis(logit_e, g_sel[:, None, None], axis=1)[:, 0]
    p_e = jax.nn.softmax(logit_e, axis=-1)
    top_p, top_i = lax.top_k(p_e, TOP_K)
    gate = (g_prob[:, None] * top_p / jnp.sum(top_p, axis=-1, keepdims=True)).reshape(-1)
    expert = (g_sel[:, None] * EXPERTS_PER_GROUP + top_i.astype(jnp.int32)).reshape(-1)
    n_assign = T * TOP_K
    n_blocks = -(-n_assign // MOE_BLOCK) + N_EXPERTS
    order = jnp.argsort(expert).astype(jnp.int32)
    counts = jnp.sum((expert[:, None] == jnp.arange(N_EXPERTS, dtype=jnp.int32)[None, :]).astype(jnp.int32), axis=0)
    start = jnp.cumsum(counts) - counts
    pcounts = (counts + MOE_BLOCK - 1) // MOE_BLOCK * MOE_BLOCK
    pend = jnp.cumsum(pcounts)
    pstart = pend - pcounts
    blk_start = jnp.arange(n_blocks, dtype=jnp.int32) * MOE_BLOCK
    blk_expert = jnp.minimum(jnp.sum((pend[None, :] <= blk_start[:, None]).astype(jnp.int32), axis=1), N_EXPERTS - 1)
    n_used = (pend[-1] // MOE_BLOCK).astype(jnp.int32).reshape(1)
    lane = jnp.arange(MOE_BLOCK, dtype=jnp.int32)[None, :]
    off = blk_start[:, None] + lane - pstart[blk_expert][:, None]
    valid = off < counts[blk_expert][:, None]
    pos = jnp.clip(start[blk_expert][:, None] + off, 0, n_assign - 1)
    asg = order[pos]
    row_tok = jnp.where(valid, asg // TOP_K, 0)
    row_dst = jnp.where(valid, asg, n_assign + lane)
    row_w = jnp.where(valid, gate[asg], 0.0)
    return blk_expert.astype(jnp.int32), n_used, row_tok, row_dst, row_w


def _combine_kernel(x_ref, y0_ref, y1_ref, gt_ref, o_ref):
    o_ref[0] = x_ref[0] + gt_ref[0] * (y0_ref[0] + y1_ref[0])


def _combine(x, y2, gt, tm):
    B, S, D = x.shape
    nt = S // tm
    return pl.pallas_call(
        _combine_kernel,
        grid=(B, nt),
        in_specs=[pl.BlockSpec((1, tm, D), lambda b, s: (b, s, 0)),
                  pl.BlockSpec((1, tm, D), lambda b, s: (0, b * nt + s, 0)),
                  pl.BlockSpec((1, tm, D), lambda b, s: (1, b * nt + s, 0)),
                  pl.BlockSpec((1, 1, D), lambda b, s: (b, 0, 0))],
        out_specs=pl.BlockSpec((1, tm, D), lambda b, s: (b, s, 0)),
        out_shape=jax.ShapeDtypeStruct((B, S, D), F32),
        compiler_params=_cp(("parallel", "parallel")),
        name="moe_combine",
    )(x, y2, y2, gt)


def _tiles(S):
    big = S >= 2048
    return dict(tm=512 if big else 256, tl=256, ts_s5=64, ts_lru=128 if big else 64,
                tq=1024 if big else 256)


def kernel(x, c, pos_offset, ada_w, ada_b, norm1_g, w_in, rwkv_mu, rwkv_w0, rwkv_w2, rwkv_a0, rwkv_a2, rwkv_g2, rwkv_k_k, rwkv_k_a, rwkv_r_k, rwkv_ln_w, rwkv_ln_b, s5_lambda_re, s5_lambda_im, s5_b_re, s5_b_im, s5_c_re, s5_c_im, s5_d, s5_log_dt, s5_glu_w, s5_glu_b, mla_q_norm_g, mla_w_uq, mla_kv_norm_g, mla_w_ukv, mla_q_head_g, mla_k_head_g, lru_conv_w, lru_conv_b, lru_w_a, lru_b_a, lru_w_x, lru_b_x, lru_lambda, branch_norm_g, w_out, norm2_g, moe_w_group, moe_b_group, moe_w_expert, moe_b_expert, moe_w1, moe_w3, moe_w2):
    B, S, D = x.shape
    depth = ada_w.shape[0]
    T = B * S
    W = D // N_MIX
    assert B == SUBLANE, "time-major scans put the batch on the 8 sublanes"
    tiles = _tiles(S)
    mod = _adaln_mod(c, ada_w, ada_b)

    def to_tm(t):
        return jnp.swapaxes(t, 0, 1).reshape(S * B, t.shape[-1])

    def from_tm(t):
        return jnp.swapaxes(t.reshape(S, B, t.shape[-1]), 0, 1)

    for l in range(depth):
        sh1, sc1, gt1, sh2, sc2, gt2 = [mod[l, :, None, i * D:(i + 1) * D] for i in range(6)]
        z_rwkv, z_s5, q_a, kv_a, kpe, z_lru, z_gate = _inproj(
            x, sc1, sh1, norm1_g[l].reshape(1, D), _pad_w_in(w_in[l]), tiles['tm'])
        y_a = _rwkv(z_rwkv, rwkv_mu[l], rwkv_w0[l], rwkv_w2[l], rwkv_a0[l], rwkv_a2[l], rwkv_g2[l],
                    rwkv_k_k[l], rwkv_k_a[l], rwkv_r_k[l].reshape(-1), rwkv_ln_w[l], rwkv_ln_b[l], tiles['tl'])
        y_b = from_tm(_s5(to_tm(z_s5), s5_lambda_re[l], s5_lambda_im[l], s5_b_re[l], s5_b_im[l], s5_c_re[l],
                          s5_c_im[l], s5_d[l], s5_log_dt[l], s5_glu_w[l], s5_glu_b[l], branch_norm_g[l, 0],
                          tiles['ts_s5']))
        q, k, v = _mla_prep(q_a, kv_a, kpe, pos_offset, mla_q_norm_g[l], mla_w_uq[l], mla_kv_norm_g[l],
                            mla_w_ukv[l], mla_q_head_g[l], mla_k_head_g[l], tiles['tm'])
        y_c = _attention(q, k, v, branch_norm_g[l, 1], tiles['tq'])
        y_d = from_tm(_lru(to_tm(z_lru), to_tm(z_gate), lru_conv_w[l], lru_conv_b[l], lru_w_a[l], lru_b_a[l],
                           lru_w_x[l], lru_b_x[l], lru_lambda[l], branch_norm_g[l, 2], tiles['ts_lru']))
        w_router = jnp.zeros((D, LANE), F32).at[:, :N_GROUPS].set(moe_w_group[l])
        w_router = w_router.at[:, N_GROUPS:N_GROUPS + N_EXPERTS].set(moe_w_expert[l])
        b_router = jnp.zeros((1, LANE), F32).at[0, :N_GROUPS].set(moe_b_group[l])
        b_router = b_router.at[0, N_GROUPS:N_GROUPS + N_EXPERTS].set(moe_b_expert[l])
        x, h2, logits = _outproj(x, y_a, y_b, y_c, y_d, w_out[l], gt1, sc2, sh2, norm2_g[l].reshape(1, D),
                                 w_router, b_router, tiles['tm'])
        blk_expert, n_used, row_tok, row_dst, row_w = _route(logits.reshape(T, LANE), T)
        y2 = _moe_experts(h2.reshape(T, D), blk_expert, n_used, row_tok, row_dst, row_w,
                          moe_w1[l].astype(BF16), moe_w3[l].astype(BF16), moe_w2[l].astype(BF16))
        x = _combine(x, y2, gt2, tiles['tm'])
    return x
```

```python
import functools
import math

import numpy as np
import jax
import jax.numpy as jnp
from jax import lax
from jax.experimental import pallas as pl
from jax.experimental.pallas import tpu as pltpu

F32 = jnp.float32
BF16 = jnp.bfloat16
HI = lax.Precision.HIGHEST

N_MIX = 4
RWKV_HEAD = 64
RWKV_W_RANK, RWKV_A_RANK, RWKV_G_RANK = 32, 32, 64
RWKV_LN_EPS = 64e-5
S5_GROUP, S5_STATE = 16, 64
MLA_HEADS, MLA_NOPE, MLA_ROPE, MLA_V = 4, 64, 32, 64
MLA_QK = MLA_NOPE + MLA_ROPE
ROPE_THETA = 10000.0
LRU_C = 8.0
CONV_WIDTH = 4
ATTN_CHUNK = 64
N_GROUPS, EXPERTS_PER_GROUP, TOP_K = 4, 8, 2
N_EXPERTS = N_GROUPS * EXPERTS_PER_GROUP
MOE_BLOCK = 256
NORM_EPS = 1e-6

LANE = 128
SUBLANE = 8
RWKV_CHUNK = 64
VMEM_LIMIT = 48 * 1024 * 1024


def _cp(sem, vmem=VMEM_LIMIT):
    return pltpu.CompilerParams(dimension_semantics=sem, vmem_limit_bytes=vmem)


def _dot(a, b):
    return jnp.dot(a, b, preferred_element_type=F32)


def _dot_hi(a, b):
    return jnp.dot(a, b, precision=HI, preferred_element_type=F32)


def _dot_nt(a, b):
    return lax.dot_general(a, b, (((1,), (1,)), ((), ())), preferred_element_type=F32)


def _dot_nt_hi(a, b):
    return lax.dot_general(a, b, (((1,), (1,)), ((), ())), precision=HI, preferred_element_type=F32)


def _softplus(x):
    return jnp.maximum(x, 0.0) + jnp.log1p(jnp.exp(-jnp.abs(x)))


def _rms(x, g, eps=NORM_EPS):
    return x * lax.rsqrt(jnp.mean(x * x, axis=-1, keepdims=True) + eps) * g


def _mod_kernel(c_ref, w_ref, b_ref, o_ref):
    c = c_ref[...]
    cond = c * jax.nn.sigmoid(c)
    o_ref[0] = _dot_hi(cond, w_ref[0]) + b_ref[0]


def _adaln_mod(c, ada_w, ada_b):
    L, D, D6 = ada_w.shape
    B = c.shape[0]
    nj = D6 // D
    return pl.pallas_call(
        _mod_kernel,
        grid=(L, nj),
        in_specs=[pl.BlockSpec((B, D), lambda l, j: (0, 0)),
                  pl.BlockSpec((1, D, D), lambda l, j: (l, 0, j)),
                  pl.BlockSpec((1, 1, D), lambda l, j: (l, 0, j))],
        out_specs=pl.BlockSpec((1, B, D), lambda l, j: (l, 0, j)),
        out_shape=jax.ShapeDtypeStruct((L, B, D6), F32),
        compiler_params=_cp(("parallel", "parallel")),
        name="adaln_mod",
    )(c, ada_w, ada_b.reshape(L, 1, D6))


_RWKV_IN = 3 * 256 + RWKV_W_RANK + RWKV_A_RANK + RWKV_G_RANK
_IN_COLS = (_RWKV_IN, 256, 256, 128, LANE, 256, 256)
_IN_OFFS = tuple(int(v) for v in np.cumsum((0,) + _IN_COLS))
_KPE_LANE0 = MLA_NOPE


def _pad_w_in(w_in):
    D = w_in.shape[0]
    o = np.cumsum((0, _RWKV_IN, 256, 256, 128, MLA_ROPE, 256, 256))
    pieces = [w_in[:, o[i]:o[i + 1]] for i in range(7)]
    kpe = jnp.zeros((D, LANE), w_in.dtype).at[:, _KPE_LANE0:_KPE_LANE0 + MLA_ROPE].set(pieces[4])
    pieces[4] = kpe
    return jnp.concatenate(pieces, axis=1).astype(BF16)


def _inproj_kernel(x_ref, sc_ref, sh_ref, g_ref, w_ref, *outs):
    x = x_ref[0]
    h = _rms(x, g_ref[...]) * (1.0 + sc_ref[0]) + sh_ref[0]
    z = _dot(h.astype(BF16), w_ref[...])
    for i, o_ref in enumerate(outs):
        o_ref[0] = z[:, _IN_OFFS[i]:_IN_OFFS[i + 1]]


def _inproj(x, sc, sh, g, w_pad, tm):
    B, S, D = x.shape
    row = lambda b, s: (b, s, 0)
    vec = lambda b, s: (b, 0, 0)
    return pl.pallas_call(
        _inproj_kernel,
        grid=(B, S // tm),
        in_specs=[pl.BlockSpec((1, tm, D), row),
                  pl.BlockSpec((1, 1, D), vec),
                  pl.BlockSpec((1, 1, D), vec),
                  pl.BlockSpec((1, D), lambda b, s: (0, 0)),
                  pl.BlockSpec(w_pad.shape, lambda b, s: (0, 0))],
        out_specs=[pl.BlockSpec((1, tm, n), row) for n in _IN_COLS],
        out_shape=[jax.ShapeDtypeStruct((B, S, n), F32) for n in _IN_COLS],
        compiler_params=_cp(("parallel", "parallel")),
        name="inproj",
    )(x, sc, sh, g, w_pad)


def _split_bf16(x, parts):
    out = []
    for _ in range(parts - 1):
        hi = x.astype(BF16)
        out.append(hi)
        x = x - hi.astype(F32)
    out.append(x.astype(BF16))
    return out


def _rwkv_kernel(z_ref, mu_ref, w0_ref, a0_ref, wl_ref, kk_ref, ka_ref, rk_ref, lnw_ref, lnb_ref,
                 hsum_ref, tri_ref, o_ref, zprev_ref, g_ref, y_ref):
    W = 256
    NH = W // RWKV_HEAD
    L = RWKV_CHUNK
    TL = z_ref.shape[1]

    @pl.when(pl.program_id(1) == 0)
    def _():
        zprev_ref[...] = jnp.zeros_like(zprev_ref)
        g_ref[...] = jnp.zeros_like(g_ref)

    hsum = hsum_ref[...]

    def head_sum(t):
        hi, lo_ = _split_bf16(t, 2)
        return _dot(hi, hsum) + _dot(lo_, hsum)

    z = z_ref[0]
    rows = lax.broadcasted_iota(jnp.int32, z.shape, 0)
    zp = jnp.where(rows == 0, zprev_ref[...], pltpu.roll(z, 1, axis=0))
    zprev_ref[...] = z[TL - 1:TL, :]
    zs = z + (zp - z) * mu_ref[...]
    r, k, v, lo = zs[:, 0:W], zs[:, W:2 * W], zs[:, 2 * W:3 * W], zs[:, 3 * W:3 * W + LANE]
    lane = lax.broadcasted_iota(jnp.int32, lo.shape, 1)
    act = jnp.where(lane < RWKV_W_RANK, jnp.tanh(lo),
                    jnp.where(lane < RWKV_W_RANK + RWKV_A_RANK, lo, jax.nn.sigmoid(lo)))
    up = _dot(act.astype(BF16), wl_ref[...])
    w = -_softplus(-(w0_ref[...] + up[:, 0:W])) - 0.5
    ld = -jnp.exp(w)
    a = jax.nn.sigmoid(a0_ref[...] + up[:, W:2 * W])
    g = up[:, 2 * W:3 * W]
    kk = k * kk_ref[...]
    kk = kk / jnp.maximum(jnp.sqrt(head_sum(kk * kk)), 1e-12)
    k2 = k * (1.0 + (a - 1.0) * ka_ref[...])
    kka = kk * a

    tri_b = tri_ref[0]
    tri_incl = tri_b > 0
    tri_strict = tri_ref[1] > 0
    eye = (lax.broadcasted_iota(jnp.int32, (RWKV_HEAD, RWKV_HEAD), 0)
           == lax.broadcasted_iota(jnp.int32, (RWKV_HEAD, RWKV_HEAD), 1)).astype(BF16)

    for j in range(TL // L):
        sl = slice(j * L, (j + 1) * L)
        ld_c = ld[sl]
        cs = sum(_dot(tri_b, p) for p in _split_bf16(ld_c, 3))
        cs_l = cs[L - 1:L, :]
        p_in = jnp.exp(cs)
        p_ex = jnp.exp(cs - ld_c)
        p_inv = jnp.exp(-cs)
        p_end = jnp.exp(cs_l - cs)
        p_l = jnp.exp(cs_l)
        bt = -kk[sl] * p_ex
        rt = r[sl] * p_in
        a_s = (kka[sl] * p_inv).astype(BF16)
        k_s = (k2[sl] * p_inv).astype(BF16)
        a_e = (kka[sl] * p_end).astype(BF16)
        k_e = (k2[sl] * p_end).astype(BF16)
        v_c = v[sl].astype(BF16)
        for h in range(NH):
            hs = slice(h * RWKV_HEAD, (h + 1) * RWKV_HEAD)
            x = jnp.concatenate([bt[:, hs], rt[:, hs]], axis=0).astype(BF16)
            ga = _dot_nt(x, a_s[:, hs])
            gk = _dot_nt(x, k_s[:, hs])
            m_ba = jnp.where(tri_strict, ga[:L], 0.0).astype(BF16)
            m_ra = jnp.where(tri_incl, ga[L:], 0.0).astype(BF16)
            m_bk = jnp.where(tri_strict, gk[:L], 0.0).astype(BF16)
            m_rk = jnp.where(tri_incl, gk[L:], 0.0).astype(BF16)
            v_h = v_c[:, hs]
            wn = jnp.concatenate([bt[:, hs], _dot(m_bk, v_h)], axis=1)
            n_pow = m_ba
            wn = wn + _dot(n_pow, wn.astype(BF16))
            for _ in range(5):
                n_pow = _dot(n_pow, n_pow).astype(BF16)
                wn = wn + _dot(n_pow, wn.astype(BF16))
            wn_b = wn.astype(BF16)
            b2, u0 = wn_b[:, :RWKV_HEAD], wn_b[:, RWKV_HEAD:]
            mw = _dot(m_ra, wn_b)
            r2 = (rt[:, hs] + mw[:, :RWKV_HEAD]).astype(BF16)
            y0 = mw[:, RWKV_HEAD:] + _dot(m_rk, v_h)
            ak = jnp.concatenate([a_e[:, hs], k_e[:, hs]], axis=0)
            b2_t = _dot_nt(eye, b2).astype(BF16)
            c_m = _dot(b2_t, a_e[:, hs]).astype(BF16)
            uv_t = _dot_nt(eye, jnp.concatenate([u0, v_h], axis=0)).astype(BF16)
            d_t = _dot(uv_t, ak)
            g0 = g_ref[h]
            g_hi, g_lo = _split_bf16(g0, 2)
            y_ref[sl, hs] = _dot_nt(r2, g_hi) + _dot_nt(r2, g_lo) + y0
            g_ref[h] = g0 * p_l[:, hs] + _dot(g_hi, c_m) + _dot(g_lo, c_m) + d_t

    y = y_ref[...]
    inv_n = 1.0 / RWKV_HEAD
    mean = head_sum(y) * inv_n
    yc = y - mean
    var = head_sum(yc * yc) * inv_n
    yn = yc * lax.rsqrt(var + RWKV_LN_EPS) * lnw_ref[...] + lnb_ref[...]
    bonus = head_sum(r * k2 * rk_ref[...]) * v
    o_ref[0] = ((yn + bonus) * g).astype(o_ref.dtype)


def _rwkv(z, mu, w0, w2, a0, a2, g2, k_k, k_a, r_k, ln_w, ln_b, tl):
    B, S, _ = z.shape
    W = 256
    L = RWKV_CHUNK
    wl = jnp.zeros((LANE, 3 * W), F32)
    wl = wl.at[0:RWKV_W_RANK, 0:W].set(w2)
    wl = wl.at[RWKV_W_RANK:RWKV_W_RANK + RWKV_A_RANK, W:2 * W].set(a2)
    wl = wl.at[RWKV_W_RANK + RWKV_A_RANK:LANE, 2 * W:3 * W].set(g2).astype(BF16)
    hid = np.arange(W) // RWKV_HEAD
    hsum = jnp.asarray((hid[:, None] == hid[None, :]).astype(np.float32), dtype=BF16)
    t = np.arange(L)
    tri = jnp.asarray(np.stack([(t[None, :] <= t[:, None]), (t[None, :] < t[:, None])]).astype(np.float32), dtype=BF16)
    row2 = lambda v_: v_.reshape(1, -1)
    const = lambda shape: pl.BlockSpec(shape, lambda b, s: (0,) * len(shape))
    return pl.pallas_call(
        _rwkv_kernel,
        grid=(B, S // tl),
        in_specs=[pl.BlockSpec((1, tl, _RWKV_IN), lambda b, s: (b, s, 0)),
                  const((1, _RWKV_IN)), const((1, W)), const((1, W)), const((LANE, 3 * W)),
                  const((1, W)), const((1, W)), const((1, W)), const((1, W)), const((1, W)),
                  const((W, W)), const((2, L, L))],
        out_specs=pl.BlockSpec((1, tl, W), lambda b, s: (b, s, 0)),
        out_shape=jax.ShapeDtypeStruct((B, S, W), BF16),
        scratch_shapes=[pltpu.VMEM((1, _RWKV_IN), F32),
                        pltpu.VMEM((W // RWKV_HEAD, RWKV_HEAD, RWKV_HEAD), F32),
                        pltpu.VMEM((tl, W), F32)],
        compiler_params=_cp(("parallel", "arbitrary")),
        name="rwkv7",
    )(z, row2(mu), row2(w0), row2(a0), wl, row2(k_k), row2(k_a), row2(r_k), row2(ln_w), row2(ln_b), hsum, tri)


def _s5_kernel(u_ref, wb_ref, are_ref, aim_ref, wc_ref, d_ref, gw_ref, gb_ref, ng_ref, o_ref,
               hre_ref, him_ref, sre_ref, sim_ref):
    nb = SUBLANE
    rows = u_ref.shape[0]
    P = are_ref.shape[1]

    @pl.when(pl.program_id(0) == 0)
    def _():
        sre_ref[...] = jnp.zeros_like(sre_ref)
        sim_ref[...] = jnp.zeros_like(sim_ref)

    u = u_ref[...]
    bu = _dot(u.astype(BF16), wb_ref[...])
    hre_ref[...] = bu[:, :P]
    him_ref[...] = bu[:, P:]
    a_re = jnp.broadcast_to(are_ref[...], (nb, P))
    a_im = jnp.broadcast_to(aim_ref[...], (nb, P))

    def step(t, carry):
        h_re, h_im = carry
        i = pl.multiple_of(t * nb, nb)
        n_re = a_re * h_re - a_im * h_im + hre_ref[pl.ds(i, nb), :]
        n_im = a_re * h_im + a_im * h_re + him_ref[pl.ds(i, nb), :]
        hre_ref[pl.ds(i, nb), :] = n_re
        him_ref[pl.ds(i, nb), :] = n_im
        return n_re, n_im

    h_re, h_im = lax.fori_loop(0, rows // nb, step, (sre_ref[...], sim_ref[...]))
    sre_ref[...] = h_re
    sim_ref[...] = h_im
    y = _dot(hre_ref[...].astype(BF16), wc_ref[0]) + _dot(him_ref[...].astype(BF16), wc_ref[1])
    y = jax.nn.gelu(y + d_ref[...] * u)
    y = y * jax.nn.sigmoid(_dot(y.astype(BF16), gw_ref[...]) + gb_ref[...])
    o_ref[...] = _rms(y, ng_ref[...]).astype(o_ref.dtype)


def _s5(u_tm, lam_re, lam_im, b_re, b_im, c_re, c_im, d, log_dt, glu_w, glu_b, norm_g, ts):
    rows_total, W = u_tm.shape
    G, P = lam_re.shape
    dt = jnp.exp(log_dt)[:, None]
    mag = jnp.exp(lam_re * dt)
    a_re = mag * jnp.cos(lam_im * dt)
    a_im = mag * jnp.sin(lam_im * dt)
    den = lam_re * lam_re + lam_im * lam_im
    q_re = ((a_re - 1.0) * lam_re + a_im * lam_im) / den
    q_im = (a_im * lam_re - (a_re - 1.0) * lam_im) / den
    bb_re = q_re[..., None] * b_re - q_im[..., None] * b_im
    bb_im = q_re[..., None] * b_im + q_im[..., None] * b_re
    eye = jnp.eye(G, dtype=F32)
    wb_re = jnp.einsum('gpi,gh->gihp', bb_re, eye).reshape(W, G * P)
    wb_im = jnp.einsum('gpi,gh->gihp', bb_im, eye).reshape(W, G * P)
    wb = jnp.concatenate([wb_re, wb_im], axis=1).astype(BF16)
    wc_re = jnp.einsum('gip,gh->gphi', c_re, eye).reshape(G * P, W)
    wc_im = jnp.einsum('gip,gh->gphi', -c_im, eye).reshape(G * P, W)
    wc = jnp.stack([wc_re, wc_im]).astype(BF16)
    GP = G * P
    rows = ts * SUBLANE
    const = lambda shape: pl.BlockSpec(shape, lambda s: (0,) * len(shape))
    return pl.pallas_call(
        _s5_kernel,
        grid=(rows_total // rows,),
        in_specs=[pl.BlockSpec((rows, W), lambda s: (s, 0)),
                  const((W, 2 * GP)), const((1, GP)), const((1, GP)), const((2, GP, W)),
                  const((1, W)), const((W, W)), const((1, W)), const((1, W))],
        out_specs=pl.BlockSpec((rows, W), lambda s: (s, 0)),
        out_shape=jax.ShapeDtypeStruct((rows_total, W), BF16),
        scratch_shapes=[pltpu.VMEM((rows, GP), F32), pltpu.VMEM((rows, GP), F32),
                        pltpu.VMEM((SUBLANE, GP), F32), pltpu.VMEM((SUBLANE, GP), F32)],
        compiler_params=_cp(("arbitrary",)),
        name="s5",
    )(u_tm, wb, a_re.reshape(1, GP), a_im.reshape(1, GP), wc, d.reshape(1, W), glu_w.astype(BF16),
      glu_b.reshape(1, W), norm_g.reshape(1, W))


def _lru_kernel(x_ref, gate_ref, cw_ref, cb_ref, wax_ref, bax_ref, lam_ref, ng_ref, o_ref,
                xc_ref, a_ref, b_ref, hs_ref):
    nb = SUBLANE
    rows, W = x_ref.shape
    halo = (CONV_WIDTH - 1) * nb

    @pl.when(pl.program_id(0) == 0)
    def _():
        xc_ref[...] = jnp.zeros_like(xc_ref)
        hs_ref[...] = jnp.zeros_like(hs_ref)

    x = x_ref[...]
    xf = jnp.concatenate([xc_ref[...], x], axis=0)
    xc_ref[...] = x[rows - halo:, :]
    xc = cb_ref[...]
    for j in range(CONV_WIDTH):
        xc = xc + cw_ref[j:j + 1, :] * xf[j * nb:j * nb + rows, :]
    gates = _dot(xc.astype(BF16), wax_ref[...]) + bax_ref[...]
    r = jax.nn.sigmoid(gates[:, :W])
    i = jax.nn.sigmoid(gates[:, W:])
    log_a = -LRU_C * r * _softplus(-lam_ref[...])
    a = jnp.exp(log_a)
    a_ref[...] = a
    b_ref[...] = jnp.sqrt(1.0 - jnp.exp(2.0 * log_a)) * (i * xc)

    def step(t, h):
        i0 = pl.multiple_of(t * nb, nb)
        h = a_ref[pl.ds(i0, nb), :] * h + b_ref[pl.ds(i0, nb), :]
        b_ref[pl.ds(i0, nb), :] = h
        return h

    hs_ref[...] = lax.fori_loop(0, rows // nb, step, hs_ref[...])
    y = b_ref[...] * jax.nn.gelu(gate_ref[...])
    o_ref[...] = _rms(y, ng_ref[...]).astype(o_ref.dtype)


def _lru(x_tm, gate_tm, conv_w, conv_b, w_a, b_a, w_x, b_x, lam, norm_g, ts):
    rows_total, W = x_tm.shape
    nblk, bs, _ = w_a.shape
    eye = jnp.eye(nblk, dtype=F32)
    bd = lambda w_: jnp.einsum('nkj,nm->nkmj', w_, eye).reshape(W, W)
    wax = jnp.concatenate([bd(w_a), bd(w_x)], axis=1).astype(BF16)
    bax = jnp.concatenate([b_a, b_x]).reshape(1, 2 * W)
    rows = ts * SUBLANE
    const = lambda shape: pl.BlockSpec(shape, lambda s: (0,) * len(shape))
    return pl.pallas_call(
        _lru_kernel,
        grid=(rows_total // rows,),
        in_specs=[pl.BlockSpec((rows, W), lambda s: (s, 0)), pl.BlockSpec((rows, W), lambda s: (s, 0)),
                  const((CONV_WIDTH, W)), const((1, W)), const((W, 2 * W)), const((1, 2 * W)),
                  const((1, W)), const((1, W))],
        out_specs=pl.BlockSpec((rows, W), lambda s: (s, 0)),
        out_shape=jax.ShapeDtypeStruct((rows_total, W), BF16),
        scratch_shapes=[pltpu.VMEM(((CONV_WIDTH - 1) * SUBLANE, W), F32), pltpu.VMEM((rows, W), F32),
                        pltpu.VMEM((rows, W), F32), pltpu.VMEM((SUBLANE, W), F32)],
        compiler_params=_cp(("arbitrary",)),
        name="rglru",
    )(x_tm, gate_tm, conv_w, conv_b.reshape(1, W), wax, bax, lam.reshape(1, W), norm_g.reshape(1, W))


def _mla_prep_kernel(pos_ref, qa_ref, kva_ref, kpe_ref, gq_ref, gkv_ref, wq_ref, wk_ref, wv_ref,
                     gqh_ref, gkh_ref, freq_ref, q_ref, k_ref, v_ref):
    tm = qa_ref.shape[1]
    b = pl.program_id(0)
    s0 = pl.program_id(1) * tm
    pos = (pos_ref[b] + s0 + lax.broadcasted_iota(jnp.int32, (tm, LANE), 0)).astype(F32)
    ang = pos * freq_ref[...]
    cos_t = jnp.cos(ang)
    sin_t = jnp.sin(ang)
    lane = lax.broadcasted_iota(jnp.int32, (tm, LANE), 1)
    half = MLA_ROPE // 2
    first = (lane >= MLA_NOPE) & (lane < MLA_NOPE + half)
    second = (lane >= MLA_NOPE + half) & (lane < MLA_QK)

    def rope(t):
        rot = jnp.where(first, -pltpu.roll(t, LANE - half, axis=1),
                        jnp.where(second, pltpu.roll(t, half, axis=1), 0.0))
        return t * cos_t + rot * sin_t

    def head_norm(t, g):
        ms = jnp.sum(t * t, axis=-1, keepdims=True) * (1.0 / MLA_QK)
        return t * lax.rsqrt(ms + NORM_EPS) * g

    qn = _rms(qa_ref[0], gq_ref[...]).astype(BF16)
    kvn = _rms(kva_ref[0], gkv_ref[...]).astype(BF16)
    q_all = _dot(qn, wq_ref[...])
    k_all = _dot(kvn, wk_ref[...])
    v_all = _dot(kvn, wv_ref[...])
    kpe = kpe_ref[0]
    scale = MLA_QK ** -0.5
    for h in range(MLA_HEADS):
        hs = slice(h * LANE, (h + 1) * LANE)
        q_ref[0, h] = (rope(head_norm(q_all[:, hs], gqh_ref[...])) * scale).astype(q_ref.dtype)
        k_ref[0, h] = rope(head_norm(k_all[:, hs] + kpe, gkh_ref[...])).astype(k_ref.dtype)
        v_ref[0, h] = v_all[:, hs].astype(v_ref.dtype)


def _mla_prep(q_a, kv_a, kpe, pos_offset, q_norm_g, w_uq, kv_norm_g, w_ukv, q_head_g, k_head_g, tm):
    B, S, QR = q_a.shape
    KVR = kv_a.shape[2]
    H = MLA_HEADS
    wq = jnp.zeros((QR, H, LANE), F32).at[:, :, :MLA_QK].set(w_uq.reshape(QR, H, MLA_QK)).reshape(QR, H * LANE)
    wkv = w_ukv.reshape(KVR, H, MLA_NOPE + MLA_V)
    wk = jnp.zeros((KVR, H, LANE), F32).at[:, :, :MLA_NOPE].set(wkv[:, :, :MLA_NOPE]).reshape(KVR, H * LANE)
    wv = jnp.zeros((KVR, H, LANE), F32).at[:, :, :MLA_V].set(wkv[:, :, MLA_NOPE:]).reshape(KVR, H * LANE)
    pad_g = lambda g: jnp.zeros((1, LANE), F32).at[0, :MLA_QK].set(g)
    half = MLA_ROPE // 2
    inv_freq = np.power(np.float32(ROPE_THETA), -np.arange(half, dtype=np.float32) * np.float32(2.0) / np.float32(MLA_ROPE))
    freq = np.zeros((1, LANE), np.float32)
    freq[0, MLA_NOPE:MLA_NOPE + half] = inv_freq
    freq[0, MLA_NOPE + half:MLA_QK] = inv_freq
    const = lambda shape: pl.BlockSpec(shape, lambda b, s, p: (0,) * len(shape))
    row = lambda n: pl.BlockSpec((1, tm, n), lambda b, s, p: (b, s, 0))
    hout = pl.BlockSpec((1, H, tm, LANE), lambda b, s, p: (b, 0, s, 0))
    grid_spec = pltpu.PrefetchScalarGridSpec(
        num_scalar_prefetch=1, grid=(B, S // tm),
        in_specs=[row(QR), row(KVR), row(LANE), const((1, QR)), const((1, KVR)),
                  const((QR, H * LANE)), const((KVR, H * LANE)), const((KVR, H * LANE)),
                  const((1, LANE)), const((1, LANE)), const((1, LANE))],
        out_specs=[hout, hout, hout])
    return pl.pallas_call(
        _mla_prep_kernel,
        grid_spec=grid_spec,
        out_shape=[jax.ShapeDtypeStruct((B, H, S, LANE), BF16)] * 3,
        compiler_params=_cp(("parallel", "parallel")),
        name="mla_prep",
    )(pos_offset, q_a, kv_a, kpe, q_norm_g.reshape(1, QR), kv_norm_g.reshape(1, KVR),
      wq.astype(BF16), wk.astype(BF16), wv.astype(BF16), pad_g(q_head_g), pad_g(k_head_g), jnp.asarray(freq))


def _attn_kernel(q_ref, k_ref, v_ref, ng_ref, o_ref, m_ref, l_ref, acc_ref):
    H = q_ref.shape[1]
    tq = q_ref.shape[2]
    qi = pl.program_id(1)
    ki = pl.program_id(2)

    @pl.when(ki == 0)
    def _():
        m_ref[...] = jnp.full_like(m_ref, -1e30)
        l_ref[...] = jnp.zeros_like(l_ref)
        acc_ref[...] = jnp.zeros_like(acc_ref)

    def update(masked):
        if masked:
            qc = lax.broadcasted_iota(jnp.int32, (tq, tq), 0) // ATTN_CHUNK
            kc = lax.broadcasted_iota(jnp.int32, (tq, tq), 1) // ATTN_CHUNK
            visible = kc <= qc
        for h in range(H):
            s = _dot_nt(q_ref[0, h], k_ref[0, h])
            if masked:
                s = jnp.where(visible, s, -1e30)
            m_old = m_ref[h]
            m_new = jnp.maximum(m_old, jnp.max(s, axis=-1, keepdims=True))
            alpha = jnp.exp(m_old - m_new)
            p = jnp.exp(s - m_new)
            l_ref[h] = alpha * l_ref[h] + jnp.sum(p, axis=-1, keepdims=True)
            acc_ref[h] = alpha * acc_ref[h] + _dot(p.astype(BF16), v_ref[0, h])
            m_ref[h] = m_new

    @pl.when(ki < qi)
    def _():
        update(False)

    @pl.when(ki == qi)
    def _():
        update(True)
        o = jnp.concatenate([(acc_ref[h] / l_ref[h])[:, :MLA_V] for h in range(H)], axis=-1)
        o_ref[0] = _rms(o, ng_ref[...]).astype(o_ref.dtype)


def _attention(q, k, v, norm_g, tq):
    B, H, S, _ = q.shape
    W = H * MLA_V
    nq = S // tq
    qspec = pl.BlockSpec((1, H, tq, LANE), lambda b, i, j: (b, 0, i, 0))
    kspec = pl.BlockSpec((1, H, tq, LANE), lambda b, i, j: (b, 0, jnp.minimum(i, j), 0))
    return pl.pallas_call(
        _attn_kernel,
        grid=(B, nq, nq),
        in_specs=[qspec, kspec, kspec, pl.BlockSpec((1, W), lambda b, i, j: (0, 0))],
        out_specs=pl.BlockSpec((1, tq, W), lambda b, i, j: (b, i, 0)),
        out_shape=jax.ShapeDtypeStruct((B, S, W), BF16),
        scratch_shapes=[pltpu.VMEM((H, tq, 1), F32), pltpu.VMEM((H, tq, 1), F32),
                        pltpu.VMEM((H, tq, LANE), F32)],
        compiler_params=_cp(("parallel", "parallel", "arbitrary")),
        name="mla_attn",
    )(q, k, v, norm_g.reshape(1, W))


def _store_token_tiles(ref, val):
    n, d = val.shape
    for s in range(d // LANE):
        ref[pl.ds(s, n, stride=SUBLANE), :] = val[:, s * LANE:(s + 1) * LANE]


def _load_token_tiles(ref, n):
    return jnp.concatenate([ref[pl.ds(s, n, stride=SUBLANE), :] for s in range(SUBLANE)], axis=1)


def _outproj_kernel(x_ref, ya_ref, yb_ref, yc_ref, yd_ref, wo_ref, gt_ref, sc_ref, sh_ref, g_ref,
                    wr_ref, br_ref, xo_ref, h_ref, lg_ref):
    acc = _dot(ya_ref[0], wo_ref[0])
    acc = acc + _dot(yb_ref[0], wo_ref[1])
    acc = acc + _dot(yc_ref[0], wo_ref[2])
    acc = acc + _dot(yd_ref[0], wo_ref[3])
    x = x_ref[0] + gt_ref[0] * acc
    xo_ref[0] = x
    h = _rms(x, g_ref[...]) * (1.0 + sc_ref[0]) + sh_ref[0]
    _store_token_tiles(h_ref.at[0], h)
    lg_ref[0] = _dot_hi(h, wr_ref[...]) + br_ref[...]


def _outproj(x, ya, yb, yc, yd, w_out, gt, sc, sh, g, w_router, b_router, tm):
    B, S, D = x.shape
    W = D // N_MIX
    row = lambda n: pl.BlockSpec((1, tm, n), lambda b, s: (b, s, 0))
    vec = pl.BlockSpec((1, 1, D), lambda b, s: (b, 0, 0))
    const = lambda shape: pl.BlockSpec(shape, lambda b, s: (0,) * len(shape))
    return pl.pallas_call(
        _outproj_kernel,
        grid=(B, S // tm),
        in_specs=[row(D), row(W), row(W), row(W), row(W), const((N_MIX, W, D)), vec, vec, vec,
                  const((1, D)), const((D, LANE)), const((1, LANE))],
        out_specs=[row(D), pl.BlockSpec((1, tm * SUBLANE, LANE), lambda b, s: (b, s, 0)), row(LANE)],
        out_shape=[jax.ShapeDtypeStruct((B, S, D), F32), jax.ShapeDtypeStruct((B, S * SUBLANE, LANE), F32),
                   jax.ShapeDtypeStruct((B, S, LANE), F32)],
        compiler_params=_cp(("parallel", "parallel")),
        name="outproj",
    )(x, ya, yb, yc, yd, w_out.reshape(N_MIX, W, D).astype(BF16), gt, sc, sh, g, w_router, b_router)


_MOE_PAD_ROWS = MOE_BLOCK // TOP_K


def _moe_kernel(be_ref, nused_ref, tok_ref, tokn_ref, dst_ref, w_ref, h_hbm, w1_ref, w3_ref, w2_ref, o_hbm,
                xbuf, ybuf, gsem, ssem):
    i = pl.program_id(0)
    R = MOE_BLOCK
    TR = SUBLANE
    nused = nused_ref[0]
    slot = lax.rem(i, 2)
    other = 1 - slot

    def gather_rows(idx_ref, s):
        def body(r, c):
            src = pl.multiple_of(idx_ref[0, 0, r], TR)
            pltpu.make_async_copy(h_hbm.at[pl.ds(src, TR)], xbuf.at[s, pl.ds(pl.multiple_of(r * TR, TR), TR)],
                                  gsem.at[s]).start()
            return c
        lax.fori_loop(0, R, body, 0, unroll=8)

    def scatter_rows(s):
        def body(r, c):
            dst = pl.multiple_of(dst_ref[0, 0, r], TR)
            pltpu.make_async_copy(ybuf.at[s, pl.ds(pl.multiple_of(r * TR, TR), TR)], o_hbm.at[pl.ds(dst, TR)],
                                  ssem.at[s]).start()
            return c
        lax.fori_loop(0, R, body, 0, unroll=8)

    def wait_gather(s):
        pltpu.make_async_copy(h_hbm.at[pl.ds(0, R * TR)], xbuf.at[s], gsem.at[s]).wait()

    def wait_scatter(s):
        pltpu.make_async_copy(ybuf.at[s], o_hbm.at[pl.ds(0, R * TR)], ssem.at[s]).wait()

    @pl.when(i == 0)
    def _():
        plane = o_hbm.shape[0] // TOP_K
        npad = _MOE_PAD_ROWS * TR
        ybuf[1] = jnp.zeros(ybuf.shape[1:], ybuf.dtype)
        for j in range(TOP_K):
            pad = pltpu.make_async_copy(ybuf.at[1, pl.ds(0, npad)],
                                        o_hbm.at[pl.ds((j + 1) * plane - npad, npad)], ssem.at[1])
            pad.start()
            pad.wait()
        gather_rows(tok_ref, 0)

    @pl.when(i < nused)
    def _():
        @pl.when(i + 1 < nused)
        def _():
            gather_rows(tokn_ref, other)

        wait_gather(slot)

        @pl.when(i >= 2)
        def _():
            wait_scatter(slot)

        xb = _load_token_tiles(xbuf.at[slot], R).astype(BF16)
        hid = _dot(xb, w1_ref[0])
        hid = hid * jax.nn.sigmoid(hid) * _dot(xb, w3_ref[0])
        y = _dot(hid.astype(BF16), w2_ref[0])
        eye_r = (lax.broadcasted_iota(jnp.int32, (R, R), 0) == lax.broadcasted_iota(jnp.int32, (R, R), 1)).astype(F32)
        wcol = _dot_nt_hi(eye_r, jnp.broadcast_to(w_ref[0], (LANE, R)))
        _store_token_tiles(ybuf.at[slot], y * wcol[:, 0:1])
        scatter_rows(slot)

        @pl.when(i == nused - 1)
        def _():
            wait_scatter(slot)

            @pl.when(i >= 1)
            def _():
                wait_scatter(other)


def _moe_experts(h2, blk_expert, n_used, row_tok, row_dst, row_w, w1, w3, w2):
    E, D, F = w1.shape
    T = h2.shape[0] // SUBLANE
    nb = blk_expert.shape[0]
    R = MOE_BLOCK
    blk = lambda i, be, nu: (i, 0, 0)
    nxt = lambda i, be, nu: (jnp.minimum(i + 1, nb - 1), 0, 0)
    wsel = lambda i, be, nu: (be[i], 0, 0)
    plane = (T + _MOE_PAD_ROWS) * SUBLANE
    row_tok = (row_tok * SUBLANE).reshape(nb, 1, R)
    row_dst = ((row_dst % TOP_K) * plane + (row_dst // TOP_K) * SUBLANE).reshape(nb, 1, R)
    grid_spec = pltpu.PrefetchScalarGridSpec(
        num_scalar_prefetch=2, grid=(nb,),
        in_specs=[pl.BlockSpec((1, 1, R), blk, memory_space=pltpu.SMEM),
                  pl.BlockSpec((1, 1, R), nxt, memory_space=pltpu.SMEM),
                  pl.BlockSpec((1, 1, R), blk, memory_space=pltpu.SMEM),
                  pl.BlockSpec((1, 1, R), blk),
                  pl.BlockSpec(memory_space=pl.ANY),
                  pl.BlockSpec((1, D, F), wsel), pl.BlockSpec((1, D, F), wsel), pl.BlockSpec((1, F, D), wsel)],
        out_specs=pl.BlockSpec(memory_space=pl.ANY),
        scratch_shapes=[pltpu.VMEM((2, R * SUBLANE, LANE), F32), pltpu.VMEM((2, R * SUBLANE, LANE), F32),
                        pltpu.SemaphoreType.DMA((2,)), pltpu.SemaphoreType.DMA((2,))])
    return pl.pallas_call(
        _moe_kernel,
        grid_spec=grid_spec,
        out_shape=jax.ShapeDtypeStruct((TOP_K * plane, LANE), F32),
        compiler_params=_cp(("arbitrary",)),
        name="moe_experts",
    )(blk_expert, n_used, row_tok, row_tok, row_dst, row_w.reshape(nb, 1, R), h2, w1, w3, w2)


def _route(logits, T):
    p_group = jax.nn.softmax(logits[:, :N_GROUPS], axis=-1)
    g_sel = jnp.argmax(p_group, axis=-1).astype(jnp.int32)
    g_prob = jnp.max(p_group, axis=-1)
    logit_e = logits[:, N_GROUPS:N_GROUPS + N_EXPERTS].reshape(T, N_GROUPS, EXPERTS_PER_GROUP)
    logit_e = jnp.take_along_axis(logit_e, g_sel[:, None, None], axis=1)[:, 0]
    p_e = jax.nn.softmax(logit_e, axis=-1)
    top_p, top_i = lax.top_k(p_e, TOP_K)
    gate = (g_prob[:, None] * top_p / jnp.sum(top_p, axis=-1, keepdims=True)).reshape(-1)
    expert = (g_sel[:, None] * EXPERTS_PER_GROUP + top_i.astype(jnp.int32)).reshape(-1)
    n_assign = T * TOP_K
    n_blocks = -(-n_assign // MOE_BLOCK) + N_EXPERTS
    order = jnp.argsort(expert).astype(jnp.int32)
    counts = jnp.sum((expert[:, None] == jnp.arange(N_EXPERTS, dtype=jnp.int32)[None, :]).astype(jnp.int32), axis=0)
    start = jnp.cumsum(counts) - counts
    pcounts = (counts + MOE_BLOCK - 1) // MOE_BLOCK * MOE_BLOCK
    pend = jnp.cumsum(pcounts)
    pstart = pend - pcounts
    blk_start = jnp.arange(n_blocks, dtype=jnp.int32) * MOE_BLOCK
    blk_expert = jnp.minimum(jnp.sum((pend[None, :] <= blk_start[:, None]).astype(jnp.int32), axis=1), N_EXPERTS - 1)
    n_used = (pend[-1] // MOE_BLOCK).astype(jnp.int32).reshape(1)
    lane = jnp.arange(MOE_BLOCK, dtype=jnp.int32)[None, :]
    off = blk_start[:, None] + lane - pstart[blk_expert][:, None]
    valid = off < counts[blk_expert][:, None]
    pos = jnp.clip(start[blk_expert][:, None] + off, 0, n_assign - 1)
    asg = order[pos]
    row_tok = jnp.where(valid, asg // TOP_K, 0)
    row_dst = jnp.where(valid, asg, n_assign + lane)
    row_w = jnp.where(valid, gate[asg], 0.0)
    return blk_expert.astype(jnp.int32), n_used, row_tok, row_dst, row_w


def _combine_kernel(x_ref, y0_ref, y1_ref, gt_ref, o_ref):
    tm = x_ref.shape[1]
    y = _load_token_tiles(y0_ref.at[0], tm) + _load_token_tiles(y1_ref.at[0], tm)
    o_ref[0] = x_ref[0] + gt_ref[0] * y


def _combine(x, y2, gt, tm):
    B, S, D = x.shape
    nt = S // tm
    return pl.pallas_call(
        _combine_kernel,
        grid=(B, nt),
        in_specs=[pl.BlockSpec((1, tm, D), lambda b, s: (b, s, 0)),
                  pl.BlockSpec((1, tm * SUBLANE, LANE), lambda b, s: (0, b * nt + s, 0)),
                  pl.BlockSpec((1, tm * SUBLANE, LANE), lambda b, s: (1, b * nt + s, 0)),
                  pl.BlockSpec((1, 1, D), lambda b, s: (b, 0, 0))],
        out_specs=pl.BlockSpec((1, tm, D), lambda b, s: (b, s, 0)),
        out_shape=jax.ShapeDtypeStruct((B, S, D), F32),
        compiler_params=_cp(("parallel", "parallel")),
        name="moe_combine",
    )(x, y2, y2, gt)


def _tiles(S):
    big = S >= 2048
    return dict(tm=512 if big else 256, tl=256, ts_s5=64, ts_lru=128 if big else 64,
                tq=1024 if big else 256)


def kernel(x, c, pos_offset, ada_w, ada_b, norm1_g, w_in, rwkv_mu, rwkv_w0, rwkv_w2, rwkv_a0, rwkv_a2, rwkv_g2, rwkv_k_k, rwkv_k_a, rwkv_r_k, rwkv_ln_w, rwkv_ln_b, s5_lambda_re, s5_lambda_im, s5_b_re, s5_b_im, s5_c_re, s5_c_im, s5_d, s5_log_dt, s5_glu_w, s5_glu_b, mla_q_norm_g, mla_w_uq, mla_kv_norm_g, mla_w_ukv, mla_q_head_g, mla_k_head_g, lru_conv_w, lru_conv_b, lru_w_a, lru_b_a, lru_w_x, lru_b_x, lru_lambda, branch_norm_g, w_out, norm2_g, moe_w_group, moe_b_group, moe_w_expert, moe_b_expert, moe_w1, moe_w3, moe_w2):
    B, S, D = x.shape
    depth = ada_w.shape[0]
    T = B * S
    W = D // N_MIX
    assert B == SUBLANE, "time-major scans put the batch on the 8 sublanes"
    assert D == SUBLANE * LANE, "token-tile layout: one (8, 128) tile per token row"
    tiles = _tiles(S)
    mod = _adaln_mod(c, ada_w, ada_b)

    def to_tm(t):
        return jnp.swapaxes(t, 0, 1).reshape(S * B, t.shape[-1])

    def from_tm(t):
        return jnp.swapaxes(t.reshape(S, B, t.shape[-1]), 0, 1)

    for l in range(depth):
        sh1, sc1, gt1, sh2, sc2, gt2 = [mod[l, :, None, i * D:(i + 1) * D] for i in range(6)]
        z_rwkv, z_s5, q_a, kv_a, kpe, z_lru, z_gate = _inproj(
            x, sc1, sh1, norm1_g[l].reshape(1, D), _pad_w_in(w_in[l]), tiles['tm'])
        y_a = _rwkv(z_rwkv, rwkv_mu[l], rwkv_w0[l], rwkv_w2[l], rwkv_a0[l], rwkv_a2[l], rwkv_g2[l],
                    rwkv_k_k[l], rwkv_k_a[l], rwkv_r_k[l].reshape(-1), rwkv_ln_w[l], rwkv_ln_b[l], tiles['tl'])
        y_b = from_tm(_s5(to_tm(z_s5), s5_lambda_re[l], s5_lambda_im[l], s5_b_re[l], s5_b_im[l], s5_c_re[l],
                          s5_c_im[l], s5_d[l], s5_log_dt[l], s5_glu_w[l], s5_glu_b[l], branch_norm_g[l, 0],
                          tiles['ts_s5']))
        q, k, v = _mla_prep(q_a, kv_a, kpe, pos_offset, mla_q_norm_g[l], mla_w_uq[l], mla_kv_norm_g[l],
                            mla_w_ukv[l], mla_q_head_g[l], mla_k_head_g[l], tiles['tm'])
        y_c = _attention(q, k, v, branch_norm_g[l, 1], tiles['tq'])
        y_d = from_tm(_lru(to_tm(z_lru), to_tm(z_gate), lru_conv_w[l], lru_conv_b[l], lru_w_a[l], lru_b_a[l],
                           lru_w_x[l], lru_b_x[l], lru_lambda[l], branch_norm_g[l, 2], tiles['ts_lru']))
        w_router = jnp.zeros((D, LANE), F32).at[:, :N_GROUPS].set(moe_w_group[l])
        w_router = w_router.at[:, N_GROUPS:N_GROUPS + N_EXPERTS].set(moe_w_expert[l])
        b_router = jnp.zeros((1, LANE), F32).at[0, :N_GROUPS].set(moe_b_group[l])
        b_router = b_router.at[0, N_GROUPS:N_GROUPS + N_EXPERTS].set(moe_b_expert[l])
        x, h2, logits = _outproj(x, y_a, y_b, y_c, y_d, w_out[l], gt1, sc2, sh2, norm2_g[l].reshape(1, D),
                                 w_router, b_router, tiles['tm'])
        blk_expert, n_used, row_tok, row_dst, row_w = _route(logits.reshape(T, LANE), T)
        y2 = _moe_experts(h2.reshape(T * SUBLANE, LANE), blk_expert, n_used, row_tok, row_dst, row_w,
                          moe_w1[l].astype(BF16), moe_w3[l].astype(BF16), moe_w2[l].astype(BF16))
        x = _combine(x, y2.reshape(TOP_K, -1, LANE), gt2, tiles['tm'])
    return x
```

```python
import functools
import math

import numpy as np
import jax
import jax.numpy as jnp
from jax import lax
from jax.experimental import pallas as pl
from jax.experimental.pallas import tpu as pltpu

F32 = jnp.float32
BF16 = jnp.bfloat16
HI = lax.Precision.HIGHEST

N_MIX = 4
RWKV_HEAD = 64
RWKV_W_RANK, RWKV_A_RANK, RWKV_G_RANK = 32, 32, 64
RWKV_LN_EPS = 64e-5
S5_GROUP, S5_STATE = 16, 64
MLA_HEADS, MLA_NOPE, MLA_ROPE, MLA_V = 4, 64, 32, 64
MLA_QK = MLA_NOPE + MLA_ROPE
ROPE_THETA = 10000.0
LRU_C = 8.0
CONV_WIDTH = 4
ATTN_CHUNK = 64
N_GROUPS, EXPERTS_PER_GROUP, TOP_K = 4, 8, 2
N_EXPERTS = N_GROUPS * EXPERTS_PER_GROUP
MOE_BLOCK = 256
NORM_EPS = 1e-6

LANE = 128
SUBLANE = 8
RWKV_CHUNK = 64
VMEM_LIMIT = 48 * 1024 * 1024


def _cp(sem, vmem=VMEM_LIMIT):
    return pltpu.CompilerParams(dimension_semantics=sem, vmem_limit_bytes=vmem)


def _dot(a, b):
    return jnp.dot(a, b, preferred_element_type=F32)


def _dot_hi(a, b):
    return jnp.dot(a, b, precision=HI, preferred_element_type=F32)


def _dot_nt(a, b):
    return lax.dot_general(a, b, (((1,), (1,)), ((), ())), preferred_element_type=F32)


def _dot_nt_hi(a, b):
    return lax.dot_general(a, b, (((1,), (1,)), ((), ())), precision=HI, preferred_element_type=F32)


def _softplus(x):
    return jnp.maximum(x, 0.0) + jnp.log1p(jnp.exp(-jnp.abs(x)))


def _rms(x, g, eps=NORM_EPS):
    return x * lax.rsqrt(jnp.mean(x * x, axis=-1, keepdims=True) + eps) * g


def _mod_kernel(c_ref, w_ref, b_ref, o_ref):
    c = c_ref[...]
    cond = c * jax.nn.sigmoid(c)
    o_ref[0] = _dot_hi(cond, w_ref[0]) + b_ref[0]


def _adaln_mod(c, ada_w, ada_b):
    L, D, D6 = ada_w.shape
    B = c.shape[0]
    nj = D6 // D
    return pl.pallas_call(
        _mod_kernel,
        grid=(L, nj),
        in_specs=[pl.BlockSpec((B, D), lambda l, j: (0, 0)),
                  pl.BlockSpec((1, D, D), lambda l, j: (l, 0, j)),
                  pl.BlockSpec((1, 1, D), lambda l, j: (l, 0, j))],
        out_specs=pl.BlockSpec((1, B, D), lambda l, j: (l, 0, j)),
        out_shape=jax.ShapeDtypeStruct((L, B, D6), F32),
        compiler_params=_cp(("parallel", "parallel")),
        name="adaln_mod",
    )(c, ada_w, ada_b.reshape(L, 1, D6))


_RWKV_IN = 3 * 256 + RWKV_W_RANK + RWKV_A_RANK + RWKV_G_RANK
_IN_COLS = (_RWKV_IN, 256, 256, 128, LANE, 256, 256)
_IN_OFFS = tuple(int(v) for v in np.cumsum((0,) + _IN_COLS))
_KPE_LANE0 = MLA_NOPE


def _pad_w_in(w_in):
    D = w_in.shape[0]
    o = np.cumsum((0, _RWKV_IN, 256, 256, 128, MLA_ROPE, 256, 256))
    pieces = [w_in[:, o[i]:o[i + 1]] for i in range(7)]
    kpe = jnp.zeros((D, LANE), w_in.dtype).at[:, _KPE_LANE0:_KPE_LANE0 + MLA_ROPE].set(pieces[4])
    pieces[4] = kpe
    return jnp.concatenate(pieces, axis=1).astype(BF16)


def _inproj_kernel(x_ref, sc_ref, sh_ref, g_ref, w_ref, *outs):
    x = x_ref[0]
    h = _rms(x, g_ref[...]) * (1.0 + sc_ref[0]) + sh_ref[0]
    z = _dot(h.astype(BF16), w_ref[...])
    for i, o_ref in enumerate(outs):
        o_ref[0] = z[:, _IN_OFFS[i]:_IN_OFFS[i + 1]]


def _inproj(x, sc, sh, g, w_pad, tm):
    B, S, D = x.shape
    row = lambda b, s: (b, s, 0)
    vec = lambda b, s: (b, 0, 0)
    return pl.pallas_call(
        _inproj_kernel,
        grid=(B, S // tm),
        in_specs=[pl.BlockSpec((1, tm, D), row),
                  pl.BlockSpec((1, 1, D), vec),
                  pl.BlockSpec((1, 1, D), vec),
                  pl.BlockSpec((1, D), lambda b, s: (0, 0)),
                  pl.BlockSpec(w_pad.shape, lambda b, s: (0, 0))],
        out_specs=[pl.BlockSpec((1, tm, n), row) for n in _IN_COLS],
        out_shape=[jax.ShapeDtypeStruct((B, S, n), F32) for n in _IN_COLS],
        compiler_params=_cp(("parallel", "parallel")),
        name="inproj",
    )(x, sc, sh, g, w_pad)


def _split_bf16(x, parts):
    out = []
    for _ in range(parts - 1):
        hi = x.astype(BF16)
        out.append(hi)
        x = x - hi.astype(F32)
    out.append(x.astype(BF16))
    return out


def _rwkv_kernel(z_ref, mu_ref, w0_ref, a0_ref, wl_ref, kk_ref, ka_ref, rk_ref, lnw_ref, lnb_ref,
                 hsum_ref, tri_ref, o_ref, zprev_ref, g_ref, y_ref):
    W = 256
    NH = W // RWKV_HEAD
    L = RWKV_CHUNK
    TL = z_ref.shape[1]

    @pl.when(pl.program_id(1) == 0)
    def _():
        zprev_ref[...] = jnp.zeros_like(zprev_ref)
        g_ref[...] = jnp.zeros_like(g_ref)

    hsum = hsum_ref[...]

    def head_sum(t):
        hi, lo_ = _split_bf16(t, 2)
        return _dot(hi, hsum) + _dot(lo_, hsum)

    z = z_ref[0]
    rows = lax.broadcasted_iota(jnp.int32, z.shape, 0)
    zp = jnp.where(rows == 0, zprev_ref[...], pltpu.roll(z, 1, axis=0))
    zprev_ref[...] = z[TL - 1:TL, :]
    zs = z + (zp - z) * mu_ref[...]
    r, k, v, lo = zs[:, 0:W], zs[:, W:2 * W], zs[:, 2 * W:3 * W], zs[:, 3 * W:3 * W + LANE]
    lane = lax.broadcasted_iota(jnp.int32, lo.shape, 1)
    act = jnp.where(lane < RWKV_W_RANK, jnp.tanh(lo),
                    jnp.where(lane < RWKV_W_RANK + RWKV_A_RANK, lo, jax.nn.sigmoid(lo)))
    up = _dot(act.astype(BF16), wl_ref[...])
    w = -_softplus(-(w0_ref[...] + up[:, 0:W])) - 0.5
    ld = -jnp.exp(w)
    a = jax.nn.sigmoid(a0_ref[...] + up[:, W:2 * W])
    g = up[:, 2 * W:3 * W]
    kk = k * kk_ref[...]
    kk = kk / jnp.maximum(jnp.sqrt(head_sum(kk * kk)), 1e-12)
    k2 = k * (1.0 + (a - 1.0) * ka_ref[...])
    kka = kk * a

    tri_b = tri_ref[0]
    tri_incl = tri_b > 0
    tri_strict = tri_ref[1] > 0
    eye = (lax.broadcasted_iota(jnp.int32, (RWKV_HEAD, RWKV_HEAD), 0)
           == lax.broadcasted_iota(jnp.int32, (RWKV_HEAD, RWKV_HEAD), 1)).astype(BF16)

    NC = TL // L
    pairs = [(j, h) for j in range(NC) for h in range(NH)]
    hsl = [slice(h * RWKV_HEAD, (h + 1) * RWKV_HEAD) for h in range(NH)]
    csl = [slice(j * L, (j + 1) * L) for j in range(NC)]
    bt, rt, a_s, k_s, a_e, k_e, v_c, p_l = [], [], [], [], [], [], [], []
    for j in range(NC):
        ld_c = ld[csl[j]]
        cs = sum(_dot(tri_b, p) for p in _split_bf16(ld_c, 3))
        cs_l = cs[L - 1:L, :]
        p_inv = jnp.exp(-cs)
        p_end = jnp.exp(cs_l - cs)
        p_l.append(jnp.exp(cs_l))
        bt.append(-kk[csl[j]] * jnp.exp(cs - ld_c))
        rt.append(r[csl[j]] * jnp.exp(cs))
        a_s.append((kka[csl[j]] * p_inv).astype(BF16))
        k_s.append((k2[csl[j]] * p_inv).astype(BF16))
        a_e.append((kka[csl[j]] * p_end).astype(BF16))
        k_e.append((k2[csl[j]] * p_end).astype(BF16))
        v_c.append(v[csl[j]].astype(BF16))

    x = [jnp.concatenate([bt[j][:, hsl[h]], rt[j][:, hsl[h]]], axis=0).astype(BF16) for j, h in pairs]
    ga = [_dot_nt(x[i], a_s[j][:, hsl[h]]) for i, (j, h) in enumerate(pairs)]
    gk = [_dot_nt(x[i], k_s[j][:, hsl[h]]) for i, (j, h) in enumerate(pairs)]
    m_ba = [jnp.where(tri_strict, t[:L], 0.0).astype(BF16) for t in ga]
    m_ra = [jnp.where(tri_incl, t[L:], 0.0).astype(BF16) for t in ga]
    m_bk = [jnp.where(tri_strict, t[:L], 0.0).astype(BF16) for t in gk]
    m_rk = [jnp.where(tri_incl, t[L:], 0.0).astype(BF16) for t in gk]
    v_h = [v_c[j][:, hsl[h]] for j, h in pairs]
    wn = [jnp.concatenate([bt[j][:, hsl[h]], _dot(m_bk[i], v_h[i])], axis=1) for i, (j, h) in enumerate(pairs)]
    n_pow = m_ba
    wn = [w_ + _dot(n_, w_.astype(BF16)) for w_, n_ in zip(wn, n_pow)]
    for _ in range(5):
        n_pow = [_dot(n_, n_).astype(BF16) for n_ in n_pow]
        wn = [w_ + _dot(n_, w_.astype(BF16)) for w_, n_ in zip(wn, n_pow)]
    wn_b = [w_.astype(BF16) for w_ in wn]
    mw = [_dot(m_, w_) for m_, w_ in zip(m_ra, wn_b)]
    r2 = [(rt[j][:, hsl[h]] + mw[i][:, :RWKV_HEAD]).astype(BF16) for i, (j, h) in enumerate(pairs)]
    y0 = [mw[i][:, RWKV_HEAD:] + _dot(m_rk[i], v_h[i]) for i in range(len(pairs))]
    b2_t = [_dot_nt(eye, w_[:, :RWKV_HEAD]).astype(BF16) for w_ in wn_b]
    c_m = [_dot(b2_t[i], a_e[j][:, hsl[h]]).astype(BF16) for i, (j, h) in enumerate(pairs)]
    uv_t = [_dot_nt(eye, jnp.concatenate([wn_b[i][:, RWKV_HEAD:], v_h[i]], axis=0)).astype(BF16)
            for i in range(len(pairs))]
    d_t = [_dot(uv_t[i], jnp.concatenate([a_e[j][:, hsl[h]], k_e[j][:, hsl[h]]], axis=0))
           for i, (j, h) in enumerate(pairs)]
    for i, (j, h) in enumerate(pairs):
        g0 = g_ref[h]
        g_hi, g_lo = _split_bf16(g0, 2)
        y_ref[csl[j], hsl[h]] = _dot_nt(r2[i], g_hi) + _dot_nt(r2[i], g_lo) + y0[i]
        g_ref[h] = g0 * p_l[j][:, hsl[h]] + _dot(g_hi, c_m[i]) + _dot(g_lo, c_m[i]) + d_t[i]

    y = y_ref[...]
    inv_n = 1.0 / RWKV_HEAD
    mean = head_sum(y) * inv_n
    yc = y - mean
    var = head_sum(yc * yc) * inv_n
    yn = yc * lax.rsqrt(var + RWKV_LN_EPS) * lnw_ref[...] + lnb_ref[...]
    bonus = head_sum(r * k2 * rk_ref[...]) * v
    o_ref[0] = ((yn + bonus) * g).astype(o_ref.dtype)


def _rwkv(z, mu, w0, w2, a0, a2, g2, k_k, k_a, r_k, ln_w, ln_b, tl):
    B, S, _ = z.shape
    W = 256
    L = RWKV_CHUNK
    wl = jnp.zeros((LANE, 3 * W), F32)
    wl = wl.at[0:RWKV_W_RANK, 0:W].set(w2)
    wl = wl.at[RWKV_W_RANK:RWKV_W_RANK + RWKV_A_RANK, W:2 * W].set(a2)
    wl = wl.at[RWKV_W_RANK + RWKV_A_RANK:LANE, 2 * W:3 * W].set(g2).astype(BF16)
    hid = np.arange(W) // RWKV_HEAD
    hsum = jnp.asarray((hid[:, None] == hid[None, :]).astype(np.float32), dtype=BF16)
    t = np.arange(L)
    tri = jnp.asarray(np.stack([(t[None, :] <= t[:, None]), (t[None, :] < t[:, None])]).astype(np.float32), dtype=BF16)
    row2 = lambda v_: v_.reshape(1, -1)
    const = lambda shape: pl.BlockSpec(shape, lambda b, s: (0,) * len(shape))
    return pl.pallas_call(
        _rwkv_kernel,
        grid=(B, S // tl),
        in_specs=[pl.BlockSpec((1, tl, _RWKV_IN), lambda b, s: (b, s, 0)),
                  const((1, _RWKV_IN)), const((1, W)), const((1, W)), const((LANE, 3 * W)),
                  const((1, W)), const((1, W)), const((1, W)), const((1, W)), const((1, W)),
                  const((W, W)), const((2, L, L))],
        out_specs=pl.BlockSpec((1, tl, W), lambda b, s: (b, s, 0)),
        out_shape=jax.ShapeDtypeStruct((B, S, W), BF16),
        scratch_shapes=[pltpu.VMEM((1, _RWKV_IN), F32),
                        pltpu.VMEM((W // RWKV_HEAD, RWKV_HEAD, RWKV_HEAD), F32),
                        pltpu.VMEM((tl, W), F32)],
        compiler_params=_cp(("parallel", "arbitrary")),
        name="rwkv7",
    )(z, row2(mu), row2(w0), row2(a0), wl, row2(k_k), row2(k_a), row2(r_k), row2(ln_w), row2(ln_b), hsum, tri)


def _s5_kernel(u_ref, wb_ref, are_ref, aim_ref, wc_ref, d_ref, gw_ref, gb_ref, ng_ref, o_ref,
               hre_ref, him_ref, sre_ref, sim_ref):
    nb = SUBLANE
    rows = u_ref.shape[0]
    P = are_ref.shape[1]

    @pl.when(pl.program_id(0) == 0)
    def _():
        sre_ref[...] = jnp.zeros_like(sre_ref)
        sim_ref[...] = jnp.zeros_like(sim_ref)

    u = u_ref[...]
    bu = _dot(u.astype(BF16), wb_ref[...])
    hre_ref[...] = bu[:, :P]
    him_ref[...] = bu[:, P:]
    a_re = jnp.broadcast_to(are_ref[...], (nb, P))
    a_im = jnp.broadcast_to(aim_ref[...], (nb, P))

    def step(t, carry):
        h_re, h_im = carry
        i = pl.multiple_of(t * nb, nb)
        n_re = a_re * h_re - a_im * h_im + hre_ref[pl.ds(i, nb), :]
        n_im = a_re * h_im + a_im * h_re + him_ref[pl.ds(i, nb), :]
        hre_ref[pl.ds(i, nb), :] = n_re
        him_ref[pl.ds(i, nb), :] = n_im
        return n_re, n_im

    h_re, h_im = lax.fori_loop(0, rows // nb, step, (sre_ref[...], sim_ref[...]))
    sre_ref[...] = h_re
    sim_ref[...] = h_im
    y = _dot(hre_ref[...].astype(BF16), wc_ref[0]) + _dot(him_ref[...].astype(BF16), wc_ref[1])
    y = jax.nn.gelu(y + d_ref[...] * u)
    y = y * jax.nn.sigmoid(_dot(y.astype(BF16), gw_ref[...]) + gb_ref[...])
    o_ref[...] = _rms(y, ng_ref[...]).astype(o_ref.dtype)


def _s5(u_tm, lam_re, lam_im, b_re, b_im, c_re, c_im, d, log_dt, glu_w, glu_b, norm_g, ts):
    rows_total, W = u_tm.shape
    G, P = lam_re.shape
    dt = jnp.exp(log_dt)[:, None]
    mag = jnp.exp(lam_re * dt)
    a_re = mag * jnp.cos(lam_im * dt)
    a_im = mag * jnp.sin(lam_im * dt)
    den = lam_re * lam_re + lam_im * lam_im
    q_re = ((a_re - 1.0) * lam_re + a_im * lam_im) / den
    q_im = (a_im * lam_re - (a_re - 1.0) * lam_im) / den
    bb_re = q_re[..., None] * b_re - q_im[..., None] * b_im
    bb_im = q_re[..., None] * b_im + q_im[..., None] * b_re
    eye = jnp.eye(G, dtype=F32)
    wb_re = jnp.einsum('gpi,gh->gihp', bb_re, eye).reshape(W, G * P)
    wb_im = jnp.einsum('gpi,gh->gihp', bb_im, eye).reshape(W, G * P)
    wb = jnp.concatenate([wb_re, wb_im], axis=1).astype(BF16)
    wc_re = jnp.einsum('gip,gh->gphi', c_re, eye).reshape(G * P, W)
    wc_im = jnp.einsum('gip,gh->gphi', -c_im, eye).reshape(G * P, W)
    wc = jnp.stack([wc_re, wc_im]).astype(BF16)
    GP = G * P
    rows = ts * SUBLANE
    const = lambda shape: pl.BlockSpec(shape, lambda s: (0,) * len(shape))
    return pl.pallas_call(
        _s5_kernel,
        grid=(rows_total // rows,),
        in_specs=[pl.BlockSpec((rows, W), lambda s: (s, 0)),
                  const((W, 2 * GP)), const((1, GP)), const((1, GP)), const((2, GP, W)),
                  const((1, W)), const((W, W)), const((1, W)), const((1, W))],
        out_specs=pl.BlockSpec((rows, W), lambda s: (s, 0)),
        out_shape=jax.ShapeDtypeStruct((rows_total, W), BF16),
        scratch_shapes=[pltpu.VMEM((rows, GP), F32), pltpu.VMEM((rows, GP), F32),
                        pltpu.VMEM((SUBLANE, GP), F32), pltpu.VMEM((SUBLANE, GP), F32)],
        compiler_params=_cp(("arbitrary",)),
        name="s5",
    )(u_tm, wb, a_re.reshape(1, GP), a_im.reshape(1, GP), wc, d.reshape(1, W), glu_w.astype(BF16),
      glu_b.reshape(1, W), norm_g.reshape(1, W))


def _lru_kernel(x_ref, gate_ref, cw_ref, cb_ref, wax_ref, bax_ref, lam_ref, ng_ref, o_ref,
                xc_ref, a_ref, b_ref, hs_ref):
    nb = SUBLANE
    rows, W = x_ref.shape
    halo = (CONV_WIDTH - 1) * nb

    @pl.when(pl.program_id(0) == 0)
    def _():
        xc_ref[...] = jnp.zeros_like(xc_ref)
        hs_ref[...] = jnp.zeros_like(hs_ref)

    x = x_ref[...]
    xf = jnp.concatenate([xc_ref[...], x], axis=0)
    xc_ref[...] = x[rows - halo:, :]
    xc = cb_ref[...]
    for j in range(CONV_WIDTH):
        xc = xc + cw_ref[j:j + 1, :] * xf[j * nb:j * nb + rows, :]
    gates = _dot(xc.astype(BF16), wax_ref[...]) + bax_ref[...]
    r = jax.nn.sigmoid(gates[:, :W])
    i = jax.nn.sigmoid(gates[:, W:])
    log_a = -LRU_C * r * _softplus(-lam_ref[...])
    a = jnp.exp(log_a)
    a_ref[...] = a
    b_ref[...] = jnp.sqrt(1.0 - jnp.exp(2.0 * log_a)) * (i * xc)

    def step(t, h):
        i0 = pl.multiple_of(t * nb, nb)
        h = a_ref[pl.ds(i0, nb), :] * h + b_ref[pl.ds(i0, nb), :]
        b_ref[pl.ds(i0, nb), :] = h
        return h

    hs_ref[...] = lax.fori_loop(0, rows // nb, step, hs_ref[...])
    y = b_ref[...] * jax.nn.gelu(gate_ref[...])
    o_ref[...] = _rms(y, ng_ref[...]).astype(o_ref.dtype)


def _lru(x_tm, gate_tm, conv_w, conv_b, w_a, b_a, w_x, b_x, lam, norm_g, ts):
    rows_total, W = x_tm.shape
    nblk, bs, _ = w_a.shape
    eye = jnp.eye(nblk, dtype=F32)
    bd = lambda w_: jnp.einsum('nkj,nm->nkmj', w_, eye).reshape(W, W)
    wax = jnp.concatenate([bd(w_a), bd(w_x)], axis=1).astype(BF16)
    bax = jnp.concatenate([b_a, b_x]).reshape(1, 2 * W)
    rows = ts * SUBLANE
    const = lambda shape: pl.BlockSpec(shape, lambda s: (0,) * len(shape))
    return pl.pallas_call(
        _lru_kernel,
        grid=(rows_total // rows,),
        in_specs=[pl.BlockSpec((rows, W), lambda s: (s, 0)), pl.BlockSpec((rows, W), lambda s: (s, 0)),
                  const((CONV_WIDTH, W)), const((1, W)), const((W, 2 * W)), const((1, 2 * W)),
                  const((1, W)), const((1, W))],
        out_specs=pl.BlockSpec((rows, W), lambda s: (s, 0)),
        out_shape=jax.ShapeDtypeStruct((rows_total, W), BF16),
        scratch_shapes=[pltpu.VMEM(((CONV_WIDTH - 1) * SUBLANE, W), F32), pltpu.VMEM((rows, W), F32),
                        pltpu.VMEM((rows, W), F32), pltpu.VMEM((SUBLANE, W), F32)],
        compiler_params=_cp(("arbitrary",)),
        name="rglru",
    )(x_tm, gate_tm, conv_w, conv_b.reshape(1, W), wax, bax, lam.reshape(1, W), norm_g.reshape(1, W))


def _mla_prep_kernel(pos_ref, qa_ref, kva_ref, kpe_ref, gq_ref, gkv_ref, wq_ref, wk_ref, wv_ref,
                     gqh_ref, gkh_ref, freq_ref, q_ref, k_ref, v_ref):
    tm = qa_ref.shape[1]
    b = pl.program_id(0)
    s0 = pl.program_id(1) * tm
    pos = (pos_ref[b] + s0 + lax.broadcasted_iota(jnp.int32, (tm, LANE), 0)).astype(F32)
    ang = pos * freq_ref[...]
    cos_t = jnp.cos(ang)
    sin_t = jnp.sin(ang)
    lane = lax.broadcasted_iota(jnp.int32, (tm, LANE), 1)
    half = MLA_ROPE // 2
    first = (lane >= MLA_NOPE) & (lane < MLA_NOPE + half)
    second = (lane >= MLA_NOPE + half) & (lane < MLA_QK)

    def rope(t):
        rot = jnp.where(first, -pltpu.roll(t, LANE - half, axis=1),
                        jnp.where(second, pltpu.roll(t, half, axis=1), 0.0))
        return t * cos_t + rot * sin_t

    def head_norm(t, g):
        ms = jnp.sum(t * t, axis=-1, keepdims=True) * (1.0 / MLA_QK)
        return t * lax.rsqrt(ms + NORM_EPS) * g

    qn = _rms(qa_ref[0], gq_ref[...]).astype(BF16)
    kvn = _rms(kva_ref[0], gkv_ref[...]).astype(BF16)
    q_all = _dot(qn, wq_ref[...])
    k_all = _dot(kvn, wk_ref[...])
    v_all = _dot(kvn, wv_ref[...])
    kpe = kpe_ref[0]
    scale = MLA_QK ** -0.5
    for h in range(MLA_HEADS):
        hs = slice(h * LANE, (h + 1) * LANE)
        q_ref[0, h] = (rope(head_norm(q_all[:, hs], gqh_ref[...])) * scale).astype(q_ref.dtype)
        k_ref[0, h] = rope(head_norm(k_all[:, hs] + kpe, gkh_ref[...])).astype(k_ref.dtype)
        v_ref[0, h] = v_all[:, hs].astype(v_ref.dtype)


def _mla_prep(q_a, kv_a, kpe, pos_offset, q_norm_g, w_uq, kv_norm_g, w_ukv, q_head_g, k_head_g, tm):
    B, S, QR = q_a.shape
    KVR = kv_a.shape[2]
    H = MLA_HEADS
    wq = jnp.zeros((QR, H, LANE), F32).at[:, :, :MLA_QK].set(w_uq.reshape(QR, H, MLA_QK)).reshape(QR, H * LANE)
    wkv = w_ukv.reshape(KVR, H, MLA_NOPE + MLA_V)
    wk = jnp.zeros((KVR, H, LANE), F32).at[:, :, :MLA_NOPE].set(wkv[:, :, :MLA_NOPE]).reshape(KVR, H * LANE)
    wv = jnp.zeros((KVR, H, LANE), F32).at[:, :, :MLA_V].set(wkv[:, :, MLA_NOPE:]).reshape(KVR, H * LANE)
    pad_g = lambda g: jnp.zeros((1, LANE), F32).at[0, :MLA_QK].set(g)
    half = MLA_ROPE // 2
    inv_freq = np.power(np.float32(ROPE_THETA), -np.arange(half, dtype=np.float32) * np.float32(2.0) / np.float32(MLA_ROPE))
    freq = np.zeros((1, LANE), np.float32)
    freq[0, MLA_NOPE:MLA_NOPE + half] = inv_freq
    freq[0, MLA_NOPE + half:MLA_QK] = inv_freq
    const = lambda shape: pl.BlockSpec(shape, lambda b, s, p: (0,) * len(shape))
    row = lambda n: pl.BlockSpec((1, tm, n), lambda b, s, p: (b, s, 0))
    hout = pl.BlockSpec((1, H, tm, LANE), lambda b, s, p: (b, 0, s, 0))
    grid_spec = pltpu.PrefetchScalarGridSpec(
        num_scalar_prefetch=1, grid=(B, S // tm),
        in_specs=[row(QR), row(KVR), row(LANE), const((1, QR)), const((1, KVR)),
                  const((QR, H * LANE)), const((KVR, H * LANE)), const((KVR, H * LANE)),
                  const((1, LANE)), const((1, LANE)), const((1, LANE))],
        out_specs=[hout, hout, hout])
    return pl.pallas_call(
        _mla_prep_kernel,
        grid_spec=grid_spec,
        out_shape=[jax.ShapeDtypeStruct((B, H, S, LANE), BF16)] * 3,
        compiler_params=_cp(("parallel", "parallel")),
        name="mla_prep",
    )(pos_offset, q_a, kv_a, kpe, q_norm_g.reshape(1, QR), kv_norm_g.reshape(1, KVR),
      wq.astype(BF16), wk.astype(BF16), wv.astype(BF16), pad_g(q_head_g), pad_g(k_head_g), jnp.asarray(freq))


def _attn_kernel(q_ref, k_ref, v_ref, ng_ref, o_ref, m_ref, l_ref, acc_ref):
    H = q_ref.shape[1]
    tq = q_ref.shape[2]
    qi = pl.program_id(1)
    ki = pl.program_id(2)

    @pl.when(ki == 0)
    def _():
        m_ref[...] = jnp.full_like(m_ref, -1e30)
        l_ref[...] = jnp.zeros_like(l_ref)
        acc_ref[...] = jnp.zeros_like(acc_ref)

    def update(masked):
        if masked:
            qc = lax.broadcasted_iota(jnp.int32, (tq, tq), 0) // ATTN_CHUNK
            kc = lax.broadcasted_iota(jnp.int32, (tq, tq), 1) // ATTN_CHUNK
            visible = kc <= qc
        for h in range(H):
            s = _dot_nt(q_ref[0, h], k_ref[0, h])
            if masked:
                s = jnp.where(visible, s, -1e30)
            m_old = m_ref[h]
            m_new = jnp.maximum(m_old, jnp.max(s, axis=-1, keepdims=True))
            alpha = jnp.exp(m_old - m_new)
            p = jnp.exp(s - m_new)
            l_ref[h] = alpha * l_ref[h] + jnp.sum(p, axis=-1, keepdims=True)
            acc_ref[h] = alpha * acc_ref[h] + _dot(p.astype(BF16), v_ref[0, h])
            m_ref[h] = m_new

    @pl.when(ki < qi)
    def _():
        update(False)

    @pl.when(ki == qi)
    def _():
        update(True)
        o = jnp.concatenate([(acc_ref[h] / l_ref[h])[:, :MLA_V] for h in range(H)], axis=-1)
        o_ref[0] = _rms(o, ng_ref[...]).astype(o_ref.dtype)


def _attention(q, k, v, norm_g, tq):
    B, H, S, _ = q.shape
    W = H * MLA_V
    nq = S // tq
    qspec = pl.BlockSpec((1, H, tq, LANE), lambda b, i, j: (b, 0, i, 0))
    kspec = pl.BlockSpec((1, H, tq, LANE), lambda b, i, j: (b, 0, jnp.minimum(i, j), 0))
    return pl.pallas_call(
        _attn_kernel,
        grid=(B, nq, nq),
        in_specs=[qspec, kspec, kspec, pl.BlockSpec((1, W), lambda b, i, j: (0, 0))],
        out_specs=pl.BlockSpec((1, tq, W), lambda b, i, j: (b, i, 0)),
        out_shape=jax.ShapeDtypeStruct((B, S, W), BF16),
        scratch_shapes=[pltpu.VMEM((H, tq, 1), F32), pltpu.VMEM((H, tq, 1), F32),
                        pltpu.VMEM((H, tq, LANE), F32)],
        compiler_params=_cp(("parallel", "parallel", "arbitrary")),
        name="mla_attn",
    )(q, k, v, norm_g.reshape(1, W))


def _store_token_tiles(ref, val):
    n, d = val.shape
    for s in range(d // LANE):
        ref[pl.ds(s, n, stride=SUBLANE), :] = val[:, s * LANE:(s + 1) * LANE]


def _load_token_tiles(ref, n):
    return jnp.concatenate([ref[pl.ds(s, n, stride=SUBLANE), :] for s in range(SUBLANE)], axis=1)


def _outproj_kernel(x_ref, ya_ref, yb_ref, yc_ref, yd_ref, wo_ref, gt_ref, sc_ref, sh_ref, g_ref,
                    wr_ref, br_ref, xo_ref, h_ref, lg_ref):
    acc = _dot(ya_ref[0], wo_ref[0])
    acc = acc + _dot(yb_ref[0], wo_ref[1])
    acc = acc + _dot(yc_ref[0], wo_ref[2])
    acc = acc + _dot(yd_ref[0], wo_ref[3])
    x = x_ref[0] + gt_ref[0] * acc
    xo_ref[0] = x
    h = _rms(x, g_ref[...]) * (1.0 + sc_ref[0]) + sh_ref[0]
    _store_token_tiles(h_ref.at[0], h)
    h_hi, h_lo = _split_bf16(h, 2)
    lg_ref[0] = _dot(h_hi, wr_ref[0]) + (_dot(h_lo, wr_ref[0]) + _dot(h_hi, wr_ref[1])) + br_ref[...]


def _outproj(x, ya, yb, yc, yd, w_out, gt, sc, sh, g, w_router, b_router, tm):
    B, S, D = x.shape
    W = D // N_MIX
    row = lambda n: pl.BlockSpec((1, tm, n), lambda b, s: (b, s, 0))
    vec = pl.BlockSpec((1, 1, D), lambda b, s: (b, 0, 0))
    const = lambda shape: pl.BlockSpec(shape, lambda b, s: (0,) * len(shape))
    return pl.pallas_call(
        _outproj_kernel,
        grid=(B, S // tm),
        in_specs=[row(D), row(W), row(W), row(W), row(W), const((N_MIX, W, D)), vec, vec, vec,
                  const((1, D)), const((2, D, LANE)), const((1, LANE))],
        out_specs=[row(D), pl.BlockSpec((1, tm * SUBLANE, LANE), lambda b, s: (b, s, 0)), row(LANE)],
        out_shape=[jax.ShapeDtypeStruct((B, S, D), F32), jax.ShapeDtypeStruct((B, S * SUBLANE, LANE), F32),
                   jax.ShapeDtypeStruct((B, S, LANE), F32)],
        compiler_params=_cp(("parallel", "parallel")),
        name="outproj",
    )(x, ya, yb, yc, yd, w_out.reshape(N_MIX, W, D).astype(BF16), gt, sc, sh, g,
      jnp.stack(_split_bf16(w_router, 2)), b_router)


_MOE_PAD_ROWS = MOE_BLOCK // TOP_K


def _moe_kernel(be_ref, nused_ref, tok_ref, tokn_ref, dst_ref, w_ref, h_hbm, w1_ref, w3_ref, w2_ref, o_hbm,
                xbuf, ybuf, gsem, ssem):
    i = pl.program_id(0)
    R = MOE_BLOCK
    TR = SUBLANE
    nused = nused_ref[0]
    slot = lax.rem(i, 2)
    other = 1 - slot

    def gather_rows(idx_ref, s):
        def body(r, c):
            src = pl.multiple_of(idx_ref[0, 0, r], TR)
            pltpu.make_async_copy(h_hbm.at[pl.ds(src, TR)], xbuf.at[s, pl.ds(pl.multiple_of(r * TR, TR), TR)],
                                  gsem.at[s]).start()
            return c
        lax.fori_loop(0, R, body, 0, unroll=8)

    def scatter_rows(s):
        def body(r, c):
            dst = pl.multiple_of(dst_ref[0, 0, r], TR)
            pltpu.make_async_copy(ybuf.at[s, pl.ds(pl.multiple_of(r * TR, TR), TR)], o_hbm.at[pl.ds(dst, TR)],
                                  ssem.at[s]).start(priority=1)
            return c
        lax.fori_loop(0, R, body, 0, unroll=8)

    def wait_gather(s):
        pltpu.make_async_copy(h_hbm.at[pl.ds(0, R * TR)], xbuf.at[s], gsem.at[s]).wait()

    def wait_scatter(s):
        pltpu.make_async_copy(ybuf.at[s], o_hbm.at[pl.ds(0, R * TR)], ssem.at[s]).wait()

    @pl.when(i == 0)
    def _():
        plane = o_hbm.shape[0] // TOP_K
        npad = _MOE_PAD_ROWS * TR
        ybuf[1] = jnp.zeros(ybuf.shape[1:], ybuf.dtype)
        for j in range(TOP_K):
            pad = pltpu.make_async_copy(ybuf.at[1, pl.ds(0, npad)],
                                        o_hbm.at[pl.ds((j + 1) * plane - npad, npad)], ssem.at[1])
            pad.start()
            pad.wait()
        gather_rows(tok_ref, 0)

    @pl.when(i < nused)
    def _():
        @pl.when(i + 1 < nused)
        def _():
            gather_rows(tokn_ref, other)

        wait_gather(slot)

        @pl.when(i >= 2)
        def _():
            wait_scatter(slot)

        xb = _load_token_tiles(xbuf.at[slot], R).astype(BF16)
        hid = _dot(xb, w1_ref[0])
        hid = hid * jax.nn.sigmoid(hid) * _dot(xb, w3_ref[0])
        y = _dot(hid.astype(BF16), w2_ref[0])
        eye_r = (lax.broadcasted_iota(jnp.int32, (R, R), 0) == lax.broadcasted_iota(jnp.int32, (R, R), 1)).astype(F32)
        wcol = _dot_nt_hi(eye_r, jnp.broadcast_to(w_ref[0], (LANE, R)))
        _store_token_tiles(ybuf.at[slot], y * wcol[:, 0:1])
        scatter_rows(slot)

        @pl.when(i == nused - 1)
        def _():
            wait_scatter(slot)

            @pl.when(i >= 1)
            def _():
                wait_scatter(other)


def _moe_experts(h2, blk_expert, n_used, row_tok, row_dst, row_w, w1, w3, w2):
    E, D, F = w1.shape
    T = h2.shape[0] // SUBLANE
    nb = blk_expert.shape[0]
    R = MOE_BLOCK
    blk = lambda i, be, nu: (i, 0, 0)
    nxt = lambda i, be, nu: (jnp.minimum(i + 1, nb - 1), 0, 0)
    wsel = lambda i, be, nu: (be[i], 0, 0)
    plane = (T + _MOE_PAD_ROWS) * SUBLANE
    row_tok = (row_tok * SUBLANE).reshape(nb, 1, R)
    row_dst = ((row_dst % TOP_K) * plane + (row_dst // TOP_K) * SUBLANE).reshape(nb, 1, R)
    grid_spec = pltpu.PrefetchScalarGridSpec(
        num_scalar_prefetch=2, grid=(nb,),
        in_specs=[pl.BlockSpec((1, 1, R), blk, memory_space=pltpu.SMEM),
                  pl.BlockSpec((1, 1, R), nxt, memory_space=pltpu.SMEM),
                  pl.BlockSpec((1, 1, R), blk, memory_space=pltpu.SMEM),
                  pl.BlockSpec((1, 1, R), blk),
                  pl.BlockSpec(memory_space=pl.ANY),
                  pl.BlockSpec((1, D, F), wsel), pl.BlockSpec((1, D, F), wsel), pl.BlockSpec((1, F, D), wsel)],
        out_specs=pl.BlockSpec(memory_space=pl.ANY),
        scratch_shapes=[pltpu.VMEM((2, R * SUBLANE, LANE), F32), pltpu.VMEM((2, R * SUBLANE, LANE), F32),
                        pltpu.SemaphoreType.DMA((2,)), pltpu.SemaphoreType.DMA((2,))])
    return pl.pallas_call(
        _moe_kernel,
        grid_spec=grid_spec,
        out_shape=jax.ShapeDtypeStruct((TOP_K * plane, LANE), F32),
        compiler_params=_cp(("arbitrary",)),
        name="moe_experts",
    )(blk_expert, n_used, row_tok, row_tok, row_dst, row_w.reshape(nb, 1, R), h2, w1, w3, w2)


def _route(logits, T):
    p_group = jax.nn.softmax(logits[:, :N_GROUPS], axis=-1)
    g_sel = jnp.argmax(p_group, axis=-1).astype(jnp.int32)
    g_prob = jnp.max(p_group, axis=-1)
    logit_e = logits[:, N_GROUPS:N_GROUPS + N_EXPERTS].reshape(T, N_GROUPS, EXPERTS_PER_GROUP)
    logit_e = jnp.take_along_axis(logit_e, g_sel[:, None, None], axis=1)[:, 0]
    p_e = jax.nn.softmax(logit_e, axis=-1)
    top_p, top_i = lax.top_k(p_e, TOP_K)
    gate = (g_prob[:, None] * top_p / jnp.sum(top_p, axis=-1, keepdims=True)).reshape(-1)
    expert = (g_sel[:, None] * EXPERTS_PER_GROUP + top_i.astype(jnp.int32)).reshape(-1)
    n_assign = T * TOP_K
    n_blocks = -(-n_assign // MOE_BLOCK) + N_EXPERTS
    order = jnp.argsort(expert).astype(jnp.int32)
    counts = jnp.sum((expert[:, None] == jnp.arange(N_EXPERTS, dtype=jnp.int32)[None, :]).astype(jnp.int32), axis=0)
    start = jnp.cumsum(counts) - counts
    pcounts = (counts + MOE_BLOCK - 1) // MOE_BLOCK * MOE_BLOCK
    pend = jnp.cumsum(pcounts)
    pstart = pend - pcounts
    blk_start = jnp.arange(n_blocks, dtype=jnp.int32) * MOE_BLOCK
    blk_expert = jnp.minimum(jnp.sum((pend[None, :] <= blk_start[:, None]).astype(jnp.int32), axis=1), N_EXPERTS - 1)
    n_used = (pend[-1] // MOE_BLOCK).astype(jnp.int32).reshape(1)
    lane = jnp.arange(MOE_BLOCK, dtype=jnp.int32)[None, :]
    off = blk_start[:, None] + lane - pstart[blk_expert][:, None]
    valid = off < counts[blk_expert][:, None]
    pos = jnp.clip(start[blk_expert][:, None] + off, 0, n_assign - 1)
    asg = order[pos]
    row_tok = jnp.where(valid, asg // TOP_K, 0)
    row_dst = jnp.where(valid, asg, n_assign + lane)
    row_w = jnp.where(valid, gate[asg], 0.0)
    return blk_expert.astype(jnp.int32), n_used, row_tok, row_dst, row_w


def _combine_kernel(x_ref, y0_ref, y1_ref, gt_ref, o_ref):
    tm = x_ref.shape[1]
    y = _load_token_tiles(y0_ref.at[0], tm) + _load_token_tiles(y1_ref.at[0], tm)
    o_ref[0] = x_ref[0] + gt_ref[0] * y


def _combine(x, y2, gt, tm):
    B, S, D = x.shape
    nt = S // tm
    return pl.pallas_call(
        _combine_kernel,
        grid=(B, nt),
        in_specs=[pl.BlockSpec((1, tm, D), lambda b, s: (b, s, 0)),
                  pl.BlockSpec((1, tm * SUBLANE, LANE), lambda b, s: (0, b * nt + s, 0)),
                  pl.BlockSpec((1, tm * SUBLANE, LANE), lambda b, s: (1, b * nt + s, 0)),
                  pl.BlockSpec((1, 1, D), lambda b, s: (b, 0, 0))],
        out_specs=pl.BlockSpec((1, tm, D), lambda b, s: (b, s, 0)),
        out_shape=jax.ShapeDtypeStruct((B, S, D), F32),
        compiler_params=_cp(("parallel", "parallel")),
        name="moe_combine",
    )(x, y2, y2, gt)


def _tiles(S):
    big = S >= 2048
    return dict(tm=512 if big else 256, tl=256, ts_s5=64, ts_lru=128 if big else 64,
                tq=1024 if big else 256)


def kernel(x, c, pos_offset, ada_w, ada_b, norm1_g, w_in, rwkv_mu, rwkv_w0, rwkv_w2, rwkv_a0, rwkv_a2, rwkv_g2, rwkv_k_k, rwkv_k_a, rwkv_r_k, rwkv_ln_w, rwkv_ln_b, s5_lambda_re, s5_lambda_im, s5_b_re, s5_b_im, s5_c_re, s5_c_im, s5_d, s5_log_dt, s5_glu_w, s5_glu_b, mla_q_norm_g, mla_w_uq, mla_kv_norm_g, mla_w_ukv, mla_q_head_g, mla_k_head_g, lru_conv_w, lru_conv_b, lru_w_a, lru_b_a, lru_w_x, lru_b_x, lru_lambda, branch_norm_g, w_out, norm2_g, moe_w_group, moe_b_group, moe_w_expert, moe_b_expert, moe_w1, moe_w3, moe_w2):
    B, S, D = x.shape
    depth = ada_w.shape[0]
    T = B * S
    W = D // N_MIX
    assert B == SUBLANE, "time-major scans put the batch on the 8 sublanes"
    assert D == SUBLANE * LANE, "token-tile layout: one (8, 128) tile per token row"
    tiles = _tiles(S)
    mod = _adaln_mod(c, ada_w, ada_b)

    def to_tm(t):
        return jnp.swapaxes(t, 0, 1).reshape(S * B, t.shape[-1])

    def from_tm(t):
        return jnp.swapaxes(t.reshape(S, B, t.shape[-1]), 0, 1)

    for l in range(depth):
        sh1, sc1, gt1, sh2, sc2, gt2 = [mod[l, :, None, i * D:(i + 1) * D] for i in range(6)]
        z_rwkv, z_s5, q_a, kv_a, kpe, z_lru, z_gate = _inproj(
            x, sc1, sh1, norm1_g[l].reshape(1, D), _pad_w_in(w_in[l]), tiles['tm'])
        y_a = _rwkv(z_rwkv, rwkv_mu[l], rwkv_w0[l], rwkv_w2[l], rwkv_a0[l], rwkv_a2[l], rwkv_g2[l],
                    rwkv_k_k[l], rwkv_k_a[l], rwkv_r_k[l].reshape(-1), rwkv_ln_w[l], rwkv_ln_b[l], tiles['tl'])
        y_b = from_tm(_s5(to_tm(z_s5), s5_lambda_re[l], s5_lambda_im[l], s5_b_re[l], s5_b_im[l], s5_c_re[l],
                          s5_c_im[l], s5_d[l], s5_log_dt[l], s5_glu_w[l], s5_glu_b[l], branch_norm_g[l, 0],
                          tiles['ts_s5']))
        q, k, v = _mla_prep(q_a, kv_a, kpe, pos_offset, mla_q_norm_g[l], mla_w_uq[l], mla_kv_norm_g[l],
                            mla_w_ukv[l], mla_q_head_g[l], mla_k_head_g[l], tiles['tm'])
        y_c = _attention(q, k, v, branch_norm_g[l, 1], tiles['tq'])
        y_d = from_tm(_lru(to_tm(z_lru), to_tm(z_gate), lru_conv_w[l], lru_conv_b[l], lru_w_a[l], lru_b_a[l],
                           lru_w_x[l], lru_b_x[l], lru_lambda[l], branch_norm_g[l, 2], tiles['ts_lru']))
        w_router = jnp.zeros((D, LANE), F32).at[:, :N_GROUPS].set(moe_w_group[l])
        w_router = w_router.at[:, N_GROUPS:N_GROUPS + N_EXPERTS].set(moe_w_expert[l])
        b_router = jnp.zeros((1, LANE), F32).at[0, :N_GROUPS].set(moe_b_group[l])
        b_router = b_router.at[0, N_GROUPS:N_GROUPS + N_EXPERTS].set(moe_b_expert[l])
        x, h2, logits = _outproj(x, y_a, y_b, y_c, y_d, w_out[l], gt1, sc2, sh2, norm2_g[l].reshape(1, D),
                                 w_router, b_router, tiles['tm'])
        blk_expert, n_used, row_tok, row_dst, row_w = _route(logits.reshape(T, LANE), T)
        y2 = _moe_experts(h2.reshape(T * SUBLANE, LANE), blk_expert, n_used, row_tok, row_dst, row_w,
                          moe_w1[l].astype(BF16), moe_w3[l].astype(BF16), moe_w2[l].astype(BF16))
        x = _combine(x, y2.reshape(TOP_K, -1, LANE), gt2, tiles['tm'])
    return x
```

```python
import functools
import math

import numpy as np
import jax
import jax.numpy as jnp
from jax import lax
from jax.experimental import pallas as pl
from jax.experimental.pallas import tpu as pltpu

F32 = jnp.float32
BF16 = jnp.bfloat16
HI = lax.Precision.HIGHEST

N_MIX = 4
RWKV_HEAD = 64
RWKV_W_RANK, RWKV_A_RANK, RWKV_G_RANK = 32, 32, 64
RWKV_LN_EPS = 64e-5
S5_GROUP, S5_STATE = 16, 64
MLA_HEADS, MLA_NOPE, MLA_ROPE, MLA_V = 4, 64, 32, 64
MLA_QK = MLA_NOPE + MLA_ROPE
ROPE_THETA = 10000.0
LRU_C = 8.0
CONV_WIDTH = 4
ATTN_CHUNK = 64
N_GROUPS, EXPERTS_PER_GROUP, TOP_K = 4, 8, 2
N_EXPERTS = N_GROUPS * EXPERTS_PER_GROUP
MOE_BLOCK = 256
NORM_EPS = 1e-6

LANE = 128
SUBLANE = 8
RWKV_CHUNK = 64
VMEM_LIMIT = 48 * 1024 * 1024


def _cp(sem, vmem=VMEM_LIMIT):
    return pltpu.CompilerParams(dimension_semantics=sem, vmem_limit_bytes=vmem)


def _dot(a, b):
    return jnp.dot(a, b, preferred_element_type=F32)


def _dot_hi(a, b):
    return jnp.dot(a, b, precision=HI, preferred_element_type=F32)


def _dot_nt(a, b):
    return lax.dot_general(a, b, (((1,), (1,)), ((), ())), preferred_element_type=F32)


def _dot_nt_hi(a, b):
    return lax.dot_general(a, b, (((1,), (1,)), ((), ())), precision=HI, preferred_element_type=F32)


def _softplus(x):
    return jnp.maximum(x, 0.0) + jnp.log1p(jnp.exp(-jnp.abs(x)))


def _rms(x, g, eps=NORM_EPS):
    return x * lax.rsqrt(jnp.mean(x * x, axis=-1, keepdims=True) + eps) * g


def _mod_kernel(c_ref, w_ref, b_ref, o_ref):
    c = c_ref[...]
    cond = c * jax.nn.sigmoid(c)
    o_ref[0] = _dot_hi(cond, w_ref[0]) + b_ref[0]


def _adaln_mod(c, ada_w, ada_b):
    L, D, D6 = ada_w.shape
    B = c.shape[0]
    nj = D6 // D
    return pl.pallas_call(
        _mod_kernel,
        grid=(L, nj),
        in_specs=[pl.BlockSpec((B, D), lambda l, j: (0, 0)),
                  pl.BlockSpec((1, D, D), lambda l, j: (l, 0, j)),
                  pl.BlockSpec((1, 1, D), lambda l, j: (l, 0, j))],
        out_specs=pl.BlockSpec((1, B, D), lambda l, j: (l, 0, j)),
        out_shape=jax.ShapeDtypeStruct((L, B, D6), F32),
        compiler_params=_cp(("parallel", "parallel")),
        name="adaln_mod",
    )(c, ada_w, ada_b.reshape(L, 1, D6))


_RWKV_IN = 3 * 256 + RWKV_W_RANK + RWKV_A_RANK + RWKV_G_RANK
_IN_COLS = (_RWKV_IN, 256, 256, 128, LANE, 256, 256)
_IN_OFFS = tuple(int(v) for v in np.cumsum((0,) + _IN_COLS))
_KPE_LANE0 = MLA_NOPE


def _pad_w_in(w_in):
    D = w_in.shape[0]
    o = np.cumsum((0, _RWKV_IN, 256, 256, 128, MLA_ROPE, 256, 256))
    pieces = [w_in[:, o[i]:o[i + 1]] for i in range(7)]
    kpe = jnp.zeros((D, LANE), w_in.dtype).at[:, _KPE_LANE0:_KPE_LANE0 + MLA_ROPE].set(pieces[4])
    pieces[4] = kpe
    return jnp.concatenate(pieces, axis=1).astype(BF16)


def _inproj_kernel(x_ref, sc_ref, sh_ref, g_ref, w_ref, *outs):
    x = x_ref[0]
    h = _rms(x, g_ref[...]) * (1.0 + sc_ref[0]) + sh_ref[0]
    z = _dot(h.astype(BF16), w_ref[...])
    for i, o_ref in enumerate(outs):
        o_ref[0] = z[:, _IN_OFFS[i]:_IN_OFFS[i + 1]]


def _inproj(x, sc, sh, g, w_pad, tm):
    B, S, D = x.shape
    row = lambda b, s: (b, s, 0)
    vec = lambda b, s: (b, 0, 0)
    return pl.pallas_call(
        _inproj_kernel,
        grid=(B, S // tm),
        in_specs=[pl.BlockSpec((1, tm, D), row),
                  pl.BlockSpec((1, 1, D), vec),
                  pl.BlockSpec((1, 1, D), vec),
                  pl.BlockSpec((1, D), lambda b, s: (0, 0)),
                  pl.BlockSpec(w_pad.shape, lambda b, s: (0, 0))],
        out_specs=[pl.BlockSpec((1, tm, n), row) for n in _IN_COLS],
        out_shape=[jax.ShapeDtypeStruct((B, S, n), F32) for n in _IN_COLS],
        compiler_params=_cp(("parallel", "parallel")),
        name="inproj",
    )(x, sc, sh, g, w_pad)


def _split_bf16(x, parts):
    out = []
    for _ in range(parts - 1):
        hi = x.astype(BF16)
        out.append(hi)
        x = x - hi.astype(F32)
    out.append(x.astype(BF16))
    return out


def _rwkv_kernel(z_ref, mu_ref, w0_ref, a0_ref, wl_ref, kk_ref, ka_ref, rk_ref, lnw_ref, lnb_ref,
                 hsum_ref, tri_ref, o_ref, zprev_ref, g_ref, y_ref):
    W = 256
    NH = W // RWKV_HEAD
    L = RWKV_CHUNK
    TL = z_ref.shape[1]

    @pl.when(pl.program_id(1) == 0)
    def _():
        zprev_ref[...] = jnp.zeros_like(zprev_ref)
        g_ref[...] = jnp.zeros_like(g_ref)

    hsum = hsum_ref[...]

    def head_sum(t):
        hi, lo_ = _split_bf16(t, 2)
        return _dot(hi, hsum) + _dot(lo_, hsum)

    z = z_ref[0]
    rows = lax.broadcasted_iota(jnp.int32, z.shape, 0)
    zp = jnp.where(rows == 0, zprev_ref[...], pltpu.roll(z, 1, axis=0))
    zprev_ref[...] = z[TL - 1:TL, :]
    zs = z + (zp - z) * mu_ref[...]
    r, k, v, lo = zs[:, 0:W], zs[:, W:2 * W], zs[:, 2 * W:3 * W], zs[:, 3 * W:3 * W + LANE]
    lane = lax.broadcasted_iota(jnp.int32, lo.shape, 1)
    act = jnp.where(lane < RWKV_W_RANK, jnp.tanh(lo),
                    jnp.where(lane < RWKV_W_RANK + RWKV_A_RANK, lo, jax.nn.sigmoid(lo)))
    up = _dot(act.astype(BF16), wl_ref[...])
    w = -_softplus(-(w0_ref[...] + up[:, 0:W])) - 0.5
    ld = -jnp.exp(w)
    a = jax.nn.sigmoid(a0_ref[...] + up[:, W:2 * W])
    g = up[:, 2 * W:3 * W]
    kk = k * kk_ref[...]
    kk = kk / jnp.maximum(jnp.sqrt(head_sum(kk * kk)), 1e-12)
    k2 = k * (1.0 + (a - 1.0) * ka_ref[...])
    kka = kk * a

    tri_b = tri_ref[0]
    tri_incl = tri_b > 0
    tri_strict = tri_ref[1] > 0
    eye = (lax.broadcasted_iota(jnp.int32, (RWKV_HEAD, RWKV_HEAD), 0)
           == lax.broadcasted_iota(jnp.int32, (RWKV_HEAD, RWKV_HEAD), 1)).astype(BF16)

    NC = TL // L
    pairs = [(j, h) for j in range(NC) for h in range(NH)]
    hsl = [slice(h * RWKV_HEAD, (h + 1) * RWKV_HEAD) for h in range(NH)]
    csl = [slice(j * L, (j + 1) * L) for j in range(NC)]
    bt, rt, a_s, k_s, a_e, k_e, v_c, p_l = [], [], [], [], [], [], [], []
    for j in range(NC):
        ld_c = ld[csl[j]]
        cs = sum(_dot(tri_b, p) for p in _split_bf16(ld_c, 3))
        cs_l = cs[L - 1:L, :]
        p_inv = jnp.exp(-cs)
        p_end = jnp.exp(cs_l - cs)
        p_l.append(jnp.exp(cs_l))
        bt.append(-kk[csl[j]] * jnp.exp(cs - ld_c))
        rt.append(r[csl[j]] * jnp.exp(cs))
        a_s.append((kka[csl[j]] * p_inv).astype(BF16))
        k_s.append((k2[csl[j]] * p_inv).astype(BF16))
        a_e.append((kka[csl[j]] * p_end).astype(BF16))
        k_e.append((k2[csl[j]] * p_end).astype(BF16))
        v_c.append(v[csl[j]].astype(BF16))

    x = [jnp.concatenate([bt[j][:, hsl[h]], rt[j][:, hsl[h]]], axis=0).astype(BF16) for j, h in pairs]
    ga = [_dot_nt(x[i], a_s[j][:, hsl[h]]) for i, (j, h) in enumerate(pairs)]
    gk = [_dot_nt(x[i], k_s[j][:, hsl[h]]) for i, (j, h) in enumerate(pairs)]
    m_ba = [jnp.where(tri_strict, t[:L], 0.0).astype(BF16) for t in ga]
    m_ra = [jnp.where(tri_incl, t[L:], 0.0).astype(BF16) for t in ga]
    m_bk = [jnp.where(tri_strict, t[:L], 0.0).astype(BF16) for t in gk]
    m_rk = [jnp.where(tri_incl, t[L:], 0.0).astype(BF16) for t in gk]
    v_h = [v_c[j][:, hsl[h]] for j, h in pairs]
    wn = [jnp.concatenate([bt[j][:, hsl[h]], _dot(m_bk[i], v_h[i])], axis=1) for i, (j, h) in enumerate(pairs)]
    n_pow = m_ba
    wn = [w_ + _dot(n_, w_.astype(BF16)) for w_, n_ in zip(wn, n_pow)]
    for _ in range(5):
        n_pow = [_dot(n_, n_).astype(BF16) for n_ in n_pow]
        wn = [w_ + _dot(n_, w_.astype(BF16)) for w_, n_ in zip(wn, n_pow)]
    wn_b = [w_.astype(BF16) for w_ in wn]
    mw = [_dot(m_, w_) for m_, w_ in zip(m_ra, wn_b)]
    r2 = [(rt[j][:, hsl[h]] + mw[i][:, :RWKV_HEAD]).astype(BF16) for i, (j, h) in enumerate(pairs)]
    y0 = [mw[i][:, RWKV_HEAD:] + _dot(m_rk[i], v_h[i]) for i in range(len(pairs))]
    b2_t = [_dot_nt(eye, w_[:, :RWKV_HEAD]).astype(BF16) for w_ in wn_b]
    c_m = [_dot(b2_t[i], a_e[j][:, hsl[h]]).astype(BF16) for i, (j, h) in enumerate(pairs)]
    uv_t = [_dot_nt(eye, jnp.concatenate([wn_b[i][:, RWKV_HEAD:], v_h[i]], axis=0)).astype(BF16)
            for i in range(len(pairs))]
    d_t = [_dot(uv_t[i], jnp.concatenate([a_e[j][:, hsl[h]], k_e[j][:, hsl[h]]], axis=0))
           for i, (j, h) in enumerate(pairs)]
    for i, (j, h) in enumerate(pairs):
        g0 = g_ref[h]
        g_hi, g_lo = _split_bf16(g0, 2)
        y_ref[csl[j], hsl[h]] = _dot_nt(r2[i], g_hi) + _dot_nt(r2[i], g_lo) + y0[i]
        g_ref[h] = g0 * p_l[j][:, hsl[h]] + _dot(g_hi, c_m[i]) + _dot(g_lo, c_m[i]) + d_t[i]

    y = y_ref[...]
    inv_n = 1.0 / RWKV_HEAD
    mean = head_sum(y) * inv_n
    yc = y - mean
    var = head_sum(yc * yc) * inv_n
    yn = yc * lax.rsqrt(var + RWKV_LN_EPS) * lnw_ref[...] + lnb_ref[...]
    bonus = head_sum(r * k2 * rk_ref[...]) * v
    o_ref[0] = ((yn + bonus) * g).astype(o_ref.dtype)


def _rwkv(z, mu, w0, w2, a0, a2, g2, k_k, k_a, r_k, ln_w, ln_b, tl):
    B, S, _ = z.shape
    W = 256
    L = RWKV_CHUNK
    wl = jnp.zeros((LANE, 3 * W), F32)
    wl = wl.at[0:RWKV_W_RANK, 0:W].set(w2)
    wl = wl.at[RWKV_W_RANK:RWKV_W_RANK + RWKV_A_RANK, W:2 * W].set(a2)
    wl = wl.at[RWKV_W_RANK + RWKV_A_RANK:LANE, 2 * W:3 * W].set(g2).astype(BF16)
    hid = np.arange(W) // RWKV_HEAD
    hsum = jnp.asarray((hid[:, None] == hid[None, :]).astype(np.float32), dtype=BF16)
    t = np.arange(L)
    tri = jnp.asarray(np.stack([(t[None, :] <= t[:, None]), (t[None, :] < t[:, None])]).astype(np.float32), dtype=BF16)
    row2 = lambda v_: v_.reshape(1, -1)
    const = lambda shape: pl.BlockSpec(shape, lambda b, s: (0,) * len(shape))
    return pl.pallas_call(
        _rwkv_kernel,
        grid=(B, S // tl),
        in_specs=[pl.BlockSpec((1, tl, _RWKV_IN), lambda b, s: (b, s, 0)),
                  const((1, _RWKV_IN)), const((1, W)), const((1, W)), const((LANE, 3 * W)),
                  const((1, W)), const((1, W)), const((1, W)), const((1, W)), const((1, W)),
                  const((W, W)), const((2, L, L))],
        out_specs=pl.BlockSpec((1, tl, W), lambda b, s: (b, s, 0)),
        out_shape=jax.ShapeDtypeStruct((B, S, W), BF16),
        scratch_shapes=[pltpu.VMEM((1, _RWKV_IN), F32),
                        pltpu.VMEM((W // RWKV_HEAD, RWKV_HEAD, RWKV_HEAD), F32),
                        pltpu.VMEM((tl, W), F32)],
        compiler_params=_cp(("parallel", "arbitrary")),
        name="rwkv7",
    )(z, row2(mu), row2(w0), row2(a0), wl, row2(k_k), row2(k_a), row2(r_k), row2(ln_w), row2(ln_b), hsum, tri)


def _s5_kernel(u_ref, wb_ref, are_ref, aim_ref, wc_ref, d_ref, gw_ref, gb_ref, ng_ref, o_ref,
               hre_ref, him_ref, sre_ref, sim_ref):
    nb = SUBLANE
    rows = u_ref.shape[0]
    P = are_ref.shape[1]

    @pl.when(pl.program_id(0) == 0)
    def _():
        sre_ref[...] = jnp.zeros_like(sre_ref)
        sim_ref[...] = jnp.zeros_like(sim_ref)

    u = u_ref[...]
    bu = _dot(u.astype(BF16), wb_ref[...])
    hre_ref[...] = bu[:, :P]
    him_ref[...] = bu[:, P:]
    a_re = jnp.broadcast_to(are_ref[...], (nb, P))
    a_im = jnp.broadcast_to(aim_ref[...], (nb, P))

    def step(t, carry):
        h_re, h_im = carry
        i = pl.multiple_of(t * nb, nb)
        n_re = a_re * h_re - a_im * h_im + hre_ref[pl.ds(i, nb), :]
        n_im = a_re * h_im + a_im * h_re + him_ref[pl.ds(i, nb), :]
        hre_ref[pl.ds(i, nb), :] = n_re
        him_ref[pl.ds(i, nb), :] = n_im
        return n_re, n_im

    h_re, h_im = lax.fori_loop(0, rows // nb, step, (sre_ref[...], sim_ref[...]))
    sre_ref[...] = h_re
    sim_ref[...] = h_im
    y = _dot(hre_ref[...].astype(BF16), wc_ref[0]) + _dot(him_ref[...].astype(BF16), wc_ref[1])
    y = jax.nn.gelu(y + d_ref[...] * u)
    y = y * jax.nn.sigmoid(_dot(y.astype(BF16), gw_ref[...]) + gb_ref[...])
    o_ref[...] = _rms(y, ng_ref[...]).astype(o_ref.dtype)


def _s5(u_tm, lam_re, lam_im, b_re, b_im, c_re, c_im, d, log_dt, glu_w, glu_b, norm_g, ts):
    rows_total, W = u_tm.shape
    G, P = lam_re.shape
    dt = jnp.exp(log_dt)[:, None]
    mag = jnp.exp(lam_re * dt)
    a_re = mag * jnp.cos(lam_im * dt)
    a_im = mag * jnp.sin(lam_im * dt)
    den = lam_re * lam_re + lam_im * lam_im
    q_re = ((a_re - 1.0) * lam_re + a_im * lam_im) / den
    q_im = (a_im * lam_re - (a_re - 1.0) * lam_im) / den
    bb_re = q_re[..., None] * b_re - q_im[..., None] * b_im
    bb_im = q_re[..., None] * b_im + q_im[..., None] * b_re
    eye = jnp.eye(G, dtype=F32)
    wb_re = jnp.einsum('gpi,gh->gihp', bb_re, eye).reshape(W, G * P)
    wb_im = jnp.einsum('gpi,gh->gihp', bb_im, eye).reshape(W, G * P)
    wb = jnp.concatenate([wb_re, wb_im], axis=1).astype(BF16)
    wc_re = jnp.einsum('gip,gh->gphi', c_re, eye).reshape(G * P, W)
    wc_im = jnp.einsum('gip,gh->gphi', -c_im, eye).reshape(G * P, W)
    wc = jnp.stack([wc_re, wc_im]).astype(BF16)
    GP = G * P
    rows = ts * SUBLANE
    const = lambda shape: pl.BlockSpec(shape, lambda s: (0,) * len(shape))
    return pl.pallas_call(
        _s5_kernel,
        grid=(rows_total // rows,),
        in_specs=[pl.BlockSpec((rows, W), lambda s: (s, 0)),
                  const((W, 2 * GP)), const((1, GP)), const((1, GP)), const((2, GP, W)),
                  const((1, W)), const((W, W)), const((1, W)), const((1, W))],
        out_specs=pl.BlockSpec((rows, W), lambda s: (s, 0)),
        out_shape=jax.ShapeDtypeStruct((rows_total, W), BF16),
        scratch_shapes=[pltpu.VMEM((rows, GP), F32), pltpu.VMEM((rows, GP), F32),
                        pltpu.VMEM((SUBLANE, GP), F32), pltpu.VMEM((SUBLANE, GP), F32)],
        compiler_params=_cp(("arbitrary",)),
        name="s5",
    )(u_tm, wb, a_re.reshape(1, GP), a_im.reshape(1, GP), wc, d.reshape(1, W), glu_w.astype(BF16),
      glu_b.reshape(1, W), norm_g.reshape(1, W))


def _lru_kernel(x_ref, gate_ref, cw_ref, cb_ref, wax_ref, bax_ref, lam_ref, ng_ref, o_ref,
                xc_ref, a_ref, b_ref, hs_ref):
    nb = SUBLANE
    rows, W = x_ref.shape
    halo = (CONV_WIDTH - 1) * nb

    @pl.when(pl.program_id(0) == 0)
    def _():
        xc_ref[...] = jnp.zeros_like(xc_ref)
        hs_ref[...] = jnp.zeros_like(hs_ref)

    x = x_ref[...]
    xf = jnp.concatenate([xc_ref[...], x], axis=0)
    xc_ref[...] = x[rows - halo:, :]
    xc = cb_ref[...]
    for j in range(CONV_WIDTH):
        xc = xc + cw_ref[j:j + 1, :] * xf[j * nb:j * nb + rows, :]
    gates = _dot(xc.astype(BF16), wax_ref[...]) + bax_ref[...]
    r = jax.nn.sigmoid(gates[:, :W])
    i = jax.nn.sigmoid(gates[:, W:])
    log_a = -LRU_C * r * _softplus(-lam_ref[...])
    a = jnp.exp(log_a)
    a_ref[...] = a
    b_ref[...] = jnp.sqrt(1.0 - jnp.exp(2.0 * log_a)) * (i * xc)

    def step(t, h):
        i0 = pl.multiple_of(t * nb, nb)
        h = a_ref[pl.ds(i0, nb), :] * h + b_ref[pl.ds(i0, nb), :]
        b_ref[pl.ds(i0, nb), :] = h
        return h

    hs_ref[...] = lax.fori_loop(0, rows // nb, step, hs_ref[...])
    y = b_ref[...] * jax.nn.gelu(gate_ref[...])
    o_ref[...] = _rms(y, ng_ref[...]).astype(o_ref.dtype)


def _lru(x_tm, gate_tm, conv_w, conv_b, w_a, b_a, w_x, b_x, lam, norm_g, ts):
    rows_total, W = x_tm.shape
    nblk, bs, _ = w_a.shape
    eye = jnp.eye(nblk, dtype=F32)
    bd = lambda w_: jnp.einsum('nkj,nm->nkmj', w_, eye).reshape(W, W)
    wax = jnp.concatenate([bd(w_a), bd(w_x)], axis=1).astype(BF16)
    bax = jnp.concatenate([b_a, b_x]).reshape(1, 2 * W)
    rows = ts * SUBLANE
    const = lambda shape: pl.BlockSpec(shape, lambda s: (0,) * len(shape))
    return pl.pallas_call(
        _lru_kernel,
        grid=(rows_total // rows,),
        in_specs=[pl.BlockSpec((rows, W), lambda s: (s, 0)), pl.BlockSpec((rows, W), lambda s: (s, 0)),
                  const((CONV_WIDTH, W)), const((1, W)), const((W, 2 * W)), const((1, 2 * W)),
                  const((1, W)), const((1, W))],
        out_specs=pl.BlockSpec((rows, W), lambda s: (s, 0)),
        out_shape=jax.ShapeDtypeStruct((rows_total, W), BF16),
        scratch_shapes=[pltpu.VMEM(((CONV_WIDTH - 1) * SUBLANE, W), F32), pltpu.VMEM((rows, W), F32),
                        pltpu.VMEM((rows, W), F32), pltpu.VMEM((SUBLANE, W), F32)],
        compiler_params=_cp(("arbitrary",)),
        name="rglru",
    )(x_tm, gate_tm, conv_w, conv_b.reshape(1, W), wax, bax, lam.reshape(1, W), norm_g.reshape(1, W))


def _mla_prep_kernel(pos_ref, qa_ref, kva_ref, kpe_ref, gq_ref, gkv_ref, wq_ref, wk_ref, wv_ref,
                     gqh_ref, gkh_ref, freq_ref, q_ref, k_ref, v_ref):
    tm = qa_ref.shape[1]
    b = pl.program_id(0)
    s0 = pl.program_id(1) * tm
    pos = (pos_ref[b] + s0 + lax.broadcasted_iota(jnp.int32, (tm, LANE), 0)).astype(F32)
    ang = pos * freq_ref[...]
    cos_t = jnp.cos(ang)
    sin_t = jnp.sin(ang)
    lane = lax.broadcasted_iota(jnp.int32, (tm, LANE), 1)
    half = MLA_ROPE // 2
    first = (lane >= MLA_NOPE) & (lane < MLA_NOPE + half)
    second = (lane >= MLA_NOPE + half) & (lane < MLA_QK)

    def rope(t):
        rot = jnp.where(first, -pltpu.roll(t, LANE - half, axis=1),
                        jnp.where(second, pltpu.roll(t, half, axis=1), 0.0))
        return t * cos_t + rot * sin_t

    def head_norm(t, g):
        ms = jnp.sum(t * t, axis=-1, keepdims=True) * (1.0 / MLA_QK)
        return t * lax.rsqrt(ms + NORM_EPS) * g

    qn = _rms(qa_ref[0], gq_ref[...]).astype(BF16)
    kvn = _rms(kva_ref[0], gkv_ref[...]).astype(BF16)
    q_all = _dot(qn, wq_ref[...])
    k_all = _dot(kvn, wk_ref[...])
    v_all = _dot(kvn, wv_ref[...])
    kpe = kpe_ref[0]
    scale = MLA_QK ** -0.5
    for h in range(MLA_HEADS):
        hs = slice(h * LANE, (h + 1) * LANE)
        q_ref[0, h] = (rope(head_norm(q_all[:, hs], gqh_ref[...])) * scale).astype(q_ref.dtype)
        k_ref[0, h] = rope(head_norm(k_all[:, hs] + kpe, gkh_ref[...])).astype(k_ref.dtype)
        v_ref[0, h] = v_all[:, hs].T[:MLA_V].astype(v_ref.dtype)


def _mla_prep(q_a, kv_a, kpe, pos_offset, q_norm_g, w_uq, kv_norm_g, w_ukv, q_head_g, k_head_g, tm):
    B, S, QR = q_a.shape
    KVR = kv_a.shape[2]
    H = MLA_HEADS
    wq = jnp.zeros((QR, H, LANE), F32).at[:, :, :MLA_QK].set(w_uq.reshape(QR, H, MLA_QK)).reshape(QR, H * LANE)
    wkv = w_ukv.reshape(KVR, H, MLA_NOPE + MLA_V)
    wk = jnp.zeros((KVR, H, LANE), F32).at[:, :, :MLA_NOPE].set(wkv[:, :, :MLA_NOPE]).reshape(KVR, H * LANE)
    wv = jnp.zeros((KVR, H, LANE), F32).at[:, :, :MLA_V].set(wkv[:, :, MLA_NOPE:]).reshape(KVR, H * LANE)
    pad_g = lambda g: jnp.zeros((1, LANE), F32).at[0, :MLA_QK].set(g)
    half = MLA_ROPE // 2
    inv_freq = np.power(np.float32(ROPE_THETA), -np.arange(half, dtype=np.float32) * np.float32(2.0) / np.float32(MLA_ROPE))
    freq = np.zeros((1, LANE), np.float32)
    freq[0, MLA_NOPE:MLA_NOPE + half] = inv_freq
    freq[0, MLA_NOPE + half:MLA_QK] = inv_freq
    const = lambda shape: pl.BlockSpec(shape, lambda b, s, p: (0,) * len(shape))
    row = lambda n: pl.BlockSpec((1, tm, n), lambda b, s, p: (b, s, 0))
    hout = pl.BlockSpec((1, H, tm, LANE), lambda b, s, p: (b, 0, s, 0))
    grid_spec = pltpu.PrefetchScalarGridSpec(
        num_scalar_prefetch=1, grid=(B, S // tm),
        in_specs=[row(QR), row(KVR), row(LANE), const((1, QR)), const((1, KVR)),
                  const((QR, H * LANE)), const((KVR, H * LANE)), const((KVR, H * LANE)),
                  const((1, LANE)), const((1, LANE)), const((1, LANE))],
        out_specs=[hout, hout, pl.BlockSpec((1, H, MLA_V, tm), lambda b, s, p: (b, 0, 0, s))])
    return pl.pallas_call(
        _mla_prep_kernel,
        grid_spec=grid_spec,
        out_shape=[jax.ShapeDtypeStruct((B, H, S, LANE), BF16)] * 2 + [jax.ShapeDtypeStruct((B, H, MLA_V, S), BF16)],
        compiler_params=_cp(("parallel", "parallel")),
        name="mla_prep",
    )(pos_offset, q_a, kv_a, kpe, q_norm_g.reshape(1, QR), kv_norm_g.reshape(1, KVR),
      wq.astype(BF16), wk.astype(BF16), wv.astype(BF16), pad_g(q_head_g), pad_g(k_head_g), jnp.asarray(freq))


def _attn_kernel(q_ref, k_ref, vt_ref, ng_ref, o_ref, m_ref, l_ref, acc_ref):
    H = q_ref.shape[1]
    tq = q_ref.shape[2]
    qi = pl.program_id(1)
    ki = pl.program_id(2)

    @pl.when(ki == 0)
    def _():
        m_ref[...] = jnp.full_like(m_ref, -1e30)
        l_ref[...] = jnp.zeros_like(l_ref)
        acc_ref[...] = jnp.zeros_like(acc_ref)

    def update(masked):
        if masked:
            kc = lax.broadcasted_iota(jnp.int32, (tq, tq), 0) // ATTN_CHUNK
            qc = lax.broadcasted_iota(jnp.int32, (tq, tq), 1) // ATTN_CHUNK
            visible = kc <= qc
        for h in range(H):
            st = _dot_nt(k_ref[0, h], q_ref[0, h])
            if masked:
                st = jnp.where(visible, st, -1e30)
            m_old = m_ref[h]
            m_new = jnp.maximum(m_old, jnp.max(st, axis=0, keepdims=True))
            alpha = jnp.exp(m_old - m_new)
            p = jnp.exp(st - m_new)
            l_ref[h] = alpha * l_ref[h] + jnp.sum(p, axis=0, keepdims=True)
            acc_ref[h] = alpha * acc_ref[h] + _dot(vt_ref[0, h], p.astype(BF16))
            m_ref[h] = m_new

    @pl.when(ki < qi)
    def _():
        update(False)

    @pl.when(ki == qi)
    def _():
        update(True)
        ot = jnp.concatenate([acc_ref[h] / l_ref[h] for h in range(H)], axis=0)
        o_ref[0] = _rms(ot.T, ng_ref[...]).astype(o_ref.dtype)


def _attention(q, k, vt, norm_g, tq):
    B, H, S, _ = q.shape
    W = H * MLA_V
    nq = S // tq
    qspec = pl.BlockSpec((1, H, tq, LANE), lambda b, i, j: (b, 0, i, 0))
    kspec = pl.BlockSpec((1, H, tq, LANE), lambda b, i, j: (b, 0, jnp.minimum(i, j), 0))
    vspec = pl.BlockSpec((1, H, MLA_V, tq), lambda b, i, j: (b, 0, 0, jnp.minimum(i, j)))
    return pl.pallas_call(
        _attn_kernel,
        grid=(B, nq, nq),
        in_specs=[qspec, kspec, vspec, pl.BlockSpec((1, W), lambda b, i, j: (0, 0))],
        out_specs=pl.BlockSpec((1, tq, W), lambda b, i, j: (b, i, 0)),
        out_shape=jax.ShapeDtypeStruct((B, S, W), BF16),
        scratch_shapes=[pltpu.VMEM((H, 1, tq), F32), pltpu.VMEM((H, 1, tq), F32),
                        pltpu.VMEM((H, MLA_V, tq), F32)],
        compiler_params=_cp(("parallel", "parallel", "arbitrary")),
        name="mla_attn",
    )(q, k, vt, norm_g.reshape(1, W))


def _store_token_tiles(ref, val):
    n, d = val.shape
    for s in range(d // LANE):
        ref[pl.ds(s, n, stride=SUBLANE), :] = val[:, s * LANE:(s + 1) * LANE]


def _load_token_tiles(ref, n):
    return jnp.concatenate([ref[pl.ds(s, n, stride=SUBLANE), :] for s in range(SUBLANE)], axis=1)


def _outproj_kernel(x_ref, ya_ref, yb_ref, yc_ref, yd_ref, wo_ref, gt_ref, sc_ref, sh_ref, g_ref,
                    wr_ref, br_ref, xo_ref, h_ref, lg_ref):
    acc = _dot(ya_ref[0], wo_ref[0])
    acc = acc + _dot(yb_ref[0], wo_ref[1])
    acc = acc + _dot(yc_ref[0], wo_ref[2])
    acc = acc + _dot(yd_ref[0], wo_ref[3])
    x = x_ref[0] + gt_ref[0] * acc
    xo_ref[0] = x
    h = _rms(x, g_ref[...]) * (1.0 + sc_ref[0]) + sh_ref[0]
    _store_token_tiles(h_ref.at[0], h)
    h_hi, h_lo = _split_bf16(h, 2)
    lg_ref[0] = _dot(h_hi, wr_ref[0]) + (_dot(h_lo, wr_ref[0]) + _dot(h_hi, wr_ref[1])) + br_ref[...]


def _outproj(x, ya, yb, yc, yd, w_out, gt, sc, sh, g, w_router, b_router, tm):
    B, S, D = x.shape
    W = D // N_MIX
    row = lambda n: pl.BlockSpec((1, tm, n), lambda b, s: (b, s, 0))
    vec = pl.BlockSpec((1, 1, D), lambda b, s: (b, 0, 0))
    const = lambda shape: pl.BlockSpec(shape, lambda b, s: (0,) * len(shape))
    return pl.pallas_call(
        _outproj_kernel,
        grid=(B, S // tm),
        in_specs=[row(D), row(W), row(W), row(W), row(W), const((N_MIX, W, D)), vec, vec, vec,
                  const((1, D)), const((2, D, LANE)), const((1, LANE))],
        out_specs=[row(D), pl.BlockSpec((1, tm * SUBLANE, LANE), lambda b, s: (b, s, 0)), row(LANE)],
        out_shape=[jax.ShapeDtypeStruct((B, S, D), F32), jax.ShapeDtypeStruct((B, S * SUBLANE, LANE), F32),
                   jax.ShapeDtypeStruct((B, S, LANE), F32)],
        compiler_params=_cp(("parallel", "parallel")),
        name="outproj",
    )(x, ya, yb, yc, yd, w_out.reshape(N_MIX, W, D).astype(BF16), gt, sc, sh, g,
      jnp.stack(_split_bf16(w_router, 2)), b_router)


_MOE_PAD_ROWS = MOE_BLOCK // TOP_K


def _moe_kernel(be_ref, nused_ref, tok_ref, tokn_ref, dst_ref, w_ref, h_hbm, w1_ref, w3_ref, w2_ref, o_hbm,
                xbuf, ybuf, gsem, ssem):
    i = pl.program_id(0)
    R = MOE_BLOCK
    TR = SUBLANE
    nused = nused_ref[0]
    slot = lax.rem(i, 2)
    other = 1 - slot

    def gather_rows(idx_ref, s):
        def body(r, c):
            src = pl.multiple_of(idx_ref[0, 0, r], TR)
            pltpu.make_async_copy(h_hbm.at[pl.ds(src, TR)], xbuf.at[s, pl.ds(pl.multiple_of(r * TR, TR), TR)],
                                  gsem.at[s]).start()
            return c
        lax.fori_loop(0, R, body, 0, unroll=8)

    def scatter_rows(s):
        def body(r, c):
            dst = pl.multiple_of(dst_ref[0, 0, r], TR)
            pltpu.make_async_copy(ybuf.at[s, pl.ds(pl.multiple_of(r * TR, TR), TR)], o_hbm.at[pl.ds(dst, TR)],
                                  ssem.at[s]).start(priority=1)
            return c
        lax.fori_loop(0, R, body, 0, unroll=8)

    def wait_gather(s):
        pltpu.make_async_copy(h_hbm.at[pl.ds(0, R * TR)], xbuf.at[s], gsem.at[s]).wait()

    def wait_scatter(s):
        pltpu.make_async_copy(ybuf.at[s], o_hbm.at[pl.ds(0, R * TR)], ssem.at[s]).wait()

    @pl.when(i == 0)
    def _():
        plane = o_hbm.shape[0] // TOP_K
        npad = _MOE_PAD_ROWS * TR
        ybuf[1] = jnp.zeros(ybuf.shape[1:], ybuf.dtype)
        for j in range(TOP_K):
            pad = pltpu.make_async_copy(ybuf.at[1, pl.ds(0, npad)],
                                        o_hbm.at[pl.ds((j + 1) * plane - npad, npad)], ssem.at[1])
            pad.start()
            pad.wait()
        gather_rows(tok_ref, 0)

    @pl.when(i < nused)
    def _():
        @pl.when(i + 1 < nused)
        def _():
            gather_rows(tokn_ref, other)

        wait_gather(slot)

        @pl.when(i >= 2)
        def _():
            wait_scatter(slot)

        xb = _load_token_tiles(xbuf.at[slot], R).astype(BF16)
        hid = _dot(xb, w1_ref[0])
        hid = hid * jax.nn.sigmoid(hid) * _dot(xb, w3_ref[0])
        y = _dot(hid.astype(BF16), w2_ref[0])
        eye_r = (lax.broadcasted_iota(jnp.int32, (R, R), 0) == lax.broadcasted_iota(jnp.int32, (R, R), 1)).astype(F32)
        wcol = _dot_nt_hi(eye_r, jnp.broadcast_to(w_ref[0], (LANE, R)))
        _store_token_tiles(ybuf.at[slot], y * wcol[:, 0:1])
        scatter_rows(slot)

        @pl.when(i == nused - 1)
        def _():
            wait_scatter(slot)

            @pl.when(i >= 1)
            def _():
                wait_scatter(other)


def _moe_experts(h2, blk_expert, n_used, row_tok, row_dst, row_w, w1, w3, w2):
    E, D, F = w1.shape
    T = h2.shape[0] // SUBLANE
    nb = blk_expert.shape[0]
    R = MOE_BLOCK
    blk = lambda i, be, nu: (i, 0, 0)
    nxt = lambda i, be, nu: (jnp.minimum(i + 1, nb - 1), 0, 0)
    wsel = lambda i, be, nu: (be[i], 0, 0)
    plane = (T + _MOE_PAD_ROWS) * SUBLANE
    row_tok = (row_tok * SUBLANE).reshape(nb, 1, R)
    row_dst = ((row_dst % TOP_K) * plane + (row_dst // TOP_K) * SUBLANE).reshape(nb, 1, R)
    grid_spec = pltpu.PrefetchScalarGridSpec(
        num_scalar_prefetch=2, grid=(nb,),
        in_specs=[pl.BlockSpec((1, 1, R), blk, memory_space=pltpu.SMEM),
                  pl.BlockSpec((1, 1, R), nxt, memory_space=pltpu.SMEM),
                  pl.BlockSpec((1, 1, R), blk, memory_space=pltpu.SMEM),
                  pl.BlockSpec((1, 1, R), blk),
                  pl.BlockSpec(memory_space=pl.ANY),
                  pl.BlockSpec((1, D, F), wsel), pl.BlockSpec((1, D, F), wsel), pl.BlockSpec((1, F, D), wsel)],
        out_specs=pl.BlockSpec(memory_space=pl.ANY),
        scratch_shapes=[pltpu.VMEM((2, R * SUBLANE, LANE), F32), pltpu.VMEM((2, R * SUBLANE, LANE), F32),
                        pltpu.SemaphoreType.DMA((2,)), pltpu.SemaphoreType.DMA((2,))])
    return pl.pallas_call(
        _moe_kernel,
        grid_spec=grid_spec,
        out_shape=jax.ShapeDtypeStruct((TOP_K * plane, LANE), F32),
        compiler_params=_cp(("arbitrary",)),
        name="moe_experts",
    )(blk_expert, n_used, row_tok, row_tok, row_dst, row_w.reshape(nb, 1, R), h2, w1, w3, w2)


def _route(logits, T):
    p_group = jax.nn.softmax(logits[:, :N_GROUPS], axis=-1)
    g_sel = jnp.argmax(p_group, axis=-1).astype(jnp.int32)
    g_prob = jnp.max(p_group, axis=-1)
    logit_e = logits[:, N_GROUPS:N_GROUPS + N_EXPERTS].reshape(T, N_GROUPS, EXPERTS_PER_GROUP)
    logit_e = jnp.take_along_axis(logit_e, g_sel[:, None, None], axis=1)[:, 0]
    p_e = jax.nn.softmax(logit_e, axis=-1)
    top_p, top_i = lax.top_k(p_e, TOP_K)
    gate = (g_prob[:, None] * top_p / jnp.sum(top_p, axis=-1, keepdims=True)).reshape(-1)
    expert = (g_sel[:, None] * EXPERTS_PER_GROUP + top_i.astype(jnp.int32)).reshape(-1)
    n_assign = T * TOP_K
    n_blocks = -(-n_assign // MOE_BLOCK) + N_EXPERTS
    order = jnp.argsort(expert).astype(jnp.int32)
    counts = jnp.sum((expert[:, None] == jnp.arange(N_EXPERTS, dtype=jnp.int32)[None, :]).astype(jnp.int32), axis=0)
    start = jnp.cumsum(counts) - counts
    pcounts = (counts + MOE_BLOCK - 1) // MOE_BLOCK * MOE_BLOCK
    pend = jnp.cumsum(pcounts)
    pstart = pend - pcounts
    blk_start = jnp.arange(n_blocks, dtype=jnp.int32) * MOE_BLOCK
    blk_expert = jnp.minimum(jnp.sum((pend[None, :] <= blk_start[:, None]).astype(jnp.int32), axis=1), N_EXPERTS - 1)
    n_used = (pend[-1] // MOE_BLOCK).astype(jnp.int32).reshape(1)
    lane = jnp.arange(MOE_BLOCK, dtype=jnp.int32)[None, :]
    off = blk_start[:, None] + lane - pstart[blk_expert][:, None]
    valid = off < counts[blk_expert][:, None]
    pos = jnp.clip(start[blk_expert][:, None] + off, 0, n_assign - 1)
    asg = order[pos]
    row_tok = jnp.where(valid, asg // TOP_K, 0)
    row_dst = jnp.where(valid, asg, n_assign + lane)
    row_w = jnp.where(valid, gate[asg], 0.0)
    return blk_expert.astype(jnp.int32), n_used, row_tok, row_dst, row_w


def _combine_kernel(x_ref, y0_ref, y1_ref, gt_ref, o_ref):
    tm = x_ref.shape[1]
    y = _load_token_tiles(y0_ref.at[0], tm) + _load_token_tiles(y1_ref.at[0], tm)
    o_ref[0] = x_ref[0] + gt_ref[0] * y


def _combine(x, y2, gt, tm):
    B, S, D = x.shape
    nt = S // tm
    return pl.pallas_call(
        _combine_kernel,
        grid=(B, nt),
        in_specs=[pl.BlockSpec((1, tm, D), lambda b, s: (b, s, 0)),
                  pl.BlockSpec((1, tm * SUBLANE, LANE), lambda b, s: (0, b * nt + s, 0)),
                  pl.BlockSpec((1, tm * SUBLANE, LANE), lambda b, s: (1, b * nt + s, 0)),
                  pl.BlockSpec((1, 1, D), lambda b, s: (b, 0, 0))],
        out_specs=pl.BlockSpec((1, tm, D), lambda b, s: (b, s, 0)),
        out_shape=jax.ShapeDtypeStruct((B, S, D), F32),
        compiler_params=_cp(("parallel", "parallel")),
        name="moe_combine",
    )(x, y2, y2, gt)


def _tiles(S):
    big = S >= 2048
    return dict(tm=512 if big else 256, tl=512 if big else 256, ts_s5=64, ts_lru=128 if big else 64,
                tq=1024 if big else 256)


def kernel(x, c, pos_offset, ada_w, ada_b, norm1_g, w_in, rwkv_mu, rwkv_w0, rwkv_w2, rwkv_a0, rwkv_a2, rwkv_g2, rwkv_k_k, rwkv_k_a, rwkv_r_k, rwkv_ln_w, rwkv_ln_b, s5_lambda_re, s5_lambda_im, s5_b_re, s5_b_im, s5_c_re, s5_c_im, s5_d, s5_log_dt, s5_glu_w, s5_glu_b, mla_q_norm_g, mla_w_uq, mla_kv_norm_g, mla_w_ukv, mla_q_head_g, mla_k_head_g, lru_conv_w, lru_conv_b, lru_w_a, lru_b_a, lru_w_x, lru_b_x, lru_lambda, branch_norm_g, w_out, norm2_g, moe_w_group, moe_b_group, moe_w_expert, moe_b_expert, moe_w1, moe_w3, moe_w2):
    B, S, D = x.shape
    depth = ada_w.shape[0]
    T = B * S
    W = D // N_MIX
    assert B == SUBLANE, "time-major scans put the batch on the 8 sublanes"
    assert D == SUBLANE * LANE, "token-tile layout: one (8, 128) tile per token row"
    tiles = _tiles(S)
    mod = _adaln_mod(c, ada_w, ada_b)

    def to_tm(t):
        return jnp.swapaxes(t, 0, 1).reshape(S * B, t.shape[-1])

    def from_tm(t):
        return jnp.swapaxes(t.reshape(S, B, t.shape[-1]), 0, 1)

    for l in range(depth):
        sh1, sc1, gt1, sh2, sc2, gt2 = [mod[l, :, None, i * D:(i + 1) * D] for i in range(6)]
        z_rwkv, z_s5, q_a, kv_a, kpe, z_lru, z_gate = _inproj(
            x, sc1, sh1, norm1_g[l].reshape(1, D), _pad_w_in(w_in[l]), tiles['tm'])
        y_a = _rwkv(z_rwkv, rwkv_mu[l], rwkv_w0[l], rwkv_w2[l], rwkv_a0[l], rwkv_a2[l], rwkv_g2[l],
                    rwkv_k_k[l], rwkv_k_a[l], rwkv_r_k[l].reshape(-1), rwkv_ln_w[l], rwkv_ln_b[l], tiles['tl'])
        y_b = from_tm(_s5(to_tm(z_s5), s5_lambda_re[l], s5_lambda_im[l], s5_b_re[l], s5_b_im[l], s5_c_re[l],
                          s5_c_im[l], s5_d[l], s5_log_dt[l], s5_glu_w[l], s5_glu_b[l], branch_norm_g[l, 0],
                          tiles['ts_s5']))
        q, k, v = _mla_prep(q_a, kv_a, kpe, pos_offset, mla_q_norm_g[l], mla_w_uq[l], mla_kv_norm_g[l],
                            mla_w_ukv[l], mla_q_head_g[l], mla_k_head_g[l], tiles['tm'])
        y_c = _attention(q, k, v, branch_norm_g[l, 1], tiles['tq'])
        y_d = from_tm(_lru(to_tm(z_lru), to_tm(z_gate), lru_conv_w[l], lru_conv_b[l], lru_w_a[l], lru_b_a[l],
                           lru_w_x[l], lru_b_x[l], lru_lambda[l], branch_norm_g[l, 2], tiles['ts_lru']))
        w_router = jnp.zeros((D, LANE), F32).at[:, :N_GROUPS].set(moe_w_group[l])
        w_router = w_router.at[:, N_GROUPS:N_GROUPS + N_EXPERTS].set(moe_w_expert[l])
        b_router = jnp.zeros((1, LANE), F32).at[0, :N_GROUPS].set(moe_b_group[l])
        b_router = b_router.at[0, N_GROUPS:N_GROUPS + N_EXPERTS].set(moe_b_expert[l])
        x, h2, logits = _outproj(x, y_a, y_b, y_c, y_d, w_out[l], gt1, sc2, sh2, norm2_g[l].reshape(1, D),
                                 w_router, b_router, tiles['tm'])
        blk_expert, n_used, row_tok, row_dst, row_w = _route(logits.reshape(T, LANE), T)
        y2 = _moe_experts(h2.reshape(T * SUBLANE, LANE), blk_expert, n_used, row_tok, row_dst, row_w,
                          moe_w1[l].astype(BF16), moe_w3[l].astype(BF16), moe_w2[l].astype(BF16))
        x = _combine(x, y2.reshape(TOP_K, -1, LANE), gt2, tiles['tm'])
    return x
```

```python
import functools
import math

import numpy as np
import jax
import jax.numpy as jnp
from jax import lax
from jax.experimental import pallas as pl
from jax.experimental.pallas import tpu as pltpu

F32 = jnp.float32
BF16 = jnp.bfloat16
HI = lax.Precision.HIGHEST

N_MIX = 4
RWKV_HEAD = 64
RWKV_W_RANK, RWKV_A_RANK, RWKV_G_RANK = 32, 32, 64
RWKV_LN_EPS = 64e-5
S5_GROUP, S5_STATE = 16, 64
MLA_HEADS, MLA_NOPE, MLA_ROPE, MLA_V = 4, 64, 32, 64
MLA_QK = MLA_NOPE + MLA_ROPE
ROPE_THETA = 10000.0
LRU_C = 8.0
CONV_WIDTH = 4
ATTN_CHUNK = 64
N_GROUPS, EXPERTS_PER_GROUP, TOP_K = 4, 8, 2
N_EXPERTS = N_GROUPS * EXPERTS_PER_GROUP
MOE_BLOCK = 256
NORM_EPS = 1e-6

LANE = 128
SUBLANE = 8
RWKV_CHUNK = 64
VMEM_LIMIT = 48 * 1024 * 1024


def _cp(sem, vmem=VMEM_LIMIT):
    return pltpu.CompilerParams(dimension_semantics=sem, vmem_limit_bytes=vmem)


def _dot(a, b):
    return jnp.dot(a, b, preferred_element_type=F32)


def _dot_hi(a, b):
    return jnp.dot(a, b, precision=HI, preferred_element_type=F32)


def _dot_nt(a, b):
    return lax.dot_general(a, b, (((1,), (1,)), ((), ())), preferred_element_type=F32)


def _dot_nt_hi(a, b):
    return lax.dot_general(a, b, (((1,), (1,)), ((), ())), precision=HI, preferred_element_type=F32)


def _softplus(x):
    return jnp.maximum(x, 0.0) + jnp.log1p(jnp.exp(-jnp.abs(x)))


def _rms(x, g, eps=NORM_EPS):
    return x * lax.rsqrt(jnp.mean(x * x, axis=-1, keepdims=True) + eps) * g


def _mod_kernel(c_ref, w_ref, b_ref, o_ref):
    c = c_ref[...]
    cond = c * jax.nn.sigmoid(c)
    o_ref[0] = _dot_hi(cond, w_ref[0]) + b_ref[0]


def _adaln_mod(c, ada_w, ada_b):
    L, D, D6 = ada_w.shape
    B = c.shape[0]
    nj = D6 // D
    return pl.pallas_call(
        _mod_kernel,
        grid=(L, nj),
        in_specs=[pl.BlockSpec((B, D), lambda l, j: (0, 0)),
                  pl.BlockSpec((1, D, D), lambda l, j: (l, 0, j)),
                  pl.BlockSpec((1, 1, D), lambda l, j: (l, 0, j))],
        out_specs=pl.BlockSpec((1, B, D), lambda l, j: (l, 0, j)),
        out_shape=jax.ShapeDtypeStruct((L, B, D6), F32),
        compiler_params=_cp(("parallel", "parallel")),
        name="adaln_mod",
    )(c, ada_w, ada_b.reshape(L, 1, D6))


_RWKV_IN = 3 * 256 + RWKV_W_RANK + RWKV_A_RANK + RWKV_G_RANK
_IN_COLS = (_RWKV_IN, 256, 256, 128, LANE, 256, 256)
_IN_OFFS = tuple(int(v) for v in np.cumsum((0,) + _IN_COLS))
_KPE_LANE0 = MLA_NOPE


def _pad_w_in(w_in):
    D = w_in.shape[0]
    o = np.cumsum((0, _RWKV_IN, 256, 256, 128, MLA_ROPE, 256, 256))
    pieces = [w_in[:, o[i]:o[i + 1]] for i in range(7)]
    kpe = jnp.zeros((D, LANE), w_in.dtype).at[:, _KPE_LANE0:_KPE_LANE0 + MLA_ROPE].set(pieces[4])
    pieces[4] = kpe
    return jnp.concatenate(pieces, axis=1).astype(BF16)


def _inproj_kernel(pos_ref, x_ref, sc_ref, sh_ref, g_ref, w_ref, gq_ref, gkv_ref, wq_ref, wk_ref, wv_ref,
                   gqh_ref, gkh_ref, freq_ref, zr_ref, zs_ref, zl_ref, zg_ref, q_ref, k_ref, vt_ref):
    x = x_ref[0]
    tm = x.shape[0]
    h = _rms(x, g_ref[...]) * (1.0 + sc_ref[0]) + sh_ref[0]
    z = _dot(h.astype(BF16), w_ref[...])
    piece = lambda i: z[:, _IN_OFFS[i]:_IN_OFFS[i + 1]]
    zr_ref[0] = piece(0)
    zs_ref[0] = piece(1)
    zl_ref[0] = piece(5)
    zg_ref[0] = piece(6)
    pos0 = pos_ref[pl.program_id(0)] + pl.program_id(1) * tm
    _mla_prep_tile(pos0, piece(2), piece(3), piece(4), gq_ref, gkv_ref, wq_ref, wk_ref, wv_ref,
                   gqh_ref, gkh_ref, freq_ref, q_ref, k_ref, vt_ref)


def _inproj(x, sc, sh, g, w_pad, pos_offset, mla, tm):
    B, S, D = x.shape
    H = MLA_HEADS
    row = lambda b, s, p: (b, s, 0)
    vec = lambda b, s, p: (b, 0, 0)
    const = lambda a: pl.BlockSpec(a.shape, lambda b, s, p: (0,) * a.ndim)
    hout = pl.BlockSpec((1, H, tm, LANE), lambda b, s, p: (b, 0, s, 0))
    zcols = [_IN_COLS[i] for i in (0, 1, 5, 6)]
    grid_spec = pltpu.PrefetchScalarGridSpec(
        num_scalar_prefetch=1, grid=(B, S // tm),
        in_specs=[pl.BlockSpec((1, tm, D), row), pl.BlockSpec((1, 1, D), vec), pl.BlockSpec((1, 1, D), vec),
                  const(g), const(w_pad)] + [const(a) for a in mla],
        out_specs=[pl.BlockSpec((1, tm, n), row) for n in zcols]
        + [hout, hout, pl.BlockSpec((1, H, MLA_V, tm), lambda b, s, p: (b, 0, 0, s))])
    return pl.pallas_call(
        _inproj_kernel,
        grid_spec=grid_spec,
        out_shape=[jax.ShapeDtypeStruct((B, S, n), F32) for n in zcols]
        + [jax.ShapeDtypeStruct((B, H, S, LANE), BF16)] * 2 + [jax.ShapeDtypeStruct((B, H, MLA_V, S), BF16)],
        compiler_params=_cp(("parallel", "parallel")),
        name="inproj",
    )(pos_offset, x, sc, sh, g, w_pad, *mla)


def _split_bf16(x, parts):
    out = []
    for _ in range(parts - 1):
        hi = x.astype(BF16)
        out.append(hi)
        x = x - hi.astype(F32)
    out.append(x.astype(BF16))
    return out


def _rwkv_kernel(z_ref, mu_ref, w0_ref, a0_ref, wl_ref, kk_ref, ka_ref, rk_ref, lnw_ref, lnb_ref,
                 hsum_ref, tri_ref, o_ref, zprev_ref, g_ref, y_ref):
    W = 256
    NH = W // RWKV_HEAD
    L = RWKV_CHUNK
    TL = z_ref.shape[1]

    @pl.when(pl.program_id(1) == 0)
    def _():
        zprev_ref[...] = jnp.zeros_like(zprev_ref)
        g_ref[...] = jnp.zeros_like(g_ref)

    hsum = hsum_ref[...]

    def head_sum(t):
        hi, lo_ = _split_bf16(t, 2)
        return _dot(hi, hsum) + _dot(lo_, hsum)

    z = z_ref[0]
    rows = lax.broadcasted_iota(jnp.int32, z.shape, 0)
    zp = jnp.where(rows == 0, zprev_ref[...], pltpu.roll(z, 1, axis=0))
    zprev_ref[...] = z[TL - 1:TL, :]
    zs = z + (zp - z) * mu_ref[...]
    r, k, v, lo = zs[:, 0:W], zs[:, W:2 * W], zs[:, 2 * W:3 * W], zs[:, 3 * W:3 * W + LANE]
    lane = lax.broadcasted_iota(jnp.int32, lo.shape, 1)
    act = jnp.where(lane < RWKV_W_RANK, jnp.tanh(lo),
                    jnp.where(lane < RWKV_W_RANK + RWKV_A_RANK, lo, jax.nn.sigmoid(lo)))
    up = _dot(act.astype(BF16), wl_ref[...])
    w = -_softplus(-(w0_ref[...] + up[:, 0:W])) - 0.5
    ld = -jnp.exp(w)
    a = jax.nn.sigmoid(a0_ref[...] + up[:, W:2 * W])
    g = up[:, 2 * W:3 * W]
    kk = k * kk_ref[...]
    kk = kk / jnp.maximum(jnp.sqrt(head_sum(kk * kk)), 1e-12)
    k2 = k * (1.0 + (a - 1.0) * ka_ref[...])
    kka = kk * a

    tri_b = tri_ref[0]
    tri_incl = tri_b > 0
    tri_strict = tri_ref[1] > 0
    eye = (lax.broadcasted_iota(jnp.int32, (RWKV_HEAD, RWKV_HEAD), 0)
           == lax.broadcasted_iota(jnp.int32, (RWKV_HEAD, RWKV_HEAD), 1)).astype(BF16)

    NC = TL // L
    pairs = [(j, h) for j in range(NC) for h in range(NH)]
    hsl = [slice(h * RWKV_HEAD, (h + 1) * RWKV_HEAD) for h in range(NH)]
    csl = [slice(j * L, (j + 1) * L) for j in range(NC)]
    bt, rt, a_s, k_s, a_e, k_e, v_c, p_l = [], [], [], [], [], [], [], []
    for j in range(NC):
        ld_c = ld[csl[j]]
        cs = sum(_dot(tri_b, p) for p in _split_bf16(ld_c, 3))
        cs_l = cs[L - 1:L, :]
        p_inv = jnp.exp(-cs)
        p_end = jnp.exp(cs_l - cs)
        p_l.append(jnp.exp(cs_l))
        bt.append(-kk[csl[j]] * jnp.exp(cs - ld_c))
        rt.append(r[csl[j]] * jnp.exp(cs))
        a_s.append((kka[csl[j]] * p_inv).astype(BF16))
        k_s.append((k2[csl[j]] * p_inv).astype(BF16))
        a_e.append((kka[csl[j]] * p_end).astype(BF16))
        k_e.append((k2[csl[j]] * p_end).astype(BF16))
        v_c.append(v[csl[j]].astype(BF16))

    x = [jnp.concatenate([bt[j][:, hsl[h]], rt[j][:, hsl[h]]], axis=0).astype(BF16) for j, h in pairs]
    ga = [_dot_nt(x[i], a_s[j][:, hsl[h]]) for i, (j, h) in enumerate(pairs)]
    gk = [_dot_nt(x[i], k_s[j][:, hsl[h]]) for i, (j, h) in enumerate(pairs)]
    m_ba = [jnp.where(tri_strict, t[:L], 0.0).astype(BF16) for t in ga]
    m_ra = [jnp.where(tri_incl, t[L:], 0.0).astype(BF16) for t in ga]
    m_bk = [jnp.where(tri_strict, t[:L], 0.0).astype(BF16) for t in gk]
    m_rk = [jnp.where(tri_incl, t[L:], 0.0).astype(BF16) for t in gk]
    v_h = [v_c[j][:, hsl[h]] for j, h in pairs]
    wn = [jnp.concatenate([bt[j][:, hsl[h]], _dot(m_bk[i], v_h[i])], axis=1) for i, (j, h) in enumerate(pairs)]
    n_pow = m_ba
    wn = [w_ + _dot(n_, w_.astype(BF16)) for w_, n_ in zip(wn, n_pow)]
    for _ in range(5):
        n_pow = [_dot(n_, n_).astype(BF16) for n_ in n_pow]
        wn = [w_ + _dot(n_, w_.astype(BF16)) for w_, n_ in zip(wn, n_pow)]
    wn_b = [w_.astype(BF16) for w_ in wn]
    mw = [_dot(m_, w_) for m_, w_ in zip(m_ra, wn_b)]
    r2 = [(rt[j][:, hsl[h]] + mw[i][:, :RWKV_HEAD]).astype(BF16) for i, (j, h) in enumerate(pairs)]
    y0 = [mw[i][:, RWKV_HEAD:] + _dot(m_rk[i], v_h[i]) for i in range(len(pairs))]
    b2_t = [_dot_nt(eye, w_[:, :RWKV_HEAD]).astype(BF16) for w_ in wn_b]
    c_m = [_dot(b2_t[i], a_e[j][:, hsl[h]]).astype(BF16) for i, (j, h) in enumerate(pairs)]
    uv_t = [_dot_nt(eye, jnp.concatenate([wn_b[i][:, RWKV_HEAD:], v_h[i]], axis=0)).astype(BF16)
            for i in range(len(pairs))]
    d_t = [_dot(uv_t[i], jnp.concatenate([a_e[j][:, hsl[h]], k_e[j][:, hsl[h]]], axis=0))
           for i, (j, h) in enumerate(pairs)]
    for i, (j, h) in enumerate(pairs):
        g0 = g_ref[h]
        g_hi, g_lo = _split_bf16(g0, 2)
        y_ref[csl[j], hsl[h]] = _dot_nt(r2[i], g_hi) + _dot_nt(r2[i], g_lo) + y0[i]
        g_ref[h] = g0 * p_l[j][:, hsl[h]] + _dot(g_hi, c_m[i]) + _dot(g_lo, c_m[i]) + d_t[i]

    y = y_ref[...]
    inv_n = 1.0 / RWKV_HEAD
    mean = head_sum(y) * inv_n
    yc = y - mean
    var = head_sum(yc * yc) * inv_n
    yn = yc * lax.rsqrt(var + RWKV_LN_EPS) * lnw_ref[...] + lnb_ref[...]
    bonus = head_sum(r * k2 * rk_ref[...]) * v
    o_ref[0] = ((yn + bonus) * g).astype(o_ref.dtype)


def _rwkv(z, mu, w0, w2, a0, a2, g2, k_k, k_a, r_k, ln_w, ln_b, tl):
    B, S, _ = z.shape
    W = 256
    L = RWKV_CHUNK
    wl = jnp.zeros((LANE, 3 * W), F32)
    wl = wl.at[0:RWKV_W_RANK, 0:W].set(w2)
    wl = wl.at[RWKV_W_RANK:RWKV_W_RANK + RWKV_A_RANK, W:2 * W].set(a2)
    wl = wl.at[RWKV_W_RANK + RWKV_A_RANK:LANE, 2 * W:3 * W].set(g2).astype(BF16)
    hid = np.arange(W) // RWKV_HEAD
    hsum = jnp.asarray((hid[:, None] == hid[None, :]).astype(np.float32), dtype=BF16)
    t = np.arange(L)
    tri = jnp.asarray(np.stack([(t[None, :] <= t[:, None]), (t[None, :] < t[:, None])]).astype(np.float32), dtype=BF16)
    row2 = lambda v_: v_.reshape(1, -1)
    const = lambda shape: pl.BlockSpec(shape, lambda b, s: (0,) * len(shape))
    return pl.pallas_call(
        _rwkv_kernel,
        grid=(B, S // tl),
        in_specs=[pl.BlockSpec((1, tl, _RWKV_IN), lambda b, s: (b, s, 0)),
                  const((1, _RWKV_IN)), const((1, W)), const((1, W)), const((LANE, 3 * W)),
                  const((1, W)), const((1, W)), const((1, W)), const((1, W)), const((1, W)),
                  const((W, W)), const((2, L, L))],
        out_specs=pl.BlockSpec((1, tl, W), lambda b, s: (b, s, 0)),
        out_shape=jax.ShapeDtypeStruct((B, S, W), BF16),
        scratch_shapes=[pltpu.VMEM((1, _RWKV_IN), F32),
                        pltpu.VMEM((W // RWKV_HEAD, RWKV_HEAD, RWKV_HEAD), F32),
                        pltpu.VMEM((tl, W), F32)],
        compiler_params=_cp(("parallel", "arbitrary")),
        name="rwkv7",
    )(z, row2(mu), row2(w0), row2(a0), wl, row2(k_k), row2(k_a), row2(r_k), row2(ln_w), row2(ln_b), hsum, tri)


def _s5_kernel(u_ref, wb_ref, are_ref, aim_ref, wc_ref, d_ref, gw_ref, gb_ref, ng_ref, o_ref,
               hre_ref, him_ref, sre_ref, sim_ref):
    nb = SUBLANE
    rows = u_ref.shape[0]
    P = are_ref.shape[1]

    @pl.when(pl.program_id(0) == 0)
    def _():
        sre_ref[...] = jnp.zeros_like(sre_ref)
        sim_ref[...] = jnp.zeros_like(sim_ref)

    u = u_ref[...]
    bu = _dot(u.astype(BF16), wb_ref[...])
    hre_ref[...] = bu[:, :P]
    him_ref[...] = bu[:, P:]
    a_re = jnp.broadcast_to(are_ref[...], (nb, P))
    a_im = jnp.broadcast_to(aim_ref[...], (nb, P))

    def step(t, carry):
        h_re, h_im = carry
        i = pl.multiple_of(t * nb, nb)
        n_re = a_re * h_re - a_im * h_im + hre_ref[pl.ds(i, nb), :]
        n_im = a_re * h_im + a_im * h_re + him_ref[pl.ds(i, nb), :]
        hre_ref[pl.ds(i, nb), :] = n_re
        him_ref[pl.ds(i, nb), :] = n_im
        return n_re, n_im

    h_re, h_im = lax.fori_loop(0, rows // nb, step, (sre_ref[...], sim_ref[...]))
    sre_ref[...] = h_re
    sim_ref[...] = h_im
    y = _dot(hre_ref[...].astype(BF16), wc_ref[0]) + _dot(him_ref[...].astype(BF16), wc_ref[1])
    y = jax.nn.gelu(y + d_ref[...] * u)
    y = y * jax.nn.sigmoid(_dot(y.astype(BF16), gw_ref[...]) + gb_ref[...])
    o_ref[...] = _rms(y, ng_ref[...]).astype(o_ref.dtype)


def _s5(u_tm, lam_re, lam_im, b_re, b_im, c_re, c_im, d, log_dt, glu_w, glu_b, norm_g, ts):
    rows_total, W = u_tm.shape
    G, P = lam_re.shape
    dt = jnp.exp(log_dt)[:, None]
    mag = jnp.exp(lam_re * dt)
    a_re = mag * jnp.cos(lam_im * dt)
    a_im = mag * jnp.sin(lam_im * dt)
    den = lam_re * lam_re + lam_im * lam_im
    q_re = ((a_re - 1.0) * lam_re + a_im * lam_im) / den
    q_im = (a_im * lam_re - (a_re - 1.0) * lam_im) / den
    bb_re = q_re[..., None] * b_re - q_im[..., None] * b_im
    bb_im = q_re[..., None] * b_im + q_im[..., None] * b_re
    eye = jnp.eye(G, dtype=F32)
    wb_re = jnp.einsum('gpi,gh->gihp', bb_re, eye).reshape(W, G * P)
    wb_im = jnp.einsum('gpi,gh->gihp', bb_im, eye).reshape(W, G * P)
    wb = jnp.concatenate([wb_re, wb_im], axis=1).astype(BF16)
    wc_re = jnp.einsum('gip,gh->gphi', c_re, eye).reshape(G * P, W)
    wc_im = jnp.einsum('gip,gh->gphi', -c_im, eye).reshape(G * P, W)
    wc = jnp.stack([wc_re, wc_im]).astype(BF16)
    GP = G * P
    rows = ts * SUBLANE
    const = lambda shape: pl.BlockSpec(shape, lambda s: (0,) * len(shape))
    return pl.pallas_call(
        _s5_kernel,
        grid=(rows_total // rows,),
        in_specs=[pl.BlockSpec((rows, W), lambda s: (s, 0)),
                  const((W, 2 * GP)), const((1, GP)), const((1, GP)), const((2, GP, W)),
                  const((1, W)), const((W, W)), const((1, W)), const((1, W))],
        out_specs=pl.BlockSpec((rows, W), lambda s: (s, 0)),
        out_shape=jax.ShapeDtypeStruct((rows_total, W), BF16),
        scratch_shapes=[pltpu.VMEM((rows, GP), F32), pltpu.VMEM((rows, GP), F32),
                        pltpu.VMEM((SUBLANE, GP), F32), pltpu.VMEM((SUBLANE, GP), F32)],
        compiler_params=_cp(("arbitrary",)),
        name="s5",
    )(u_tm, wb, a_re.reshape(1, GP), a_im.reshape(1, GP), wc, d.reshape(1, W), glu_w.astype(BF16),
      glu_b.reshape(1, W), norm_g.reshape(1, W))


def _lru_kernel(x_ref, gate_ref, cw_ref, cb_ref, wax_ref, bax_ref, lam_ref, ng_ref, o_ref,
                xc_ref, a_ref, b_ref, hs_ref):
    nb = SUBLANE
    rows, W = x_ref.shape
    halo = (CONV_WIDTH - 1) * nb

    @pl.when(pl.program_id(0) == 0)
    def _():
        xc_ref[...] = jnp.zeros_like(xc_ref)
        hs_ref[...] = jnp.zeros_like(hs_ref)

    x = x_ref[...]
    xf = jnp.concatenate([xc_ref[...], x], axis=0)
    xc_ref[...] = x[rows - halo:, :]
    xc = cb_ref[...]
    for j in range(CONV_WIDTH):
        xc = xc + cw_ref[j:j + 1, :] * xf[j * nb:j * nb + rows, :]
    gates = _dot(xc.astype(BF16), wax_ref[...]) + bax_ref[...]
    r = jax.nn.sigmoid(gates[:, :W])
    i = jax.nn.sigmoid(gates[:, W:])
    log_a = -LRU_C * r * _softplus(-lam_ref[...])
    a = jnp.exp(log_a)
    a_ref[...] = a
    b_ref[...] = jnp.sqrt(1.0 - jnp.exp(2.0 * log_a)) * (i * xc)

    def step(t, h):
        i0 = pl.multiple_of(t * nb, nb)
        h = a_ref[pl.ds(i0, nb), :] * h + b_ref[pl.ds(i0, nb), :]
        b_ref[pl.ds(i0, nb), :] = h
        return h

    hs_ref[...] = lax.fori_loop(0, rows // nb, step, hs_ref[...])
    y = b_ref[...] * jax.nn.gelu(gate_ref[...])
    o_ref[...] = _rms(y, ng_ref[...]).astype(o_ref.dtype)


def _lru(x_tm, gate_tm, conv_w, conv_b, w_a, b_a, w_x, b_x, lam, norm_g, ts):
    rows_total, W = x_tm.shape
    nblk, bs, _ = w_a.shape
    eye = jnp.eye(nblk, dtype=F32)
    bd = lambda w_: jnp.einsum('nkj,nm->nkmj', w_, eye).reshape(W, W)
    wax = jnp.concatenate([bd(w_a), bd(w_x)], axis=1).astype(BF16)
    bax = jnp.concatenate([b_a, b_x]).reshape(1, 2 * W)
    rows = ts * SUBLANE
    const = lambda shape: pl.BlockSpec(shape, lambda s: (0,) * len(shape))
    return pl.pallas_call(
        _lru_kernel,
        grid=(rows_total // rows,),
        in_specs=[pl.BlockSpec((rows, W), lambda s: (s, 0)), pl.BlockSpec((rows, W), lambda s: (s, 0)),
                  const((CONV_WIDTH, W)), const((1, W)), const((W, 2 * W)), const((1, 2 * W)),
                  const((1, W)), const((1, W))],
        out_specs=pl.BlockSpec((rows, W), lambda s: (s, 0)),
        out_shape=jax.ShapeDtypeStruct((rows_total, W), BF16),
        scratch_shapes=[pltpu.VMEM(((CONV_WIDTH - 1) * SUBLANE, W), F32), pltpu.VMEM((rows, W), F32),
                        pltpu.VMEM((rows, W), F32), pltpu.VMEM((SUBLANE, W), F32)],
        compiler_params=_cp(("arbitrary",)),
        name="rglru",
    )(x_tm, gate_tm, conv_w, conv_b.reshape(1, W), wax, bax, lam.reshape(1, W), norm_g.reshape(1, W))


def _mla_prep_tile(pos0, q_a, kv_a, kpe, gq_ref, gkv_ref, wq_ref, wk_ref, wv_ref,
                   gqh_ref, gkh_ref, freq_ref, q_ref, k_ref, v_ref):
    tm = q_a.shape[0]
    pos = (pos0 + lax.broadcasted_iota(jnp.int32, (tm, LANE), 0)).astype(F32)
    ang = pos * freq_ref[...]
    cos_t = jnp.cos(ang)
    sin_t = jnp.sin(ang)
    lane = lax.broadcasted_iota(jnp.int32, (tm, LANE), 1)
    half = MLA_ROPE // 2
    first = (lane >= MLA_NOPE) & (lane < MLA_NOPE + half)
    second = (lane >= MLA_NOPE + half) & (lane < MLA_QK)

    def rope(t):
        rot = jnp.where(first, -pltpu.roll(t, LANE - half, axis=1),
                        jnp.where(second, pltpu.roll(t, half, axis=1), 0.0))
        return t * cos_t + rot * sin_t

    def head_norm(t, g):
        ms = jnp.sum(t * t, axis=-1, keepdims=True) * (1.0 / MLA_QK)
        return t * lax.rsqrt(ms + NORM_EPS) * g

    qn = _rms(q_a, gq_ref[...]).astype(BF16)
    kvn = _rms(kv_a, gkv_ref[...]).astype(BF16)
    q_all = _dot(qn, wq_ref[...])
    k_all = _dot(kvn, wk_ref[...])
    v_all = _dot(kvn, wv_ref[...])
    scale = MLA_QK ** -0.5
    for h in range(MLA_HEADS):
        hs = slice(h * LANE, (h + 1) * LANE)
        q_ref[0, h] = (rope(head_norm(q_all[:, hs], gqh_ref[...])) * scale).astype(q_ref.dtype)
        k_ref[0, h] = rope(head_norm(k_all[:, hs] + kpe, gkh_ref[...])).astype(k_ref.dtype)
        v_ref[0, h] = v_all[:, hs].T[:MLA_V].astype(v_ref.dtype)


def _mla_params(q_norm_g, w_uq, kv_norm_g, w_ukv, q_head_g, k_head_g):
    QR, KVR = w_uq.shape[0], w_ukv.shape[0]
    H = MLA_HEADS
    wq = jnp.zeros((QR, H, LANE), F32).at[:, :, :MLA_QK].set(w_uq.reshape(QR, H, MLA_QK)).reshape(QR, H * LANE)
    wkv = w_ukv.reshape(KVR, H, MLA_NOPE + MLA_V)
    wk = jnp.zeros((KVR, H, LANE), F32).at[:, :, :MLA_NOPE].set(wkv[:, :, :MLA_NOPE]).reshape(KVR, H * LANE)
    wv = jnp.zeros((KVR, H, LANE), F32).at[:, :, :MLA_V].set(wkv[:, :, MLA_NOPE:]).reshape(KVR, H * LANE)
    pad_g = lambda g: jnp.zeros((1, LANE), F32).at[0, :MLA_QK].set(g)
    half = MLA_ROPE // 2
    inv_freq = np.power(np.float32(ROPE_THETA), -np.arange(half, dtype=np.float32) * np.float32(2.0) / np.float32(MLA_ROPE))
    freq = np.zeros((1, LANE), np.float32)
    freq[0, MLA_NOPE:MLA_NOPE + half] = inv_freq
    freq[0, MLA_NOPE + half:MLA_QK] = inv_freq
    return (q_norm_g.reshape(1, QR), kv_norm_g.reshape(1, KVR), wq.astype(BF16), wk.astype(BF16), wv.astype(BF16),
            pad_g(q_head_g), pad_g(k_head_g), jnp.asarray(freq))


def _attn_kernel(q_ref, k_ref, vt_ref, ng_ref, o_ref, m_ref, l_ref, acc_ref):
    H = q_ref.shape[1]
    tq = q_ref.shape[2]
    qi = pl.program_id(1)
    ki = pl.program_id(2)

    @pl.when(ki == 0)
    def _():
        m_ref[...] = jnp.full_like(m_ref, -1e30)
        l_ref[...] = jnp.zeros_like(l_ref)
        acc_ref[...] = jnp.zeros_like(acc_ref)

    def update(masked):
        if masked:
            kc = lax.broadcasted_iota(jnp.int32, (tq, tq), 0) // ATTN_CHUNK
            qc = lax.broadcasted_iota(jnp.int32, (tq, tq), 1) // ATTN_CHUNK
            visible = kc <= qc
        for h in range(H):
            st = _dot_nt(k_ref[0, h], q_ref[0, h])
            if masked:
                st = jnp.where(visible, st, -1e30)
            m_old = m_ref[h]
            m_new = jnp.maximum(m_old, jnp.max(st, axis=0, keepdims=True))
            alpha = jnp.exp(m_old - m_new)
            p = jnp.exp(st - m_new)
            l_ref[h] = alpha * l_ref[h] + jnp.sum(p, axis=0, keepdims=True)
            acc_ref[h] = alpha * acc_ref[h] + _dot(vt_ref[0, h], p.astype(BF16))
            m_ref[h] = m_new

    @pl.when(ki < qi)
    def _():
        update(False)

    @pl.when(ki == qi)
    def _():
        update(True)
        ot = jnp.concatenate([acc_ref[h] / l_ref[h] for h in range(H)], axis=0)
        o_ref[0] = _rms(ot.T, ng_ref[...]).astype(o_ref.dtype)


def _attention(q, k, vt, norm_g, tq):
    B, H, S, _ = q.shape
    W = H * MLA_V
    nq = S // tq
    qspec = pl.BlockSpec((1, H, tq, LANE), lambda b, i, j: (b, 0, i, 0))
    kspec = pl.BlockSpec((1, H, tq, LANE), lambda b, i, j: (b, 0, jnp.minimum(i, j), 0))
    vspec = pl.BlockSpec((1, H, MLA_V, tq), lambda b, i, j: (b, 0, 0, jnp.minimum(i, j)))
    return pl.pallas_call(
        _attn_kernel,
        grid=(B, nq, nq),
        in_specs=[qspec, kspec, vspec, pl.BlockSpec((1, W), lambda b, i, j: (0, 0))],
        out_specs=pl.BlockSpec((1, tq, W), lambda b, i, j: (b, i, 0)),
        out_shape=jax.ShapeDtypeStruct((B, S, W), BF16),
        scratch_shapes=[pltpu.VMEM((H, 1, tq), F32), pltpu.VMEM((H, 1, tq), F32),
                        pltpu.VMEM((H, MLA_V, tq), F32)],
        compiler_params=_cp(("parallel", "parallel", "arbitrary")),
        name="mla_attn",
    )(q, k, vt, norm_g.reshape(1, W))


def _store_token_tiles(ref, val):
    n, d = val.shape
    for s in range(d // LANE):
        ref[pl.ds(s, n, stride=SUBLANE), :] = val[:, s * LANE:(s + 1) * LANE]


def _load_token_tiles(ref, n):
    return jnp.concatenate([ref[pl.ds(s, n, stride=SUBLANE), :] for s in range(SUBLANE)], axis=1)


def _outproj_kernel(x_ref, ya_ref, yb_ref, yc_ref, yd_ref, wo_ref, gt_ref, sc_ref, sh_ref, g_ref,
                    wr_ref, br_ref, xo_ref, h_ref, lg_ref):
    acc = _dot(ya_ref[0], wo_ref[0])
    acc = acc + _dot(yb_ref[0], wo_ref[1])
    acc = acc + _dot(yc_ref[0], wo_ref[2])
    acc = acc + _dot(yd_ref[0], wo_ref[3])
    x = x_ref[0] + gt_ref[0] * acc
    xo_ref[0] = x
    h = _rms(x, g_ref[...]) * (1.0 + sc_ref[0]) + sh_ref[0]
    _store_token_tiles(h_ref.at[0], h)
    h_hi, h_lo = _split_bf16(h, 2)
    lg_ref[0] = _dot(h_hi, wr_ref[0]) + (_dot(h_lo, wr_ref[0]) + _dot(h_hi, wr_ref[1])) + br_ref[...]


def _outproj(x, ya, yb, yc, yd, w_out, gt, sc, sh, g, w_router, b_router, tm):
    B, S, D = x.shape
    W = D // N_MIX
    row = lambda n: pl.BlockSpec((1, tm, n), lambda b, s: (b, s, 0))
    vec = pl.BlockSpec((1, 1, D), lambda b, s: (b, 0, 0))
    const = lambda shape: pl.BlockSpec(shape, lambda b, s: (0,) * len(shape))
    return pl.pallas_call(
        _outproj_kernel,
        grid=(B, S // tm),
        in_specs=[row(D), row(W), row(W), row(W), row(W), const((N_MIX, W, D)), vec, vec, vec,
                  const((1, D)), const((2, D, LANE)), const((1, LANE))],
        out_specs=[row(D), pl.BlockSpec((1, tm * SUBLANE, LANE), lambda b, s: (b, s, 0)), row(LANE)],
        out_shape=[jax.ShapeDtypeStruct((B, S, D), F32), jax.ShapeDtypeStruct((B, S * SUBLANE, LANE), F32),
                   jax.ShapeDtypeStruct((B, S, LANE), F32)],
        compiler_params=_cp(("parallel", "parallel")),
        name="outproj",
    )(x, ya, yb, yc, yd, w_out.reshape(N_MIX, W, D).astype(BF16), gt, sc, sh, g,
      jnp.stack(_split_bf16(w_router, 2)), b_router)


_MOE_PAD_ROWS = MOE_BLOCK // TOP_K


def _moe_kernel(be_ref, nused_ref, tok_ref, tokn_ref, dst_ref, w_ref, h_hbm, w1_ref, w3_ref, w2_ref, o_hbm,
                xbuf, ybuf, gsem, ssem):
    i = pl.program_id(0)
    R = MOE_BLOCK
    TR = SUBLANE
    nused = nused_ref[0]
    slot = lax.rem(i, 2)
    other = 1 - slot

    def gather_rows(idx_ref, s):
        def body(r, c):
            src = pl.multiple_of(idx_ref[0, 0, r], TR)
            pltpu.make_async_copy(h_hbm.at[pl.ds(src, TR)], xbuf.at[s, pl.ds(pl.multiple_of(r * TR, TR), TR)],
                                  gsem.at[s]).start()
            return c
        lax.fori_loop(0, R, body, 0, unroll=8)

    def scatter_rows(s):
        def body(r, c):
            dst = pl.multiple_of(dst_ref[0, 0, r], TR)
            pltpu.make_async_copy(ybuf.at[s, pl.ds(pl.multiple_of(r * TR, TR), TR)], o_hbm.at[pl.ds(dst, TR)],
                                  ssem.at[s]).start(priority=1)
            return c
        lax.fori_loop(0, R, body, 0, unroll=8)

    def wait_gather(s):
        pltpu.make_async_copy(h_hbm.at[pl.ds(0, R * TR)], xbuf.at[s], gsem.at[s]).wait()

    def wait_scatter(s):
        pltpu.make_async_copy(ybuf.at[s], o_hbm.at[pl.ds(0, R * TR)], ssem.at[s]).wait()

    @pl.when(i == 0)
    def _():
        plane = o_hbm.shape[0] // TOP_K
        npad = _MOE_PAD_ROWS * TR
        ybuf[1] = jnp.zeros(ybuf.shape[1:], ybuf.dtype)
        for j in range(TOP_K):
            pad = pltpu.make_async_copy(ybuf.at[1, pl.ds(0, npad)],
                                        o_hbm.at[pl.ds((j + 1) * plane - npad, npad)], ssem.at[1])
            pad.start()
            pad.wait()
        gather_rows(tok_ref, 0)

    @pl.when(i < nused)
    def _():
        @pl.when(i + 1 < nused)
        def _():
            gather_rows(tokn_ref, other)

        wait_gather(slot)

        @pl.when(i >= 2)
        def _():
            wait_scatter(slot)

        xb = _load_token_tiles(xbuf.at[slot], R).astype(BF16)
        hid = _dot(xb, w1_ref[0])
        hid = hid * jax.nn.sigmoid(hid) * _dot(xb, w3_ref[0])
        y = _dot(hid.astype(BF16), w2_ref[0])
        _store_token_tiles(ybuf.at[slot], y * w_ref[:, 0:1])
        scatter_rows(slot)

        @pl.when(i == nused - 1)
        def _():
            wait_scatter(slot)

            @pl.when(i >= 1)
            def _():
                wait_scatter(other)


def _moe_experts(h2, blk_expert, n_used, row_tok, row_dst, row_w, w1, w3, w2):
    E, D, F = w1.shape
    T = h2.shape[0] // SUBLANE
    nb = blk_expert.shape[0]
    R = MOE_BLOCK
    blk = lambda i, be, nu: (i, 0, 0)
    nxt = lambda i, be, nu: (jnp.minimum(i + 1, nb - 1), 0, 0)
    wsel = lambda i, be, nu: (be[i], 0, 0)
    plane = (T + _MOE_PAD_ROWS) * SUBLANE
    row_tok = (row_tok * SUBLANE).reshape(nb, 1, R)
    row_dst = ((row_dst % TOP_K) * plane + (row_dst // TOP_K) * SUBLANE).reshape(nb, 1, R)
    grid_spec = pltpu.PrefetchScalarGridSpec(
        num_scalar_prefetch=2, grid=(nb,),
        in_specs=[pl.BlockSpec((1, 1, R), blk, memory_space=pltpu.SMEM),
                  pl.BlockSpec((1, 1, R), nxt, memory_space=pltpu.SMEM),
                  pl.BlockSpec((1, 1, R), blk, memory_space=pltpu.SMEM),
                  pl.BlockSpec((R, LANE), lambda i, be, nu: (i, 0)),
                  pl.BlockSpec(memory_space=pl.ANY),
                  pl.BlockSpec((1, D, F), wsel), pl.BlockSpec((1, D, F), wsel), pl.BlockSpec((1, F, D), wsel)],
        out_specs=pl.BlockSpec(memory_space=pl.ANY),
        scratch_shapes=[pltpu.VMEM((2, R * SUBLANE, LANE), F32), pltpu.VMEM((2, R * SUBLANE, LANE), F32),
                        pltpu.SemaphoreType.DMA((2,)), pltpu.SemaphoreType.DMA((2,))])
    return pl.pallas_call(
        _moe_kernel,
        grid_spec=grid_spec,
        out_shape=jax.ShapeDtypeStruct((TOP_K * plane, LANE), F32),
        compiler_params=_cp(("arbitrary",)),
        name="moe_experts",
    )(blk_expert, n_used, row_tok, row_tok, row_dst, jnp.broadcast_to(row_w.reshape(nb * R, 1), (nb * R, LANE)),
      h2, w1, w3, w2)


def _route(logits, T):
    p_group = jax.nn.softmax(logits[:, :N_GROUPS], axis=-1)
    g_sel = jnp.argmax(p_group, axis=-1).astype(jnp.int32)
    g_prob = jnp.max(p_group, axis=-1)
    logit_e = logits[:, N_GROUPS:N_GROUPS + N_EXPERTS].reshape(T, N_GROUPS, EXPERTS_PER_GROUP)
    logit_e = jnp.take_along_axis(logit_e, g_sel[:, None, None], axis=1)[:, 0]
    p_e = jax.nn.softmax(logit_e, axis=-1)
    top_p, top_i = lax.top_k(p_e, TOP_K)
    gate = (g_prob[:, None] * top_p / jnp.sum(top_p, axis=-1, keepdims=True)).reshape(-1)
    expert = (g_sel[:, None] * EXPERTS_PER_GROUP + top_i.astype(jnp.int32)).reshape(-1)
    n_assign = T * TOP_K
    n_blocks = -(-n_assign // MOE_BLOCK) + N_EXPERTS
    order = jnp.argsort(expert).astype(jnp.int32)
    counts = jnp.sum((expert[:, None] == jnp.arange(N_EXPERTS, dtype=jnp.int32)[None, :]).astype(jnp.int32), axis=0)
    start = jnp.cumsum(counts) - counts
    pcounts = (counts + MOE_BLOCK - 1) // MOE_BLOCK * MOE_BLOCK
    pend = jnp.cumsum(pcounts)
    pstart = pend - pcounts
    blk_start = jnp.arange(n_blocks, dtype=jnp.int32) * MOE_BLOCK
    blk_expert = jnp.minimum(jnp.sum((pend[None, :] <= blk_start[:, None]).astype(jnp.int32), axis=1), N_EXPERTS - 1)
    n_used = (pend[-1] // MOE_BLOCK).astype(jnp.int32).reshape(1)
    lane = jnp.arange(MOE_BLOCK, dtype=jnp.int32)[None, :]
    off = blk_start[:, None] + lane - pstart[blk_expert][:, None]
    valid = off < counts[blk_expert][:, None]
    pos = jnp.clip(start[blk_expert][:, None] + off, 0, n_assign - 1)
    asg = order[pos]
    row_tok = jnp.where(valid, asg // TOP_K, 0)
    row_dst = jnp.where(valid, asg, n_assign + lane)
    row_w = jnp.where(valid, gate[asg], 0.0)
    return blk_expert.astype(jnp.int32), n_used, row_tok, row_dst, row_w


def _combine_kernel(x_ref, y0_ref, y1_ref, gt_ref, o_ref):
    tm = x_ref.shape[1]
    y = _load_token_tiles(y0_ref.at[0], tm) + _load_token_tiles(y1_ref.at[0], tm)
    o_ref[0] = x_ref[0] + gt_ref[0] * y


def _combine(x, y2, gt, tm):
    B, S, D = x.shape
    nt = S // tm
    return pl.pallas_call(
        _combine_kernel,
        grid=(B, nt),
        in_specs=[pl.BlockSpec((1, tm, D), lambda b, s: (b, s, 0)),
                  pl.BlockSpec((1, tm * SUBLANE, LANE), lambda b, s: (0, b * nt + s, 0)),
                  pl.BlockSpec((1, tm * SUBLANE, LANE), lambda b, s: (1, b * nt + s, 0)),
                  pl.BlockSpec((1, 1, D), lambda b, s: (b, 0, 0))],
        out_specs=pl.BlockSpec((1, tm, D), lambda b, s: (b, s, 0)),
        out_shape=jax.ShapeDtypeStruct((B, S, D), F32),
        compiler_params=_cp(("parallel", "parallel")),
        name="moe_combine",
    )(x, y2, y2, gt)


def _tiles(S):
    big = S >= 2048
    return dict(tm=512 if big else 256, tl=512 if big else 256, ts_s5=64, ts_lru=128 if big else 64,
                tq=1024 if big else 256)


def kernel(x, c, pos_offset, ada_w, ada_b, norm1_g, w_in, rwkv_mu, rwkv_w0, rwkv_w2, rwkv_a0, rwkv_a2, rwkv_g2, rwkv_k_k, rwkv_k_a, rwkv_r_k, rwkv_ln_w, rwkv_ln_b, s5_lambda_re, s5_lambda_im, s5_b_re, s5_b_im, s5_c_re, s5_c_im, s5_d, s5_log_dt, s5_glu_w, s5_glu_b, mla_q_norm_g, mla_w_uq, mla_kv_norm_g, mla_w_ukv, mla_q_head_g, mla_k_head_g, lru_conv_w, lru_conv_b, lru_w_a, lru_b_a, lru_w_x, lru_b_x, lru_lambda, branch_norm_g, w_out, norm2_g, moe_w_group, moe_b_group, moe_w_expert, moe_b_expert, moe_w1, moe_w3, moe_w2):
    B, S, D = x.shape
    depth = ada_w.shape[0]
    T = B * S
    W = D // N_MIX
    assert B == SUBLANE, "time-major scans put the batch on the 8 sublanes"
    assert D == SUBLANE * LANE, "token-tile layout: one (8, 128) tile per token row"
    tiles = _tiles(S)
    mod = _adaln_mod(c, ada_w, ada_b)

    def to_tm(t):
        return jnp.swapaxes(t, 0, 1).reshape(S * B, t.shape[-1])

    def from_tm(t):
        return jnp.swapaxes(t.reshape(S, B, t.shape[-1]), 0, 1)

    for l in range(depth):
        sh1, sc1, gt1, sh2, sc2, gt2 = [mod[l, :, None, i * D:(i + 1) * D] for i in range(6)]
        mla = _mla_params(mla_q_norm_g[l], mla_w_uq[l], mla_kv_norm_g[l], mla_w_ukv[l], mla_q_head_g[l],
                          mla_k_head_g[l])
        z_rwkv, z_s5, z_lru, z_gate, q, k, vt = _inproj(
            x, sc1, sh1, norm1_g[l].reshape(1, D), _pad_w_in(w_in[l]), pos_offset, mla, tiles['tm'])
        y_a = _rwkv(z_rwkv, rwkv_mu[l], rwkv_w0[l], rwkv_w2[l], rwkv_a0[l], rwkv_a2[l], rwkv_g2[l],
                    rwkv_k_k[l], rwkv_k_a[l], rwkv_r_k[l].reshape(-1), rwkv_ln_w[l], rwkv_ln_b[l], tiles['tl'])
        y_b = from_tm(_s5(to_tm(z_s5), s5_lambda_re[l], s5_lambda_im[l], s5_b_re[l], s5_b_im[l], s5_c_re[l],
                          s5_c_im[l], s5_d[l], s5_log_dt[l], s5_glu_w[l], s5_glu_b[l], branch_norm_g[l, 0],
                          tiles['ts_s5']))
        y_c = _attention(q, k, vt, branch_norm_g[l, 1], tiles['tq'])
        y_d = from_tm(_lru(to_tm(z_lru), to_tm(z_gate), lru_conv_w[l], lru_conv_b[l], lru_w_a[l], lru_b_a[l],
                           lru_w_x[l], lru_b_x[l], lru_lambda[l], branch_norm_g[l, 2], tiles['ts_lru']))
        w_router = jnp.zeros((D, LANE), F32).at[:, :N_GROUPS].set(moe_w_group[l])
        w_router = w_router.at[:, N_GROUPS:N_GROUPS + N_EXPERTS].set(moe_w_expert[l])
        b_router = jnp.zeros((1, LANE), F32).at[0, :N_GROUPS].set(moe_b_group[l])
        b_router = b_router.at[0, N_GROUPS:N_GROUPS + N_EXPERTS].set(moe_b_expert[l])
        x, h2, logits = _outproj(x, y_a, y_b, y_c, y_d, w_out[l], gt1, sc2, sh2, norm2_g[l].reshape(1, D),
                                 w_router, b_router, tiles['tm'])
        blk_expert, n_used, row_tok, row_dst, row_w = _route(logits.reshape(T, LANE), T)
        y2 = _moe_experts(h2.reshape(T * SUBLANE, LANE), blk_expert, n_used, row_tok, row_dst, row_w,
                          moe_w1[l].astype(BF16), moe_w3[l].astype(BF16), moe_w2[l].astype(BF16))
        x = _combine(x, y2.reshape(TOP_K, -1, LANE), gt2, tiles['tm'])
    return x
```

```python
import functools
import math

import numpy as np
import jax
import jax.numpy as jnp
from jax import lax
from jax.experimental import pallas as pl
from jax.experimental.pallas import tpu as pltpu

F32 = jnp.float32
BF16 = jnp.bfloat16
HI = lax.Precision.HIGHEST

N_MIX = 4
RWKV_HEAD = 64
RWKV_W_RANK, RWKV_A_RANK, RWKV_G_RANK = 32, 32, 64
RWKV_LN_EPS = 64e-5
S5_GROUP, S5_STATE = 16, 64
MLA_HEADS, MLA_NOPE, MLA_ROPE, MLA_V = 4, 64, 32, 64
MLA_QK = MLA_NOPE + MLA_ROPE
ROPE_THETA = 10000.0
LRU_C = 8.0
CONV_WIDTH = 4
ATTN_CHUNK = 64
N_GROUPS, EXPERTS_PER_GROUP, TOP_K = 4, 8, 2
N_EXPERTS = N_GROUPS * EXPERTS_PER_GROUP
MOE_BLOCK = 256
NORM_EPS = 1e-6

LANE = 128
SUBLANE = 8
RWKV_CHUNK = 64
VMEM_LIMIT = 48 * 1024 * 1024


def _cp(sem, vmem=VMEM_LIMIT):
    return pltpu.CompilerParams(dimension_semantics=sem, vmem_limit_bytes=vmem)


def _dot(a, b):
    return jnp.dot(a, b, preferred_element_type=F32)


def _dot_hi(a, b):
    return jnp.dot(a, b, precision=HI, preferred_element_type=F32)


def _dot_nt(a, b):
    return lax.dot_general(a, b, (((1,), (1,)), ((), ())), preferred_element_type=F32)


def _dot_nt_hi(a, b):
    return lax.dot_general(a, b, (((1,), (1,)), ((), ())), precision=HI, preferred_element_type=F32)


def _softplus(x):
    return jnp.maximum(x, 0.0) + jnp.log1p(jnp.exp(-jnp.abs(x)))


def _rms(x, g, eps=NORM_EPS):
    return x * lax.rsqrt(jnp.mean(x * x, axis=-1, keepdims=True) + eps) * g


def _mod_kernel(c_ref, w_ref, b_ref, o_ref):
    c = c_ref[...]
    cond = c * jax.nn.sigmoid(c)
    o_ref[0] = _dot_hi(cond, w_ref[0]) + b_ref[0]


def _adaln_mod(c, ada_w, ada_b):
    L, D, D6 = ada_w.shape
    B = c.shape[0]
    nj = D6 // D
    return pl.pallas_call(
        _mod_kernel,
        grid=(L, nj),
        in_specs=[pl.BlockSpec((B, D), lambda l, j: (0, 0)),
                  pl.BlockSpec((1, D, D), lambda l, j: (l, 0, j)),
                  pl.BlockSpec((1, 1, D), lambda l, j: (l, 0, j))],
        out_specs=pl.BlockSpec((1, B, D), lambda l, j: (l, 0, j)),
        out_shape=jax.ShapeDtypeStruct((L, B, D6), F32),
        compiler_params=_cp(("parallel", "parallel")),
        name="adaln_mod",
    )(c, ada_w, ada_b.reshape(L, 1, D6))


_RWKV_IN = 3 * 256 + RWKV_W_RANK + RWKV_A_RANK + RWKV_G_RANK
_IN_COLS = (_RWKV_IN, 256, 256, 128, LANE, 256, 256)
_IN_OFFS = tuple(int(v) for v in np.cumsum((0,) + _IN_COLS))
_KPE_LANE0 = MLA_NOPE


def _pad_w_in(w_in):
    D = w_in.shape[0]
    o = np.cumsum((0, _RWKV_IN, 256, 256, 128, MLA_ROPE, 256, 256))
    pieces = [w_in[:, o[i]:o[i + 1]] for i in range(7)]
    kpe = jnp.zeros((D, LANE), w_in.dtype).at[:, _KPE_LANE0:_KPE_LANE0 + MLA_ROPE].set(pieces[4])
    pieces[4] = kpe
    return jnp.concatenate(pieces, axis=1).astype(BF16)


def _inproj_kernel(x_ref, sc_ref, sh_ref, g_ref, w_ref, *outs):
    x = x_ref[0]
    h = _rms(x, g_ref[...]) * (1.0 + sc_ref[0]) + sh_ref[0]
    z = _dot(h.astype(BF16), w_ref[...])
    for i, o_ref in enumerate(outs):
        o_ref[0] = z[:, _IN_OFFS[i]:_IN_OFFS[i + 1]]


def _inproj(x, sc, sh, g, w_pad, tm):
    B, S, D = x.shape
    row = lambda b, s: (b, s, 0)
    vec = lambda b, s: (b, 0, 0)
    return pl.pallas_call(
        _inproj_kernel,
        grid=(B, S // tm),
        in_specs=[pl.BlockSpec((1, tm, D), row),
                  pl.BlockSpec((1, 1, D), vec),
                  pl.BlockSpec((1, 1, D), vec),
                  pl.BlockSpec((1, D), lambda b, s: (0, 0)),
                  pl.BlockSpec(w_pad.shape, lambda b, s: (0, 0))],
        out_specs=[pl.BlockSpec((1, tm, n), row) for n in _IN_COLS],
        out_shape=[jax.ShapeDtypeStruct((B, S, n), F32) for n in _IN_COLS],
        compiler_params=_cp(("parallel", "parallel")),
        name="inproj",
    )(x, sc, sh, g, w_pad)


def _split_bf16(x, parts):
    out = []
    for _ in range(parts - 1):
        hi = x.astype(BF16)
        out.append(hi)
        x = x - hi.astype(F32)
    out.append(x.astype(BF16))
    return out


def _rwkv_kernel(z_ref, mu_ref, w0_ref, a0_ref, wl_ref, kk_ref, ka_ref, rk_ref, lnw_ref, lnb_ref,
                 hsum_ref, tri_ref, o_ref, zprev_ref, g_ref, y_ref):
    W = 256
    NH = W // RWKV_HEAD
    L = RWKV_CHUNK
    TL = z_ref.shape[1]

    @pl.when(pl.program_id(1) == 0)
    def _():
        zprev_ref[...] = jnp.zeros_like(zprev_ref)
        g_ref[...] = jnp.zeros_like(g_ref)

    hsum = hsum_ref[...]

    def head_sum(t):
        hi, lo_ = _split_bf16(t, 2)
        return _dot(hi, hsum) + _dot(lo_, hsum)

    z = z_ref[0]
    rows = lax.broadcasted_iota(jnp.int32, z.shape, 0)
    zp = jnp.where(rows == 0, zprev_ref[...], pltpu.roll(z, 1, axis=0))
    zprev_ref[...] = z[TL - 1:TL, :]
    zs = z + (zp - z) * mu_ref[...]
    r, k, v, lo = zs[:, 0:W], zs[:, W:2 * W], zs[:, 2 * W:3 * W], zs[:, 3 * W:3 * W + LANE]
    lane = lax.broadcasted_iota(jnp.int32, lo.shape, 1)
    act = jnp.where(lane < RWKV_W_RANK, jnp.tanh(lo),
                    jnp.where(lane < RWKV_W_RANK + RWKV_A_RANK, lo, jax.nn.sigmoid(lo)))
    up = _dot(act.astype(BF16), wl_ref[...])
    w = -_softplus(-(w0_ref[...] + up[:, 0:W])) - 0.5
    ld = -jnp.exp(w)
    a = jax.nn.sigmoid(a0_ref[...] + up[:, W:2 * W])
    g = up[:, 2 * W:3 * W]
    kk = k * kk_ref[...]
    kk = kk / jnp.maximum(jnp.sqrt(head_sum(kk * kk)), 1e-12)
    k2 = k * (1.0 + (a - 1.0) * ka_ref[...])
    kka = kk * a

    tri_b = tri_ref[0]
    tri_incl = tri_b > 0
    tri_strict = tri_ref[1] > 0
    eye = (lax.broadcasted_iota(jnp.int32, (RWKV_HEAD, RWKV_HEAD), 0)
           == lax.broadcasted_iota(jnp.int32, (RWKV_HEAD, RWKV_HEAD), 1)).astype(BF16)

    NC = TL // L
    pairs = [(j, h) for j in range(NC) for h in range(NH)]
    hsl = [slice(h * RWKV_HEAD, (h + 1) * RWKV_HEAD) for h in range(NH)]
    csl = [slice(j * L, (j + 1) * L) for j in range(NC)]
    bt, rt, a_s, k_s, a_e, k_e, v_c, p_l = [], [], [], [], [], [], [], []
    for j in range(NC):
        ld_c = ld[csl[j]]
        cs = sum(_dot(tri_b, p) for p in _split_bf16(ld_c, 3))
        cs_l = cs[L - 1:L, :]
        p_inv = jnp.exp(-cs)
        p_end = jnp.exp(cs_l - cs)
        p_l.append(jnp.exp(cs_l))
        bt.append(-kk[csl[j]] * jnp.exp(cs - ld_c))
        rt.append(r[csl[j]] * jnp.exp(cs))
        a_s.append((kka[csl[j]] * p_inv).astype(BF16))
        k_s.append((k2[csl[j]] * p_inv).astype(BF16))
        a_e.append((kka[csl[j]] * p_end).astype(BF16))
        k_e.append((k2[csl[j]] * p_end).astype(BF16))
        v_c.append(v[csl[j]].astype(BF16))

    x = [jnp.concatenate([bt[j][:, hsl[h]], rt[j][:, hsl[h]]], axis=0).astype(BF16) for j, h in pairs]
    ga = [_dot_nt(x[i], a_s[j][:, hsl[h]]) for i, (j, h) in enumerate(pairs)]
    gk = [_dot_nt(x[i], k_s[j][:, hsl[h]]) for i, (j, h) in enumerate(pairs)]
    m_ba = [jnp.where(tri_strict, t[:L], 0.0).astype(BF16) for t in ga]
    m_ra = [jnp.where(tri_incl, t[L:], 0.0).astype(BF16) for t in ga]
    m_bk = [jnp.where(tri_strict, t[:L], 0.0).astype(BF16) for t in gk]
    m_rk = [jnp.where(tri_incl, t[L:], 0.0).astype(BF16) for t in gk]
    v_h = [v_c[j][:, hsl[h]] for j, h in pairs]
    wn = [jnp.concatenate([bt[j][:, hsl[h]], _dot(m_bk[i], v_h[i])], axis=1) for i, (j, h) in enumerate(pairs)]
    n_pow = m_ba
    wn = [w_ + _dot(n_, w_.astype(BF16)) for w_, n_ in zip(wn, n_pow)]
    for _ in range(5):
        n_pow = [_dot(n_, n_).astype(BF16) for n_ in n_pow]
        wn = [w_ + _dot(n_, w_.astype(BF16)) for w_, n_ in zip(wn, n_pow)]
    wn_b = [w_.astype(BF16) for w_ in wn]
    mw = [_dot(m_, w_) for m_, w_ in zip(m_ra, wn_b)]
    r2 = [(rt[j][:, hsl[h]] + mw[i][:, :RWKV_HEAD]).astype(BF16) for i, (j, h) in enumerate(pairs)]
    y0 = [mw[i][:, RWKV_HEAD:] + _dot(m_rk[i], v_h[i]) for i in range(len(pairs))]
    b2_t = [_dot_nt(eye, w_[:, :RWKV_HEAD]).astype(BF16) for w_ in wn_b]
    c_m = [_dot(b2_t[i], a_e[j][:, hsl[h]]).astype(BF16) for i, (j, h) in enumerate(pairs)]
    uv_t = [_dot_nt(eye, jnp.concatenate([wn_b[i][:, RWKV_HEAD:], v_h[i]], axis=0)).astype(BF16)
            for i in range(len(pairs))]
    d_t = [_dot(uv_t[i], jnp.concatenate([a_e[j][:, hsl[h]], k_e[j][:, hsl[h]]], axis=0))
           for i, (j, h) in enumerate(pairs)]
    for i, (j, h) in enumerate(pairs):
        g0 = g_ref[h]
        g_hi, g_lo = _split_bf16(g0, 2)
        y_ref[csl[j], hsl[h]] = _dot_nt(r2[i], g_hi) + _dot_nt(r2[i], g_lo) + y0[i]
        g_ref[h] = g0 * p_l[j][:, hsl[h]] + _dot(g_hi, c_m[i]) + _dot(g_lo, c_m[i]) + d_t[i]

    y = y_ref[...]
    inv_n = 1.0 / RWKV_HEAD
    mean = head_sum(y) * inv_n
    yc = y - mean
    var = head_sum(yc * yc) * inv_n
    yn = yc * lax.rsqrt(var + RWKV_LN_EPS) * lnw_ref[...] + lnb_ref[...]
    bonus = head_sum(r * k2 * rk_ref[...]) * v
    o_ref[0] = ((yn + bonus) * g).astype(o_ref.dtype)


def _rwkv(z, mu, w0, w2, a0, a2, g2, k_k, k_a, r_k, ln_w, ln_b, tl):
    B, S, _ = z.shape
    W = 256
    L = RWKV_CHUNK
    wl = jnp.zeros((LANE, 3 * W), F32)
    wl = wl.at[0:RWKV_W_RANK, 0:W].set(w2)
    wl = wl.at[RWKV_W_RANK:RWKV_W_RANK + RWKV_A_RANK, W:2 * W].set(a2)
    wl = wl.at[RWKV_W_RANK + RWKV_A_RANK:LANE, 2 * W:3 * W].set(g2).astype(BF16)
    hid = np.arange(W) // RWKV_HEAD
    hsum = jnp.asarray((hid[:, None] == hid[None, :]).astype(np.float32), dtype=BF16)
    t = np.arange(L)
    tri = jnp.asarray(np.stack([(t[None, :] <= t[:, None]), (t[None, :] < t[:, None])]).astype(np.float32), dtype=BF16)
    row2 = lambda v_: v_.reshape(1, -1)
    const = lambda shape: pl.BlockSpec(shape, lambda b, s: (0,) * len(shape))
    return pl.pallas_call(
        _rwkv_kernel,
        grid=(B, S // tl),
        in_specs=[pl.BlockSpec((1, tl, _RWKV_IN), lambda b, s: (b, s, 0)),
                  const((1, _RWKV_IN)), const((1, W)), const((1, W)), const((LANE, 3 * W)),
                  const((1, W)), const((1, W)), const((1, W)), const((1, W)), const((1, W)),
                  const((W, W)), const((2, L, L))],
        out_specs=pl.BlockSpec((1, tl, W), lambda b, s: (b, s, 0)),
        out_shape=jax.ShapeDtypeStruct((B, S, W), BF16),
        scratch_shapes=[pltpu.VMEM((1, _RWKV_IN), F32),
                        pltpu.VMEM((W // RWKV_HEAD, RWKV_HEAD, RWKV_HEAD), F32),
                        pltpu.VMEM((tl, W), F32)],
        compiler_params=_cp(("parallel", "arbitrary")),
        name="rwkv7",
    )(z, row2(mu), row2(w0), row2(a0), wl, row2(k_k), row2(k_a), row2(r_k), row2(ln_w), row2(ln_b), hsum, tri)


def _s5_kernel(u_ref, wb_ref, are_ref, aim_ref, wc_ref, d_ref, gw_ref, gb_ref, ng_ref, o_ref,
               hre_ref, him_ref, sre_ref, sim_ref):
    nb = SUBLANE
    rows = u_ref.shape[0]
    P = are_ref.shape[1]

    @pl.when(pl.program_id(0) == 0)
    def _():
        sre_ref[...] = jnp.zeros_like(sre_ref)
        sim_ref[...] = jnp.zeros_like(sim_ref)

    u = u_ref[...]
    bu = _dot(u.astype(BF16), wb_ref[...])
    hre_ref[...] = bu[:, :P]
    him_ref[...] = bu[:, P:]
    a_re = jnp.broadcast_to(are_ref[...], (nb, P))
    a_im = jnp.broadcast_to(aim_ref[...], (nb, P))

    def step(t, carry):
        h_re, h_im = carry
        i = pl.multiple_of(t * nb, nb)
        n_re = a_re * h_re - a_im * h_im + hre_ref[pl.ds(i, nb), :]
        n_im = a_re * h_im + a_im * h_re + him_ref[pl.ds(i, nb), :]
        hre_ref[pl.ds(i, nb), :] = n_re
        him_ref[pl.ds(i, nb), :] = n_im
        return n_re, n_im

    h_re, h_im = lax.fori_loop(0, rows // nb, step, (sre_ref[...], sim_ref[...]))
    sre_ref[...] = h_re
    sim_ref[...] = h_im
    y = _dot(hre_ref[...].astype(BF16), wc_ref[0]) + _dot(him_ref[...].astype(BF16), wc_ref[1])
    y = jax.nn.gelu(y + d_ref[...] * u)
    y = y * jax.nn.sigmoid(_dot(y.astype(BF16), gw_ref[...]) + gb_ref[...])
    o_ref[...] = _rms(y, ng_ref[...]).astype(o_ref.dtype)


def _s5(u_tm, lam_re, lam_im, b_re, b_im, c_re, c_im, d, log_dt, glu_w, glu_b, norm_g, ts):
    rows_total, W = u_tm.shape
    G, P = lam_re.shape
    dt = jnp.exp(log_dt)[:, None]
    mag = jnp.exp(lam_re * dt)
    a_re = mag * jnp.cos(lam_im * dt)
    a_im = mag * jnp.sin(lam_im * dt)
    den = lam_re * lam_re + lam_im * lam_im
    q_re = ((a_re - 1.0) * lam_re + a_im * lam_im) / den
    q_im = (a_im * lam_re - (a_re - 1.0) * lam_im) / den
    bb_re = q_re[..., None] * b_re - q_im[..., None] * b_im
    bb_im = q_re[..., None] * b_im + q_im[..., None] * b_re
    eye = jnp.eye(G, dtype=F32)
    wb_re = jnp.einsum('gpi,gh->gihp', bb_re, eye).reshape(W, G * P)
    wb_im = jnp.einsum('gpi,gh->gihp', bb_im, eye).reshape(W, G * P)
    wb = jnp.concatenate([wb_re, wb_im], axis=1).astype(BF16)
    wc_re = jnp.einsum('gip,gh->gphi', c_re, eye).reshape(G * P, W)
    wc_im = jnp.einsum('gip,gh->gphi', -c_im, eye).reshape(G * P, W)
    wc = jnp.stack([wc_re, wc_im]).astype(BF16)
    GP = G * P
    rows = ts * SUBLANE
    const = lambda shape: pl.BlockSpec(shape, lambda s: (0,) * len(shape))
    return pl.pallas_call(
        _s5_kernel,
        grid=(rows_total // rows,),
        in_specs=[pl.BlockSpec((rows, W), lambda s: (s, 0)),
                  const((W, 2 * GP)), const((1, GP)), const((1, GP)), const((2, GP, W)),
                  const((1, W)), const((W, W)), const((1, W)), const((1, W))],
        out_specs=pl.BlockSpec((rows, W), lambda s: (s, 0)),
        out_shape=jax.ShapeDtypeStruct((rows_total, W), BF16),
        scratch_shapes=[pltpu.VMEM((rows, GP), F32), pltpu.VMEM((rows, GP), F32),
                        pltpu.VMEM((SUBLANE, GP), F32), pltpu.VMEM((SUBLANE, GP), F32)],
        compiler_params=_cp(("arbitrary",)),
        name="s5",
    )(u_tm, wb, a_re.reshape(1, GP), a_im.reshape(1, GP), wc, d.reshape(1, W), glu_w.astype(BF16),
      glu_b.reshape(1, W), norm_g.reshape(1, W))


def _lru_kernel(x_ref, gate_ref, cw_ref, cb_ref, wax_ref, bax_ref, lam_ref, ng_ref, o_ref,
                xc_ref, a_ref, b_ref, hs_ref):
    nb = SUBLANE
    rows, W = x_ref.shape
    halo = (CONV_WIDTH - 1) * nb

    @pl.when(pl.program_id(0) == 0)
    def _():
        xc_ref[...] = jnp.zeros_like(xc_ref)
        hs_ref[...] = jnp.zeros_like(hs_ref)

    x = x_ref[...]
    xf = jnp.concatenate([xc_ref[...], x], axis=0)
    xc_ref[...] = x[rows - halo:, :]
    xc = cb_ref[...]
    for j in range(CONV_WIDTH):
        xc = xc + cw_ref[j:j + 1, :] * xf[j * nb:j * nb + rows, :]
    gates = _dot(xc.astype(BF16), wax_ref[...]) + bax_ref[...]
    r = jax.nn.sigmoid(gates[:, :W])
    i = jax.nn.sigmoid(gates[:, W:])
    log_a = -LRU_C * r * _softplus(-lam_ref[...])
    a = jnp.exp(log_a)
    a_ref[...] = a
    b_ref[...] = jnp.sqrt(1.0 - jnp.exp(2.0 * log_a)) * (i * xc)

    def step(t, h):
        i0 = pl.multiple_of(t * nb, nb)
        h = a_ref[pl.ds(i0, nb), :] * h + b_ref[pl.ds(i0, nb), :]
        b_ref[pl.ds(i0, nb), :] = h
        return h

    hs_ref[...] = lax.fori_loop(0, rows // nb, step, hs_ref[...])
    y = b_ref[...] * jax.nn.gelu(gate_ref[...])
    o_ref[...] = _rms(y, ng_ref[...]).astype(o_ref.dtype)


def _lru(x_tm, gate_tm, conv_w, conv_b, w_a, b_a, w_x, b_x, lam, norm_g, ts):
    rows_total, W = x_tm.shape
    nblk, bs, _ = w_a.shape
    eye = jnp.eye(nblk, dtype=F32)
    bd = lambda w_: jnp.einsum('nkj,nm->nkmj', w_, eye).reshape(W, W)
    wax = jnp.concatenate([bd(w_a), bd(w_x)], axis=1).astype(BF16)
    bax = jnp.concatenate([b_a, b_x]).reshape(1, 2 * W)
    rows = ts * SUBLANE
    const = lambda shape: pl.BlockSpec(shape, lambda s: (0,) * len(shape))
    return pl.pallas_call(
        _lru_kernel,
        grid=(rows_total // rows,),
        in_specs=[pl.BlockSpec((rows, W), lambda s: (s, 0)), pl.BlockSpec((rows, W), lambda s: (s, 0)),
                  const((CONV_WIDTH, W)), const((1, W)), const((W, 2 * W)), const((1, 2 * W)),
                  const((1, W)), const((1, W))],
        out_specs=pl.BlockSpec((rows, W), lambda s: (s, 0)),
        out_shape=jax.ShapeDtypeStruct((rows_total, W), BF16),
        scratch_shapes=[pltpu.VMEM(((CONV_WIDTH - 1) * SUBLANE, W), F32), pltpu.VMEM((rows, W), F32),
                        pltpu.VMEM((rows, W), F32), pltpu.VMEM((SUBLANE, W), F32)],
        compiler_params=_cp(("arbitrary",)),
        name="rglru",
    )(x_tm, gate_tm, conv_w, conv_b.reshape(1, W), wax, bax, lam.reshape(1, W), norm_g.reshape(1, W))


_VT_ROWS = MLA_V + 2 * SUBLANE


def _mla_prep_tile(pos0, q_a, kv_a, kpe, gq_ref, gkv_ref, wq_ref, wk_ref, wv_ref,
                   gqh_ref, gkh_ref, freq_ref, q_ref, k_ref, v_ref):
    tm = q_a.shape[0]
    pos = (pos0 + lax.broadcasted_iota(jnp.int32, (tm, LANE), 0)).astype(F32)
    ang = pos * freq_ref[...]
    cos_t = jnp.cos(ang)
    sin_t = jnp.sin(ang)
    lane = lax.broadcasted_iota(jnp.int32, (tm, LANE), 1)
    half = MLA_ROPE // 2
    first = (lane >= MLA_NOPE) & (lane < MLA_NOPE + half)
    second = (lane >= MLA_NOPE + half) & (lane < MLA_QK)

    def rope(t):
        rot = jnp.where(first, -pltpu.roll(t, LANE - half, axis=1),
                        jnp.where(second, pltpu.roll(t, half, axis=1), 0.0))
        return t * cos_t + rot * sin_t

    def head_norm(t, g):
        ms = jnp.sum(t * t, axis=-1, keepdims=True) * (1.0 / MLA_QK)
        return t * lax.rsqrt(ms + NORM_EPS) * g

    qn = _rms(q_a, gq_ref[...]).astype(BF16)
    kvn = _rms(kv_a, gkv_ref[...]).astype(BF16)
    q_all = _dot(qn, wq_ref[...])
    k_all = _dot(kvn, wk_ref[...])
    v_all = _dot(kvn, wv_ref[...])
    scale = MLA_QK ** -0.5 * math.log2(math.e)
    ones_row = lax.broadcasted_iota(jnp.int32, (_VT_ROWS, tm), 0) == MLA_V
    for h in range(MLA_HEADS):
        hs = slice(h * LANE, (h + 1) * LANE)
        q_ref[0, h] = (rope(head_norm(q_all[:, hs], gqh_ref[...])) * scale).astype(q_ref.dtype)
        k_ref[0, h] = rope(head_norm(k_all[:, hs] + kpe, gkh_ref[...])).astype(k_ref.dtype)
        v_ref[0, h] = jnp.where(ones_row, 1.0, v_all[:, hs].T[:_VT_ROWS]).astype(v_ref.dtype)


def _mla_prep_kernel(pos_ref, qa_ref, kva_ref, kpe_ref, *rest):
    pos0 = pos_ref[pl.program_id(0)] + pl.program_id(1) * qa_ref.shape[1]
    _mla_prep_tile(pos0, qa_ref[0], kva_ref[0], kpe_ref[0], *rest)


def _mla_prep(q_a, kv_a, kpe, pos_offset, q_norm_g, w_uq, kv_norm_g, w_ukv, q_head_g, k_head_g, tm):
    B, S, QR = q_a.shape
    KVR = kv_a.shape[2]
    H = MLA_HEADS
    wq = jnp.zeros((QR, H, LANE), F32).at[:, :, :MLA_QK].set(w_uq.reshape(QR, H, MLA_QK)).reshape(QR, H * LANE)
    wkv = w_ukv.reshape(KVR, H, MLA_NOPE + MLA_V)
    wk = jnp.zeros((KVR, H, LANE), F32).at[:, :, :MLA_NOPE].set(wkv[:, :, :MLA_NOPE]).reshape(KVR, H * LANE)
    wv = jnp.zeros((KVR, H, LANE), F32).at[:, :, :MLA_V].set(wkv[:, :, MLA_NOPE:]).reshape(KVR, H * LANE)
    pad_g = lambda g: jnp.zeros((1, LANE), F32).at[0, :MLA_QK].set(g)
    half = MLA_ROPE // 2
    inv_freq = np.power(np.float32(ROPE_THETA), -np.arange(half, dtype=np.float32) * np.float32(2.0) / np.float32(MLA_ROPE))
    freq = np.zeros((1, LANE), np.float32)
    freq[0, MLA_NOPE:MLA_NOPE + half] = inv_freq
    freq[0, MLA_NOPE + half:MLA_QK] = inv_freq
    const = lambda shape: pl.BlockSpec(shape, lambda b, s, p: (0,) * len(shape))
    row = lambda n: pl.BlockSpec((1, tm, n), lambda b, s, p: (b, s, 0))
    hout = pl.BlockSpec((1, H, tm, LANE), lambda b, s, p: (b, 0, s, 0))
    grid_spec = pltpu.PrefetchScalarGridSpec(
        num_scalar_prefetch=1, grid=(B, S // tm),
        in_specs=[row(QR), row(KVR), row(LANE), const((1, QR)), const((1, KVR)),
                  const((QR, H * LANE)), const((KVR, H * LANE)), const((KVR, H * LANE)),
                  const((1, LANE)), const((1, LANE)), const((1, LANE))],
        out_specs=[hout, hout, pl.BlockSpec((1, H, _VT_ROWS, tm), lambda b, s, p: (b, 0, 0, s))])
    return pl.pallas_call(
        _mla_prep_kernel,
        grid_spec=grid_spec,
        out_shape=[jax.ShapeDtypeStruct((B, H, S, LANE), BF16)] * 2 + [jax.ShapeDtypeStruct((B, H, _VT_ROWS, S), BF16)],
        compiler_params=_cp(("parallel", "parallel")),
        name="mla_prep",
    )(pos_offset, q_a, kv_a, kpe, q_norm_g.reshape(1, QR), kv_norm_g.reshape(1, KVR),
      wq.astype(BF16), wk.astype(BF16), wv.astype(BF16), pad_g(q_head_g), pad_g(k_head_g), jnp.asarray(freq))


def _attn_kernel(q_ref, k_ref, vt_ref, ng_ref, o_ref, m_ref, acc_ref):
    H = q_ref.shape[1]
    tq = q_ref.shape[2]
    qi = pl.program_id(1)
    ki = pl.program_id(2)

    @pl.when(ki == 0)
    def _():
        m_ref[...] = jnp.full_like(m_ref, -1e30)
        acc_ref[...] = jnp.zeros_like(acc_ref)

    def update(masked):
        if masked:
            kc = lax.broadcasted_iota(jnp.int32, (tq, tq), 0) // ATTN_CHUNK
            qc = lax.broadcasted_iota(jnp.int32, (tq, tq), 1) // ATTN_CHUNK
            visible = kc <= qc
        for h in range(H):
            st = _dot_nt(k_ref[0, h], q_ref[0, h])
            if masked:
                st = jnp.where(visible, st, -1e30)
            m_old = m_ref[h]
            m_new = jnp.maximum(m_old, jnp.max(st, axis=0, keepdims=True))
            alpha = jnp.exp2(m_old - m_new)
            p = jnp.exp2(st - m_new).astype(BF16)
            acc_ref[h] = alpha * acc_ref[h] + _dot(vt_ref[0, h], p)
            m_ref[h] = m_new

    @pl.when(ki < qi)
    def _():
        update(False)

    @pl.when(ki == qi)
    def _():
        update(True)
        ot = jnp.concatenate([acc_ref[h, :MLA_V] / acc_ref[h, MLA_V:MLA_V + 1] for h in range(H)], axis=0)
        o_ref[0] = _rms(ot.T, ng_ref[...]).astype(o_ref.dtype)


def _attention(q, k, vt, norm_g, tq):
    B, H, S, _ = q.shape
    W = H * MLA_V
    nq = S // tq
    qspec = pl.BlockSpec((1, H, tq, LANE), lambda b, i, j: (b, 0, i, 0))
    kspec = pl.BlockSpec((1, H, tq, LANE), lambda b, i, j: (b, 0, jnp.minimum(i, j), 0))
    vspec = pl.BlockSpec((1, H, _VT_ROWS, tq), lambda b, i, j: (b, 0, 0, jnp.minimum(i, j)))
    return pl.pallas_call(
        _attn_kernel,
        grid=(B, nq, nq),
        in_specs=[qspec, kspec, vspec, pl.BlockSpec((1, W), lambda b, i, j: (0, 0))],
        out_specs=pl.BlockSpec((1, tq, W), lambda b, i, j: (b, i, 0)),
        out_shape=jax.ShapeDtypeStruct((B, S, W), BF16),
        scratch_shapes=[pltpu.VMEM((H, 1, tq), F32), pltpu.VMEM((H, _VT_ROWS, tq), F32)],
        compiler_params=_cp(("parallel", "parallel", "arbitrary")),
        name="mla_attn",
    )(q, k, vt, norm_g.reshape(1, W))


def _store_token_tiles(ref, val):
    n, d = val.shape
    for s in range(d // LANE):
        ref[pl.ds(s, n, stride=SUBLANE), :] = val[:, s * LANE:(s + 1) * LANE]


def _load_token_tiles(ref, n):
    return jnp.concatenate([ref[pl.ds(s, n, stride=SUBLANE), :] for s in range(SUBLANE)], axis=1)


def _outproj_kernel(x_ref, ya_ref, yb_ref, yc_ref, yd_ref, wo_ref, gt_ref, sc_ref, sh_ref, g_ref,
                    wr_ref, br_ref, xo_ref, h_ref, lg_ref):
    acc = _dot(ya_ref[0], wo_ref[0])
    acc = acc + _dot(yb_ref[0], wo_ref[1])
    acc = acc + _dot(yc_ref[0], wo_ref[2])
    acc = acc + _dot(yd_ref[0], wo_ref[3])
    x = x_ref[0] + gt_ref[0] * acc
    xo_ref[0] = x
    h = _rms(x, g_ref[...]) * (1.0 + sc_ref[0]) + sh_ref[0]
    _store_token_tiles(h_ref.at[0], h)
    h_hi, h_lo = _split_bf16(h, 2)
    lg_ref[0] = _dot(h_hi, wr_ref[0]) + (_dot(h_lo, wr_ref[0]) + _dot(h_hi, wr_ref[1])) + br_ref[...]


def _outproj(x, ya, yb, yc, yd, w_out, gt, sc, sh, g, w_router, b_router, tm):
    B, S, D = x.shape
    W = D // N_MIX
    row = lambda n: pl.BlockSpec((1, tm, n), lambda b, s: (b, s, 0))
    vec = pl.BlockSpec((1, 1, D), lambda b, s: (b, 0, 0))
    const = lambda shape: pl.BlockSpec(shape, lambda b, s: (0,) * len(shape))
    return pl.pallas_call(
        _outproj_kernel,
        grid=(B, S // tm),
        in_specs=[row(D), row(W), row(W), row(W), row(W), const((N_MIX, W, D)), vec, vec, vec,
                  const((1, D)), const((2, D, LANE)), const((1, LANE))],
        out_specs=[row(D), pl.BlockSpec((1, tm * SUBLANE, LANE), lambda b, s: (b, s, 0)), row(LANE)],
        out_shape=[jax.ShapeDtypeStruct((B, S, D), F32), jax.ShapeDtypeStruct((B, S * SUBLANE, LANE), F32),
                   jax.ShapeDtypeStruct((B, S, LANE), F32)],
        compiler_params=_cp(("parallel", "parallel")),
        name="outproj",
    )(x, ya, yb, yc, yd, w_out.reshape(N_MIX, W, D).astype(BF16), gt, sc, sh, g,
      jnp.stack(_split_bf16(w_router, 2)), b_router)


_MOE_PAD_ROWS = MOE_BLOCK // TOP_K


def _moe_kernel(be_ref, nused_ref, tok_ref, tokn_ref, dst_ref, w_ref, h_hbm, w1_ref, w3_ref, w2_ref, o_hbm,
                xbuf, ybuf, gsem, ssem):
    i = pl.program_id(0)
    R = MOE_BLOCK
    TR = SUBLANE
    nused = nused_ref[0]
    slot = lax.rem(i, 2)
    other = 1 - slot

    def gather_rows(idx_ref, s):
        def body(r, c):
            src = pl.multiple_of(idx_ref[0, 0, r], TR)
            pltpu.make_async_copy(h_hbm.at[pl.ds(src, TR)], xbuf.at[s, pl.ds(pl.multiple_of(r * TR, TR), TR)],
                                  gsem.at[s]).start()
            return c
        lax.fori_loop(0, R, body, 0, unroll=8)

    def scatter_rows(s):
        def body(r, c):
            dst = pl.multiple_of(dst_ref[0, 0, r], TR)
            pltpu.make_async_copy(ybuf.at[s, pl.ds(pl.multiple_of(r * TR, TR), TR)], o_hbm.at[pl.ds(dst, TR)],
                                  ssem.at[s]).start(priority=1)
            return c
        lax.fori_loop(0, R, body, 0, unroll=8)

    def wait_gather(s):
        pltpu.make_async_copy(h_hbm.at[pl.ds(0, R * TR)], xbuf.at[s], gsem.at[s]).wait()

    def wait_scatter(s):
        pltpu.make_async_copy(ybuf.at[s], o_hbm.at[pl.ds(0, R * TR)], ssem.at[s]).wait()

    @pl.when(i == 0)
    def _():
        plane = o_hbm.shape[0] // TOP_K
        npad = _MOE_PAD_ROWS * TR
        ybuf[1] = jnp.zeros(ybuf.shape[1:], ybuf.dtype)
        for j in range(TOP_K):
            pad = pltpu.make_async_copy(ybuf.at[1, pl.ds(0, npad)],
                                        o_hbm.at[pl.ds((j + 1) * plane - npad, npad)], ssem.at[1])
            pad.start()
            pad.wait()
        gather_rows(tok_ref, 0)

    @pl.when(i < nused)
    def _():
        @pl.when(i + 1 < nused)
        def _():
            gather_rows(tokn_ref, other)

        wait_gather(slot)

        @pl.when(i >= 2)
        def _():
            wait_scatter(slot)

        xb = _load_token_tiles(xbuf.at[slot], R).astype(BF16)
        hid = _dot(xb, w1_ref[0])
        hid = hid * jax.nn.sigmoid(hid) * _dot(xb, w3_ref[0])
        y = _dot(hid.astype(BF16), w2_ref[0])
        _store_token_tiles(ybuf.at[slot], y * w_ref[:, 0:1])
        scatter_rows(slot)

        @pl.when(i == nused - 1)
        def _():
            wait_scatter(slot)

            @pl.when(i >= 1)
            def _():
                wait_scatter(other)


def _moe_experts(h2, blk_expert, n_used, row_tok, row_dst, row_w, w1, w3, w2):
    E, D, F = w1.shape
    T = h2.shape[0] // SUBLANE
    nb = blk_expert.shape[0]
    R = MOE_BLOCK
    blk = lambda i, be, nu: (i, 0, 0)
    nxt = lambda i, be, nu: (jnp.minimum(i + 1, nb - 1), 0, 0)
    wsel = lambda i, be, nu: (be[i], 0, 0)
    plane = (T + _MOE_PAD_ROWS) * SUBLANE
    row_tok = (row_tok * SUBLANE).reshape(nb, 1, R)
    row_dst = ((row_dst % TOP_K) * plane + (row_dst // TOP_K) * SUBLANE).reshape(nb, 1, R)
    grid_spec = pltpu.PrefetchScalarGridSpec(
        num_scalar_prefetch=2, grid=(nb,),
        in_specs=[pl.BlockSpec((1, 1, R), blk, memory_space=pltpu.SMEM),
                  pl.BlockSpec((1, 1, R), nxt, memory_space=pltpu.SMEM),
                  pl.BlockSpec((1, 1, R), blk, memory_space=pltpu.SMEM),
                  pl.BlockSpec((R, LANE), lambda i, be, nu: (i, 0)),
                  pl.BlockSpec(memory_space=pl.ANY),
                  pl.BlockSpec((1, D, F), wsel), pl.BlockSpec((1, D, F), wsel), pl.BlockSpec((1, F, D), wsel)],
        out_specs=pl.BlockSpec(memory_space=pl.ANY),
        scratch_shapes=[pltpu.VMEM((2, R * SUBLANE, LANE), F32), pltpu.VMEM((2, R * SUBLANE, LANE), F32),
                        pltpu.SemaphoreType.DMA((2,)), pltpu.SemaphoreType.DMA((2,))])
    return pl.pallas_call(
        _moe_kernel,
        grid_spec=grid_spec,
        out_shape=jax.ShapeDtypeStruct((TOP_K * plane, LANE), F32),
        compiler_params=_cp(("arbitrary",)),
        name="moe_experts",
    )(blk_expert, n_used, row_tok, row_tok, row_dst, jnp.broadcast_to(row_w.reshape(nb * R, 1), (nb * R, LANE)),
      h2, w1, w3, w2)


def _route(logits, T):
    p_group = jax.nn.softmax(logits[:, :N_GROUPS], axis=-1)
    g_sel = jnp.argmax(p_group, axis=-1).astype(jnp.int32)
    g_prob = jnp.max(p_group, axis=-1)
    logit_e = logits[:, N_GROUPS:N_GROUPS + N_EXPERTS].reshape(T, N_GROUPS, EXPERTS_PER_GROUP)
    logit_e = jnp.take_along_axis(logit_e, g_sel[:, None, None], axis=1)[:, 0]
    p_e = jax.nn.softmax(logit_e, axis=-1)
    top_p, top_i = lax.top_k(p_e, TOP_K)
    gate = (g_prob[:, None] * top_p / jnp.sum(top_p, axis=-1, keepdims=True)).reshape(-1)
    expert = (g_sel[:, None] * EXPERTS_PER_GROUP + top_i.astype(jnp.int32)).reshape(-1)
    n_assign = T * TOP_K
    n_blocks = -(-n_assign // MOE_BLOCK) + N_EXPERTS
    order = jnp.argsort(expert).astype(jnp.int32)
    counts = jnp.sum((expert[:, None] == jnp.arange(N_EXPERTS, dtype=jnp.int32)[None, :]).astype(jnp.int32), axis=0)
    start = jnp.cumsum(counts) - counts
    pcounts = (counts + MOE_BLOCK - 1) // MOE_BLOCK * MOE_BLOCK
    pend = jnp.cumsum(pcounts)
    pstart = pend - pcounts
    blk_start = jnp.arange(n_blocks, dtype=jnp.int32) * MOE_BLOCK
    blk_expert = jnp.minimum(jnp.sum((pend[None, :] <= blk_start[:, None]).astype(jnp.int32), axis=1), N_EXPERTS - 1)
    n_used = (pend[-1] // MOE_BLOCK).astype(jnp.int32).reshape(1)
    lane = jnp.arange(MOE_BLOCK, dtype=jnp.int32)[None, :]
    off = blk_start[:, None] + lane - pstart[blk_expert][:, None]
    valid = off < counts[blk_expert][:, None]
    pos = jnp.clip(start[blk_expert][:, None] + off, 0, n_assign - 1)
    asg = order[pos]
    row_tok = jnp.where(valid, asg // TOP_K, 0)
    row_dst = jnp.where(valid, asg, n_assign + lane)
    row_w = jnp.where(valid, gate[asg], 0.0)
    return blk_expert.astype(jnp.int32), n_used, row_tok, row_dst, row_w


def _combine_kernel(x_ref, y0_ref, y1_ref, gt_ref, o_ref):
    tm = x_ref.shape[1]
    y = _load_token_tiles(y0_ref.at[0], tm) + _load_token_tiles(y1_ref.at[0], tm)
    o_ref[0] = x_ref[0] + gt_ref[0] * y


def _combine(x, y2, gt, tm):
    B, S, D = x.shape
    nt = S // tm
    return pl.pallas_call(
        _combine_kernel,
        grid=(B, nt),
        in_specs=[pl.BlockSpec((1, tm, D), lambda b, s: (b, s, 0)),
                  pl.BlockSpec((1, tm * SUBLANE, LANE), lambda b, s: (0, b * nt + s, 0)),
                  pl.BlockSpec((1, tm * SUBLANE, LANE), lambda b, s: (1, b * nt + s, 0)),
                  pl.BlockSpec((1, 1, D), lambda b, s: (b, 0, 0))],
        out_specs=pl.BlockSpec((1, tm, D), lambda b, s: (b, s, 0)),
        out_shape=jax.ShapeDtypeStruct((B, S, D), F32),
        compiler_params=_cp(("parallel", "parallel")),
        name="moe_combine",
    )(x, y2, y2, gt)


def _tiles(S):
    big = S >= 2048
    return dict(tm=512 if big else 256, tl=512 if big else 256, ts_s5=64, ts_lru=128 if big else 64,
                tq=1024 if big else 256)


def kernel(x, c, pos_offset, ada_w, ada_b, norm1_g, w_in, rwkv_mu, rwkv_w0, rwkv_w2, rwkv_a0, rwkv_a2, rwkv_g2, rwkv_k_k, rwkv_k_a, rwkv_r_k, rwkv_ln_w, rwkv_ln_b, s5_lambda_re, s5_lambda_im, s5_b_re, s5_b_im, s5_c_re, s5_c_im, s5_d, s5_log_dt, s5_glu_w, s5_glu_b, mla_q_norm_g, mla_w_uq, mla_kv_norm_g, mla_w_ukv, mla_q_head_g, mla_k_head_g, lru_conv_w, lru_conv_b, lru_w_a, lru_b_a, lru_w_x, lru_b_x, lru_lambda, branch_norm_g, w_out, norm2_g, moe_w_group, moe_b_group, moe_w_expert, moe_b_expert, moe_w1, moe_w3, moe_w2):
    B, S, D = x.shape
    depth = ada_w.shape[0]
    T = B * S
    W = D // N_MIX
    assert B == SUBLANE, "time-major scans put the batch on the 8 sublanes"
    assert D == SUBLANE * LANE, "token-tile layout: one (8, 128) tile per token row"
    tiles = _tiles(S)
    mod = _adaln_mod(c, ada_w, ada_b)

    def to_tm(t):
        return jnp.swapaxes(t, 0, 1).reshape(S * B, t.shape[-1])

    def from_tm(t):
        return jnp.swapaxes(t.reshape(S, B, t.shape[-1]), 0, 1)

    for l in range(depth):
        sh1, sc1, gt1, sh2, sc2, gt2 = [mod[l, :, None, i * D:(i + 1) * D] for i in range(6)]
        z_rwkv, z_s5, q_a, kv_a, kpe, z_lru, z_gate = _inproj(
            x, sc1, sh1, norm1_g[l].reshape(1, D), _pad_w_in(w_in[l]), tiles['tm'])
        y_a = _rwkv(z_rwkv, rwkv_mu[l], rwkv_w0[l], rwkv_w2[l], rwkv_a0[l], rwkv_a2[l], rwkv_g2[l],
                    rwkv_k_k[l], rwkv_k_a[l], rwkv_r_k[l].reshape(-1), rwkv_ln_w[l], rwkv_ln_b[l], tiles['tl'])
        y_b = from_tm(_s5(to_tm(z_s5), s5_lambda_re[l], s5_lambda_im[l], s5_b_re[l], s5_b_im[l], s5_c_re[l],
                          s5_c_im[l], s5_d[l], s5_log_dt[l], s5_glu_w[l], s5_glu_b[l], branch_norm_g[l, 0],
                          tiles['ts_s5']))
        q, k, vt = _mla_prep(q_a, kv_a, kpe, pos_offset, mla_q_norm_g[l], mla_w_uq[l], mla_kv_norm_g[l],
                             mla_w_ukv[l], mla_q_head_g[l], mla_k_head_g[l], tiles['tm'])
        y_c = _attention(q, k, vt, branch_norm_g[l, 1], tiles['tq'])
        y_d = from_tm(_lru(to_tm(z_lru), to_tm(z_gate), lru_conv_w[l], lru_conv_b[l], lru_w_a[l], lru_b_a[l],
                           lru_w_x[l], lru_b_x[l], lru_lambda[l], branch_norm_g[l, 2], tiles['ts_lru']))
        w_router = jnp.zeros((D, LANE), F32).at[:, :N_GROUPS].set(moe_w_group[l])
        w_router = w_router.at[:, N_GROUPS:N_GROUPS + N_EXPERTS].set(moe_w_expert[l])
        b_router = jnp.zeros((1, LANE), F32).at[0, :N_GROUPS].set(moe_b_group[l])
        b_router = b_router.at[0, N_GROUPS:N_GROUPS + N_EXPERTS].set(moe_b_expert[l])
        x, h2, logits = _outproj(x, y_a, y_b, y_c, y_d, w_out[l], gt1, sc2, sh2, norm2_g[l].reshape(1, D),
                                 w_router, b_router, tiles['tm'])
        blk_expert, n_used, row_tok, row_dst, row_w = _route(logits.reshape(T, LANE), T)
        y2 = _moe_experts(h2.reshape(T * SUBLANE, LANE), blk_expert, n_used, row_tok, row_dst, row_w,
                          moe_w1[l].astype(BF16), moe_w3[l].astype(BF16), moe_w2[l].astype(BF16))
        x = _combine(x, y2.reshape(TOP_K, -1, LANE), gt2, tiles['tm'])
    return x
```

```python
import functools
import math

import numpy as np
import jax
import jax.numpy as jnp
from jax import lax
from jax.experimental import pallas as pl
from jax.experimental.pallas import tpu as pltpu

F32 = jnp.float32
BF16 = jnp.bfloat16
HI = lax.Precision.HIGHEST

N_MIX = 4
RWKV_HEAD = 64
RWKV_W_RANK, RWKV_A_RANK, RWKV_G_RANK = 32, 32, 64
RWKV_LN_EPS = 64e-5
S5_GROUP, S5_STATE = 16, 64
MLA_HEADS, MLA_NOPE, MLA_ROPE, MLA_V = 4, 64, 32, 64
MLA_QK = MLA_NOPE + MLA_ROPE
ROPE_THETA = 10000.0
LRU_C = 8.0
CONV_WIDTH = 4
ATTN_CHUNK = 64
N_GROUPS, EXPERTS_PER_GROUP, TOP_K = 4, 8, 2
N_EXPERTS = N_GROUPS * EXPERTS_PER_GROUP
MOE_BLOCK = 256
NORM_EPS = 1e-6

LANE = 128
SUBLANE = 8
RWKV_CHUNK = 64
VMEM_LIMIT = 48 * 1024 * 1024


def _cp(sem, vmem=VMEM_LIMIT):
    return pltpu.CompilerParams(dimension_semantics=sem, vmem_limit_bytes=vmem)


def _dot(a, b):
    return jnp.dot(a, b, preferred_element_type=F32)


def _dot_hi(a, b):
    return jnp.dot(a, b, precision=HI, preferred_element_type=F32)


def _dot_nt(a, b):
    return lax.dot_general(a, b, (((1,), (1,)), ((), ())), preferred_element_type=F32)


def _dot_nt_hi(a, b):
    return lax.dot_general(a, b, (((1,), (1,)), ((), ())), precision=HI, preferred_element_type=F32)


def _softplus(x):
    return jnp.maximum(x, 0.0) + jnp.log1p(jnp.exp(-jnp.abs(x)))


def _rms(x, g, eps=NORM_EPS):
    return x * lax.rsqrt(jnp.mean(x * x, axis=-1, keepdims=True) + eps) * g


def _mod_kernel(c_ref, w_ref, b_ref, o_ref):
    c = c_ref[...]
    cond = c * jax.nn.sigmoid(c)
    o_ref[0] = _dot_hi(cond, w_ref[0]) + b_ref[0]


def _adaln_mod(c, ada_w, ada_b):
    L, D, D6 = ada_w.shape
    B = c.shape[0]
    nj = D6 // D
    return pl.pallas_call(
        _mod_kernel,
        grid=(L, nj),
        in_specs=[pl.BlockSpec((B, D), lambda l, j: (0, 0)),
                  pl.BlockSpec((1, D, D), lambda l, j: (l, 0, j)),
                  pl.BlockSpec((1, 1, D), lambda l, j: (l, 0, j))],
        out_specs=pl.BlockSpec((1, B, D), lambda l, j: (l, 0, j)),
        out_shape=jax.ShapeDtypeStruct((L, B, D6), F32),
        compiler_params=_cp(("parallel", "parallel")),
        name="adaln_mod",
    )(c, ada_w, ada_b.reshape(L, 1, D6))


_RWKV_IN = 3 * 256 + RWKV_W_RANK + RWKV_A_RANK + RWKV_G_RANK
_IN_COLS = (_RWKV_IN, 256, 256, 128, LANE, 256, 256)
_IN_OFFS = tuple(int(v) for v in np.cumsum((0,) + _IN_COLS))
_KPE_LANE0 = MLA_NOPE


def _pad_w_in(w_in):
    D = w_in.shape[0]
    o = np.cumsum((0, _RWKV_IN, 256, 256, 128, MLA_ROPE, 256, 256))
    pieces = [w_in[:, o[i]:o[i + 1]] for i in range(7)]
    kpe = jnp.zeros((D, LANE), w_in.dtype).at[:, _KPE_LANE0:_KPE_LANE0 + MLA_ROPE].set(pieces[4])
    pieces[4] = kpe
    return jnp.concatenate(pieces, axis=1).astype(BF16)


def _inproj_kernel(x_ref, sc_ref, sh_ref, g_ref, w_ref, *outs):
    x = x_ref[0]
    h = _rms(x, g_ref[...]) * (1.0 + sc_ref[0]) + sh_ref[0]
    z = _dot(h.astype(BF16), w_ref[...])
    for i, o_ref in enumerate(outs):
        o_ref[0] = z[:, _IN_OFFS[i]:_IN_OFFS[i + 1]]


def _inproj(x, sc, sh, g, w_pad, tm):
    B, S, D = x.shape
    row = lambda b, s: (b, s, 0)
    vec = lambda b, s: (b, 0, 0)
    return pl.pallas_call(
        _inproj_kernel,
        grid=(B, S // tm),
        in_specs=[pl.BlockSpec((1, tm, D), row),
                  pl.BlockSpec((1, 1, D), vec),
                  pl.BlockSpec((1, 1, D), vec),
                  pl.BlockSpec((1, D), lambda b, s: (0, 0)),
                  pl.BlockSpec(w_pad.shape, lambda b, s: (0, 0))],
        out_specs=[pl.BlockSpec((1, tm, n), row) for n in _IN_COLS],
        out_shape=[jax.ShapeDtypeStruct((B, S, n), F32) for n in _IN_COLS],
        compiler_params=_cp(("parallel", "parallel")),
        name="inproj",
    )(x, sc, sh, g, w_pad)


def _split_bf16(x, parts):
    out = []
    for _ in range(parts - 1):
        hi = x.astype(BF16)
        out.append(hi)
        x = x - hi.astype(F32)
    out.append(x.astype(BF16))
    return out


def _rwkv_kernel(z_ref, mu_ref, w0_ref, a0_ref, wl_ref, kk_ref, ka_ref, rk_ref, lnw_ref, lnb_ref,
                 hsum_ref, tri_ref, o_ref, zprev_ref, g_ref, y_ref):
    W = 256
    NH = W // RWKV_HEAD
    L = RWKV_CHUNK
    TL = z_ref.shape[1]

    @pl.when(pl.program_id(1) == 0)
    def _():
        zprev_ref[...] = jnp.zeros_like(zprev_ref)
        g_ref[...] = jnp.zeros_like(g_ref)

    hsum = hsum_ref[...]

    def head_sum(t):
        hi, lo_ = _split_bf16(t, 2)
        return _dot(hi, hsum) + _dot(lo_, hsum)

    z = z_ref[0]
    rows = lax.broadcasted_iota(jnp.int32, z.shape, 0)
    zp = jnp.where(rows == 0, zprev_ref[...], pltpu.roll(z, 1, axis=0))
    zprev_ref[...] = z[TL - 1:TL, :]
    zs = z + (zp - z) * mu_ref[...]
    r, k, v, lo = zs[:, 0:W], zs[:, W:2 * W], zs[:, 2 * W:3 * W], zs[:, 3 * W:3 * W + LANE]
    lane = lax.broadcasted_iota(jnp.int32, lo.shape, 1)
    act = jnp.where(lane < RWKV_W_RANK, jnp.tanh(lo),
                    jnp.where(lane < RWKV_W_RANK + RWKV_A_RANK, lo, jax.nn.sigmoid(lo)))
    up = _dot(act.astype(BF16), wl_ref[...])
    w = -_softplus(-(w0_ref[...] + up[:, 0:W])) - 0.5
    ld = -jnp.exp(w)
    a = jax.nn.sigmoid(a0_ref[...] + up[:, W:2 * W])
    g = up[:, 2 * W:3 * W]
    kk = k * kk_ref[...]
    kk = kk / jnp.maximum(jnp.sqrt(head_sum(kk * kk)), 1e-12)
    k2 = k * (1.0 + (a - 1.0) * ka_ref[...])
    kka = kk * a

    tri_b = tri_ref[0]
    tri_incl = tri_b > 0
    tri_strict = tri_ref[1] > 0
    eye = (lax.broadcasted_iota(jnp.int32, (RWKV_HEAD, RWKV_HEAD), 0)
           == lax.broadcasted_iota(jnp.int32, (RWKV_HEAD, RWKV_HEAD), 1)).astype(BF16)

    NC = TL // L
    pairs = [(j, h) for j in range(NC) for h in range(NH)]
    hsl = [slice(h * RWKV_HEAD, (h + 1) * RWKV_HEAD) for h in range(NH)]
    csl = [slice(j * L, (j + 1) * L) for j in range(NC)]
    bt, rt, a_s, k_s, a_e, k_e, v_c, p_l = [], [], [], [], [], [], [], []
    for j in range(NC):
        ld_c = ld[csl[j]]
        cs = sum(_dot(tri_b, p) for p in _split_bf16(ld_c, 3))
        cs_l = cs[L - 1:L, :]
        p_inv = jnp.exp(-cs)
        p_end = jnp.exp(cs_l - cs)
        p_l.append(jnp.exp(cs_l))
        bt.append(-kk[csl[j]] * jnp.exp(cs - ld_c))
        rt.append(r[csl[j]] * jnp.exp(cs))
        a_s.append((kka[csl[j]] * p_inv).astype(BF16))
        k_s.append((k2[csl[j]] * p_inv).astype(BF16))
        a_e.append((kka[csl[j]] * p_end).astype(BF16))
        k_e.append((k2[csl[j]] * p_end).astype(BF16))
        v_c.append(v[csl[j]].astype(BF16))

    x = [jnp.concatenate([bt[j][:, hsl[h]], rt[j][:, hsl[h]]], axis=0).astype(BF16) for j, h in pairs]
    ga = [_dot_nt(x[i], a_s[j][:, hsl[h]]) for i, (j, h) in enumerate(pairs)]
    gk = [_dot_nt(x[i], k_s[j][:, hsl[h]]) for i, (j, h) in enumerate(pairs)]
    m_ba = [jnp.where(tri_strict, t[:L], 0.0).astype(BF16) for t in ga]
    m_ra = [jnp.where(tri_incl, t[L:], 0.0).astype(BF16) for t in ga]
    m_bk = [jnp.where(tri_strict, t[:L], 0.0).astype(BF16) for t in gk]
    m_rk = [jnp.where(tri_incl, t[L:], 0.0).astype(BF16) for t in gk]
    v_h = [v_c[j][:, hsl[h]] for j, h in pairs]
    wn = [jnp.concatenate([bt[j][:, hsl[h]], _dot(m_bk[i], v_h[i])], axis=1) for i, (j, h) in enumerate(pairs)]
    n_pow = m_ba
    wn = [w_ + _dot(n_, w_.astype(BF16)) for w_, n_ in zip(wn, n_pow)]
    for _ in range(5):
        n_pow = [_dot(n_, n_).astype(BF16) for n_ in n_pow]
        wn = [w_ + _dot(n_, w_.astype(BF16)) for w_, n_ in zip(wn, n_pow)]
    wn_b = [w_.astype(BF16) for w_ in wn]
    mw = [_dot(m_, w_) for m_, w_ in zip(m_ra, wn_b)]
    r2 = [(rt[j][:, hsl[h]] + mw[i][:, :RWKV_HEAD]).astype(BF16) for i, (j, h) in enumerate(pairs)]
    y0 = [mw[i][:, RWKV_HEAD:] + _dot(m_rk[i], v_h[i]) for i in range(len(pairs))]
    b2_t = [_dot_nt(eye, w_[:, :RWKV_HEAD]).astype(BF16) for w_ in wn_b]
    c_m = [_dot(b2_t[i], a_e[j][:, hsl[h]]).astype(BF16) for i, (j, h) in enumerate(pairs)]
    uv_t = [_dot_nt(eye, jnp.concatenate([wn_b[i][:, RWKV_HEAD:], v_h[i]], axis=0)).astype(BF16)
            for i in range(len(pairs))]
    d_t = [_dot(uv_t[i], jnp.concatenate([a_e[j][:, hsl[h]], k_e[j][:, hsl[h]]], axis=0))
           for i, (j, h) in enumerate(pairs)]
    for i, (j, h) in enumerate(pairs):
        g0 = g_ref[h]
        g_hi, g_lo = _split_bf16(g0, 2)
        y_ref[csl[j], hsl[h]] = _dot_nt(r2[i], g_hi) + _dot_nt(r2[i], g_lo) + y0[i]
        g_ref[h] = g0 * p_l[j][:, hsl[h]] + _dot(g_hi, c_m[i]) + _dot(g_lo, c_m[i]) + d_t[i]

    y = y_ref[...]
    inv_n = 1.0 / RWKV_HEAD
    mean = head_sum(y) * inv_n
    yc = y - mean
    var = head_sum(yc * yc) * inv_n
    yn = yc * lax.rsqrt(var + RWKV_LN_EPS) * lnw_ref[...] + lnb_ref[...]
    bonus = head_sum(r * k2 * rk_ref[...]) * v
    o_ref[0] = ((yn + bonus) * g).astype(o_ref.dtype)


def _rwkv(z, mu, w0, w2, a0, a2, g2, k_k, k_a, r_k, ln_w, ln_b, tl):
    B, S, _ = z.shape
    W = 256
    L = RWKV_CHUNK
    wl = jnp.zeros((LANE, 3 * W), F32)
    wl = wl.at[0:RWKV_W_RANK, 0:W].set(w2)
    wl = wl.at[RWKV_W_RANK:RWKV_W_RANK + RWKV_A_RANK, W:2 * W].set(a2)
    wl = wl.at[RWKV_W_RANK + RWKV_A_RANK:LANE, 2 * W:3 * W].set(g2).astype(BF16)
    hid = np.arange(W) // RWKV_HEAD
    hsum = jnp.asarray((hid[:, None] == hid[None, :]).astype(np.float32), dtype=BF16)
    t = np.arange(L)
    tri = jnp.asarray(np.stack([(t[None, :] <= t[:, None]), (t[None, :] < t[:, None])]).astype(np.float32), dtype=BF16)
    row2 = lambda v_: v_.reshape(1, -1)
    const = lambda shape: pl.BlockSpec(shape, lambda b, s: (0,) * len(shape))
    return pl.pallas_call(
        _rwkv_kernel,
        grid=(B, S // tl),
        in_specs=[pl.BlockSpec((1, tl, _RWKV_IN), lambda b, s: (b, s, 0)),
                  const((1, _RWKV_IN)), const((1, W)), const((1, W)), const((LANE, 3 * W)),
                  const((1, W)), const((1, W)), const((1, W)), const((1, W)), const((1, W)),
                  const((W, W)), const((2, L, L))],
        out_specs=pl.BlockSpec((1, tl, W), lambda b, s: (b, s, 0)),
        out_shape=jax.ShapeDtypeStruct((B, S, W), BF16),
        scratch_shapes=[pltpu.VMEM((1, _RWKV_IN), F32),
                        pltpu.VMEM((W // RWKV_HEAD, RWKV_HEAD, RWKV_HEAD), F32),
                        pltpu.VMEM((tl, W), F32)],
        compiler_params=_cp(("parallel", "arbitrary")),
        name="rwkv7",
    )(z, row2(mu), row2(w0), row2(a0), wl, row2(k_k), row2(k_a), row2(r_k), row2(ln_w), row2(ln_b), hsum, tri)


def _s5_kernel(u_ref, wb_ref, are_ref, aim_ref, wc_ref, d_ref, gw_ref, gb_ref, ng_ref, o_ref,
               hre_ref, him_ref, sre_ref, sim_ref):
    nb = SUBLANE
    rows = u_ref.shape[0]
    P = are_ref.shape[1]

    @pl.when(pl.program_id(0) == 0)
    def _():
        sre_ref[...] = jnp.zeros_like(sre_ref)
        sim_ref[...] = jnp.zeros_like(sim_ref)

    u = u_ref[...]
    bu = _dot(u.astype(BF16), wb_ref[...])
    hre_ref[...] = bu[:, :P]
    him_ref[...] = bu[:, P:]
    a_re = jnp.broadcast_to(are_ref[...], (nb, P))
    a_im = jnp.broadcast_to(aim_ref[...], (nb, P))

    def step(t, carry):
        h_re, h_im = carry
        i = pl.multiple_of(t * nb, nb)
        n_re = a_re * h_re - a_im * h_im + hre_ref[pl.ds(i, nb), :]
        n_im = a_re * h_im + a_im * h_re + him_ref[pl.ds(i, nb), :]
        hre_ref[pl.ds(i, nb), :] = n_re
        him_ref[pl.ds(i, nb), :] = n_im
        return n_re, n_im

    h_re, h_im = lax.fori_loop(0, rows // nb, step, (sre_ref[...], sim_ref[...]))
    sre_ref[...] = h_re
    sim_ref[...] = h_im
    y = _dot(hre_ref[...].astype(BF16), wc_ref[0]) + _dot(him_ref[...].astype(BF16), wc_ref[1])
    y = jax.nn.gelu(y + d_ref[...] * u)
    y = y * jax.nn.sigmoid(_dot(y.astype(BF16), gw_ref[...]) + gb_ref[...])
    o_ref[...] = _rms(y, ng_ref[...]).astype(o_ref.dtype)


def _s5(u_tm, lam_re, lam_im, b_re, b_im, c_re, c_im, d, log_dt, glu_w, glu_b, norm_g, ts):
    rows_total, W = u_tm.shape
    G, P = lam_re.shape
    dt = jnp.exp(log_dt)[:, None]
    mag = jnp.exp(lam_re * dt)
    a_re = mag * jnp.cos(lam_im * dt)
    a_im = mag * jnp.sin(lam_im * dt)
    den = lam_re * lam_re + lam_im * lam_im
    q_re = ((a_re - 1.0) * lam_re + a_im * lam_im) / den
    q_im = (a_im * lam_re - (a_re - 1.0) * lam_im) / den
    bb_re = q_re[..., None] * b_re - q_im[..., None] * b_im
    bb_im = q_re[..., None] * b_im + q_im[..., None] * b_re
    eye = jnp.eye(G, dtype=F32)
    wb_re = jnp.einsum('gpi,gh->gihp', bb_re, eye).reshape(W, G * P)
    wb_im = jnp.einsum('gpi,gh->gihp', bb_im, eye).reshape(W, G * P)
    wb = jnp.concatenate([wb_re, wb_im], axis=1).astype(BF16)
    wc_re = jnp.einsum('gip,gh->gphi', c_re, eye).reshape(G * P, W)
    wc_im = jnp.einsum('gip,gh->gphi', -c_im, eye).reshape(G * P, W)
    wc = jnp.stack([wc_re, wc_im]).astype(BF16)
    GP = G * P
    rows = ts * SUBLANE
    const = lambda shape: pl.BlockSpec(shape, lambda s: (0,) * len(shape))
    return pl.pallas_call(
        _s5_kernel,
        grid=(rows_total // rows,),
        in_specs=[pl.BlockSpec((rows, W), lambda s: (s, 0)),
                  const((W, 2 * GP)), const((1, GP)), const((1, GP)), const((2, GP, W)),
                  const((1, W)), const((W, W)), const((1, W)), const((1, W))],
        out_specs=pl.BlockSpec((rows, W), lambda s: (s, 0)),
        out_shape=jax.ShapeDtypeStruct((rows_total, W), BF16),
        scratch_shapes=[pltpu.VMEM((rows, GP), F32), pltpu.VMEM((rows, GP), F32),
                        pltpu.VMEM((SUBLANE, GP), F32), pltpu.VMEM((SUBLANE, GP), F32)],
        compiler_params=_cp(("arbitrary",)),
        name="s5",
    )(u_tm, wb, a_re.reshape(1, GP), a_im.reshape(1, GP), wc, d.reshape(1, W), glu_w.astype(BF16),
      glu_b.reshape(1, W), norm_g.reshape(1, W))


def _lru_kernel(x_ref, gate_ref, cw_ref, cb_ref, wax_ref, bax_ref, lam_ref, ng_ref, o_ref,
                xc_ref, a_ref, b_ref, hs_ref):
    nb = SUBLANE
    rows, W = x_ref.shape
    halo = (CONV_WIDTH - 1) * nb

    @pl.when(pl.program_id(0) == 0)
    def _():
        xc_ref[...] = jnp.zeros_like(xc_ref)
        hs_ref[...] = jnp.zeros_like(hs_ref)

    x = x_ref[...]
    xf = jnp.concatenate([xc_ref[...], x], axis=0)
    xc_ref[...] = x[rows - halo:, :]
    xc = cb_ref[...]
    for j in range(CONV_WIDTH):
        xc = xc + cw_ref[j:j + 1, :] * xf[j * nb:j * nb + rows, :]
    gates = _dot(xc.astype(BF16), wax_ref[...]) + bax_ref[...]
    r = jax.nn.sigmoid(gates[:, :W])
    i = jax.nn.sigmoid(gates[:, W:])
    log_a = -LRU_C * r * _softplus(-lam_ref[...])
    a = jnp.exp(log_a)
    a_ref[...] = a
    b_ref[...] = jnp.sqrt(1.0 - jnp.exp(2.0 * log_a)) * (i * xc)

    def step(t, h):
        i0 = pl.multiple_of(t * nb, nb)
        h = a_ref[pl.ds(i0, nb), :] * h + b_ref[pl.ds(i0, nb), :]
        b_ref[pl.ds(i0, nb), :] = h
        return h

    hs_ref[...] = lax.fori_loop(0, rows // nb, step, hs_ref[...])
    y = b_ref[...] * jax.nn.gelu(gate_ref[...])
    o_ref[...] = _rms(y, ng_ref[...]).astype(o_ref.dtype)


def _lru(x_tm, gate_tm, conv_w, conv_b, w_a, b_a, w_x, b_x, lam, norm_g, ts):
    rows_total, W = x_tm.shape
    nblk, bs, _ = w_a.shape
    eye = jnp.eye(nblk, dtype=F32)
    bd = lambda w_: jnp.einsum('nkj,nm->nkmj', w_, eye).reshape(W, W)
    wax = jnp.concatenate([bd(w_a), bd(w_x)], axis=1).astype(BF16)
    bax = jnp.concatenate([b_a, b_x]).reshape(1, 2 * W)
    rows = ts * SUBLANE
    const = lambda shape: pl.BlockSpec(shape, lambda s: (0,) * len(shape))
    return pl.pallas_call(
        _lru_kernel,
        grid=(rows_total // rows,),
        in_specs=[pl.BlockSpec((rows, W), lambda s: (s, 0)), pl.BlockSpec((rows, W), lambda s: (s, 0)),
                  const((CONV_WIDTH, W)), const((1, W)), const((W, 2 * W)), const((1, 2 * W)),
                  const((1, W)), const((1, W))],
        out_specs=pl.BlockSpec((rows, W), lambda s: (s, 0)),
        out_shape=jax.ShapeDtypeStruct((rows_total, W), BF16),
        scratch_shapes=[pltpu.VMEM(((CONV_WIDTH - 1) * SUBLANE, W), F32), pltpu.VMEM((rows, W), F32),
                        pltpu.VMEM((rows, W), F32), pltpu.VMEM((SUBLANE, W), F32)],
        compiler_params=_cp(("arbitrary",)),
        name="rglru",
    )(x_tm, gate_tm, conv_w, conv_b.reshape(1, W), wax, bax, lam.reshape(1, W), norm_g.reshape(1, W))


_VT_ROWS = MLA_V + 2 * SUBLANE


def _mla_prep_tile(pos0, q_a, kv_a, kpe, gq_ref, gkv_ref, wq_ref, wk_ref, wv_ref,
                   gqh_ref, gkh_ref, freq_ref, q_ref, k_ref, v_ref):
    tm = q_a.shape[0]
    pos = (pos0 + lax.broadcasted_iota(jnp.int32, (tm, LANE), 0)).astype(F32)
    ang = pos * freq_ref[...]
    cos_t = jnp.cos(ang)
    sin_t = jnp.sin(ang)
    lane = lax.broadcasted_iota(jnp.int32, (tm, LANE), 1)
    half = MLA_ROPE // 2
    first = (lane >= MLA_NOPE) & (lane < MLA_NOPE + half)
    second = (lane >= MLA_NOPE + half) & (lane < MLA_QK)

    def rope(t):
        rot = jnp.where(first, -pltpu.roll(t, LANE - half, axis=1),
                        jnp.where(second, pltpu.roll(t, half, axis=1), 0.0))
        return t * cos_t + rot * sin_t

    def head_norm(t, g):
        ms = jnp.sum(t * t, axis=-1, keepdims=True) * (1.0 / MLA_QK)
        return t * lax.rsqrt(ms + NORM_EPS) * g

    qn = _rms(q_a, gq_ref[...]).astype(BF16)
    kvn = _rms(kv_a, gkv_ref[...]).astype(BF16)
    q_all = _dot(qn, wq_ref[...])
    k_all = _dot(kvn, wk_ref[...])
    v_all = _dot(kvn, wv_ref[...])
    scale = MLA_QK ** -0.5 * math.log2(math.e)
    ones_row = lax.broadcasted_iota(jnp.int32, (_VT_ROWS, tm), 0) == MLA_V
    for h in range(MLA_HEADS):
        hs = slice(h * LANE, (h + 1) * LANE)
        q_ref[0, h] = (rope(head_norm(q_all[:, hs], gqh_ref[...])) * scale).astype(q_ref.dtype)
        k_ref[0, h] = rope(head_norm(k_all[:, hs] + kpe, gkh_ref[...])).astype(k_ref.dtype)
        v_ref[0, h] = jnp.where(ones_row, 1.0, v_all[:, hs].T[:_VT_ROWS]).astype(v_ref.dtype)


def _mla_prep_kernel(pos_ref, qa_ref, kva_ref, kpe_ref, *rest):
    pos0 = pos_ref[pl.program_id(0)] + pl.program_id(1) * qa_ref.shape[1]
    _mla_prep_tile(pos0, qa_ref[0], kva_ref[0], kpe_ref[0], *rest)


def _mla_prep(q_a, kv_a, kpe, pos_offset, q_norm_g, w_uq, kv_norm_g, w_ukv, q_head_g, k_head_g, tm):
    B, S, QR = q_a.shape
    KVR = kv_a.shape[2]
    H = MLA_HEADS
    wq = jnp.zeros((QR, H, LANE), F32).at[:, :, :MLA_QK].set(w_uq.reshape(QR, H, MLA_QK)).reshape(QR, H * LANE)
    wkv = w_ukv.reshape(KVR, H, MLA_NOPE + MLA_V)
    wk = jnp.zeros((KVR, H, LANE), F32).at[:, :, :MLA_NOPE].set(wkv[:, :, :MLA_NOPE]).reshape(KVR, H * LANE)
    wv = jnp.zeros((KVR, H, LANE), F32).at[:, :, :MLA_V].set(wkv[:, :, MLA_NOPE:]).reshape(KVR, H * LANE)
    pad_g = lambda g: jnp.zeros((1, LANE), F32).at[0, :MLA_QK].set(g)
    half = MLA_ROPE // 2
    inv_freq = np.power(np.float32(ROPE_THETA), -np.arange(half, dtype=np.float32) * np.float32(2.0) / np.float32(MLA_ROPE))
    freq = np.zeros((1, LANE), np.float32)
    freq[0, MLA_NOPE:MLA_NOPE + half] = inv_freq
    freq[0, MLA_NOPE + half:MLA_QK] = inv_freq
    const = lambda shape: pl.BlockSpec(shape, lambda b, s, p: (0,) * len(shape))
    row = lambda n: pl.BlockSpec((1, tm, n), lambda b, s, p: (b, s, 0))
    hout = pl.BlockSpec((1, H, tm, LANE), lambda b, s, p: (b, 0, s, 0))
    grid_spec = pltpu.PrefetchScalarGridSpec(
        num_scalar_prefetch=1, grid=(B, S // tm),
        in_specs=[row(QR), row(KVR), row(LANE), const((1, QR)), const((1, KVR)),
                  const((QR, H * LANE)), const((KVR, H * LANE)), const((KVR, H * LANE)),
                  const((1, LANE)), const((1, LANE)), const((1, LANE))],
        out_specs=[hout, hout, pl.BlockSpec((1, H, _VT_ROWS, tm), lambda b, s, p: (b, 0, 0, s))])
    return pl.pallas_call(
        _mla_prep_kernel,
        grid_spec=grid_spec,
        out_shape=[jax.ShapeDtypeStruct((B, H, S, LANE), BF16)] * 2 + [jax.ShapeDtypeStruct((B, H, _VT_ROWS, S), BF16)],
        compiler_params=_cp(("parallel", "parallel")),
        name="mla_prep",
    )(pos_offset, q_a, kv_a, kpe, q_norm_g.reshape(1, QR), kv_norm_g.reshape(1, KVR),
      wq.astype(BF16), wk.astype(BF16), wv.astype(BF16), pad_g(q_head_g), pad_g(k_head_g), jnp.asarray(freq))


def _attn_kernel(q_ref, k_ref, vt_ref, ng_ref, o_ref, m_ref, acc_ref):
    H = q_ref.shape[1]
    tq = q_ref.shape[2]
    qi = pl.program_id(1)
    ki = pl.program_id(2)

    @pl.when(ki == 0)
    def _():
        m_ref[...] = jnp.full_like(m_ref, -1e30)
        acc_ref[...] = jnp.zeros_like(acc_ref)

    def update(masked):
        if masked:
            kc = lax.broadcasted_iota(jnp.int32, (tq, tq), 0) // ATTN_CHUNK
            qc = lax.broadcasted_iota(jnp.int32, (tq, tq), 1) // ATTN_CHUNK
            visible = kc <= qc
        for h in range(H):
            st = _dot_nt(k_ref[0, h], q_ref[0, h])
            if masked:
                st = jnp.where(visible, st, -1e30)
            m_old = m_ref[h]
            m_new = jnp.maximum(m_old, jnp.max(st, axis=0, keepdims=True))
            alpha = jnp.exp2(m_old - m_new)
            p = jnp.exp2(st - m_new).astype(BF16)
            acc_ref[h] = alpha * acc_ref[h] + _dot(vt_ref[0, h], p)
            m_ref[h] = m_new

    @pl.when(ki < qi)
    def _():
        update(False)

    @pl.when(ki == qi)
    def _():
        update(True)
        ot = jnp.concatenate([acc_ref[h, :MLA_V] / acc_ref[h, MLA_V:MLA_V + 1] for h in range(H)], axis=0)
        o_ref[0] = _rms(ot.T, ng_ref[...]).astype(o_ref.dtype)


def _attention(q, k, vt, norm_g, tq):
    B, H, S, _ = q.shape
    W = H * MLA_V
    nq = S // tq
    qspec = pl.BlockSpec((1, H, tq, LANE), lambda b, i, j: (b, 0, i, 0))
    kspec = pl.BlockSpec((1, H, tq, LANE), lambda b, i, j: (b, 0, jnp.minimum(i, j), 0))
    vspec = pl.BlockSpec((1, H, _VT_ROWS, tq), lambda b, i, j: (b, 0, 0, jnp.minimum(i, j)))
    return pl.pallas_call(
        _attn_kernel,
        grid=(B, nq, nq),
        in_specs=[qspec, kspec, vspec, pl.BlockSpec((1, W), lambda b, i, j: (0, 0))],
        out_specs=pl.BlockSpec((1, tq, W), lambda b, i, j: (b, i, 0)),
        out_shape=jax.ShapeDtypeStruct((B, S, W), BF16),
        scratch_shapes=[pltpu.VMEM((H, 1, tq), F32), pltpu.VMEM((H, _VT_ROWS, tq), F32)],
        compiler_params=_cp(("parallel", "parallel", "arbitrary")),
        name="mla_attn",
    )(q, k, vt, norm_g.reshape(1, W))


def _store_token_tiles(ref, val):
    n, d = val.shape
    for s in range(d // LANE):
        ref[pl.ds(s, n, stride=SUBLANE), :] = val[:, s * LANE:(s + 1) * LANE]


def _load_token_tiles(ref, n):
    return jnp.concatenate([ref[pl.ds(s, n, stride=SUBLANE), :] for s in range(SUBLANE)], axis=1)


def _outproj_kernel(x_ref, ya_ref, yb_ref, yc_ref, yd_ref, wo_ref, gt_ref, sc_ref, sh_ref, g_ref,
                    wr_ref, br_ref, xo_ref, h_ref, lg_ref):
    acc = _dot(ya_ref[0], wo_ref[0])
    acc = acc + _dot(yb_ref[0], wo_ref[1])
    acc = acc + _dot(yc_ref[0], wo_ref[2])
    acc = acc + _dot(yd_ref[0], wo_ref[3])
    x = x_ref[0] + gt_ref[0] * acc
    xo_ref[0] = x
    h = _rms(x, g_ref[...]) * (1.0 + sc_ref[0]) + sh_ref[0]
    _store_token_tiles(h_ref.at[0], h)
    h_hi, h_lo = _split_bf16(h, 2)
    lg_ref[0] = _dot(h_hi, wr_ref[0]) + (_dot(h_lo, wr_ref[0]) + _dot(h_hi, wr_ref[1])) + br_ref[...]


def _outproj(x, ya, yb, yc, yd, w_out, gt, sc, sh, g, w_router, b_router, tm):
    B, S, D = x.shape
    W = D // N_MIX
    row = lambda n: pl.BlockSpec((1, tm, n), lambda b, s: (b, s, 0))
    vec = pl.BlockSpec((1, 1, D), lambda b, s: (b, 0, 0))
    const = lambda shape: pl.BlockSpec(shape, lambda b, s: (0,) * len(shape))
    return pl.pallas_call(
        _outproj_kernel,
        grid=(B, S // tm),
        in_specs=[row(D), row(W), row(W), row(W), row(W), const((N_MIX, W, D)), vec, vec, vec,
                  const((1, D)), const((2, D, LANE)), const((1, LANE))],
        out_specs=[row(D), pl.BlockSpec((1, tm * SUBLANE, LANE), lambda b, s: (b, s, 0)), row(LANE)],
        out_shape=[jax.ShapeDtypeStruct((B, S, D), F32), jax.ShapeDtypeStruct((B, S * SUBLANE, LANE), F32),
                   jax.ShapeDtypeStruct((B, S, LANE), F32)],
        compiler_params=_cp(("parallel", "parallel")),
        name="outproj",
    )(x, ya, yb, yc, yd, w_out.reshape(N_MIX, W, D).astype(BF16), gt, sc, sh, g,
      jnp.stack(_split_bf16(w_router, 2)), b_router)


_MOE_PAD_ROWS = MOE_BLOCK // TOP_K


def _moe_kernel(be_ref, nused_ref, tok_ref, tokn_ref, dst_ref, dstp_ref, w_ref, h_hbm, w1_ref, w3_ref, w2_ref, o_hbm,
                xbuf, ybuf, gsem, ssem):
    i = pl.program_id(0)
    R = MOE_BLOCK
    TR = SUBLANE
    nused = nused_ref[0]
    slot = lax.rem(i, 2)
    other = 1 - slot

    def gather_rows(idx_ref, s):
        def body(r, c):
            src = pl.multiple_of(idx_ref[0, 0, r], TR)
            pltpu.make_async_copy(h_hbm.at[pl.ds(src, TR)], xbuf.at[s, pl.ds(pl.multiple_of(r * TR, TR), TR)],
                                  gsem.at[s]).start()
            return c
        lax.fori_loop(0, R, body, 0, unroll=8)

    def scatter_rows(s):
        def body(r, c):
            dst = pl.multiple_of(dst_ref[0, 0, r], TR)
            pltpu.make_async_copy(ybuf.at[s, pl.ds(pl.multiple_of(r * TR, TR), TR)], o_hbm.at[pl.ds(dst, TR)],
                                  ssem.at[s]).start(priority=1)
            return c
        lax.fori_loop(0, R, body, 0, unroll=8)

    def wait_gather(s):
        pltpu.make_async_copy(h_hbm.at[pl.ds(0, R * TR)], xbuf.at[s], gsem.at[s]).wait()

    def wait_scatter(s):
        pltpu.make_async_copy(ybuf.at[s], o_hbm.at[pl.ds(0, R * TR)], ssem.at[s]).wait()

    @pl.when(i == 0)
    def _():
        ybuf[1] = jnp.zeros(ybuf.shape[1:], ybuf.dtype)
        gather_rows(tok_ref, 0)

    @pl.when(i < nused)
    def _():
        wait_gather(slot)

        @pl.when(i >= 1)
        def _():
            wait_scatter(slot)

        xb = _load_token_tiles(xbuf.at[slot], R).astype(BF16)
        for r in range(R):
            src = pl.multiple_of(tokn_ref[0, 0, r], TR)
            pltpu.make_async_copy(h_hbm.at[pl.ds(src, TR)], xbuf.at[other, pl.ds(r * TR, TR)], gsem.at[other]).start()
            dst = pl.multiple_of(dstp_ref[0, 0, r], TR)
            pltpu.make_async_copy(ybuf.at[other, pl.ds(r * TR, TR)], o_hbm.at[pl.ds(dst, TR)],
                                  ssem.at[other]).start(priority=1)
        hid = _dot(xb, w1_ref[0])
        hid = hid * jax.nn.sigmoid(hid) * _dot(xb, w3_ref[0])
        y = _dot(hid.astype(BF16), w2_ref[0])
        _store_token_tiles(ybuf.at[slot], y * w_ref[:, 0:1])

        @pl.when(i == nused - 1)
        def _():
            scatter_rows(slot)
            wait_gather(other)
            wait_scatter(other)
            wait_scatter(slot)


def _moe_experts(h2, blk_expert, n_used, row_tok, row_dst, row_w, w1, w3, w2):
    E, D, F = w1.shape
    T = h2.shape[0] // SUBLANE
    nb = blk_expert.shape[0]
    R = MOE_BLOCK
    blk = lambda i, be, nu: (i, 0, 0)
    nxt = lambda i, be, nu: (jnp.minimum(i + 1, nb - 1), 0, 0)
    wsel = lambda i, be, nu: (be[i], 0, 0)
    plane = (T + _MOE_PAD_ROWS) * SUBLANE
    row_tok = (row_tok * SUBLANE).reshape(nb, 1, R)
    tile_row = lambda a: (a % TOP_K) * plane + (a // TOP_K) * SUBLANE
    row_dst = tile_row(row_dst).reshape(nb, 1, R)
    pad_dst = tile_row(T * TOP_K + jnp.arange(R, dtype=jnp.int32)).reshape(1, 1, R)
    row_dst_prev = jnp.concatenate([pad_dst, row_dst[:-1]], axis=0)
    grid_spec = pltpu.PrefetchScalarGridSpec(
        num_scalar_prefetch=2, grid=(nb,),
        in_specs=[pl.BlockSpec((1, 1, R), blk, memory_space=pltpu.SMEM),
                  pl.BlockSpec((1, 1, R), nxt, memory_space=pltpu.SMEM),
                  pl.BlockSpec((1, 1, R), blk, memory_space=pltpu.SMEM),
                  pl.BlockSpec((1, 1, R), blk, memory_space=pltpu.SMEM),
                  pl.BlockSpec((R, LANE), lambda i, be, nu: (i, 0)),
                  pl.BlockSpec(memory_space=pl.ANY),
                  pl.BlockSpec((1, D, F), wsel), pl.BlockSpec((1, D, F), wsel), pl.BlockSpec((1, F, D), wsel)],
        out_specs=pl.BlockSpec(memory_space=pl.ANY),
        scratch_shapes=[pltpu.VMEM((2, R * SUBLANE, LANE), F32), pltpu.VMEM((2, R * SUBLANE, LANE), F32),
                        pltpu.SemaphoreType.DMA((2,)), pltpu.SemaphoreType.DMA((2,))])
    return pl.pallas_call(
        _moe_kernel,
        grid_spec=grid_spec,
        out_shape=jax.ShapeDtypeStruct((TOP_K * plane, LANE), F32),
        compiler_params=_cp(("arbitrary",)),
        name="moe_experts",
    )(blk_expert, n_used, row_tok, row_tok, row_dst, row_dst_prev,
      jnp.broadcast_to(row_w.reshape(nb * R, 1), (nb * R, LANE)), h2, w1, w3, w2)


def _route(logits, T):
    p_group = jax.nn.softmax(logits[:, :N_GROUPS], axis=-1)
    g_sel = jnp.argmax(p_group, axis=-1).astype(jnp.int32)
    g_prob = jnp.max(p_group, axis=-1)
    logit_e = logits[:, N_GROUPS:N_GROUPS + N_EXPERTS].reshape(T, N_GROUPS, EXPERTS_PER_GROUP)
    logit_e = jnp.take_along_axis(logit_e, g_sel[:, None, None], axis=1)[:, 0]
    p_e = jax.nn.softmax(logit_e, axis=-1)
    top_p, top_i = lax.top_k(p_e, TOP_K)
    gate = (g_prob[:, None] * top_p / jnp.sum(top_p, axis=-1, keepdims=True)).reshape(-1)
    expert = (g_sel[:, None] * EXPERTS_PER_GROUP + top_i.astype(jnp.int32)).reshape(-1)
    n_assign = T * TOP_K
    n_blocks = -(-n_assign // MOE_BLOCK) + N_EXPERTS
    order = jnp.argsort(expert).astype(jnp.int32)
    counts = jnp.sum((expert[:, None] == jnp.arange(N_EXPERTS, dtype=jnp.int32)[None, :]).astype(jnp.int32), axis=0)
    start = jnp.cumsum(counts) - counts
    pcounts = (counts + MOE_BLOCK - 1) // MOE_BLOCK * MOE_BLOCK
    pend = jnp.cumsum(pcounts)
    pstart = pend - pcounts
    blk_start = jnp.arange(n_blocks, dtype=jnp.int32) * MOE_BLOCK
    blk_expert = jnp.minimum(jnp.sum((pend[None, :] <= blk_start[:, None]).astype(jnp.int32), axis=1), N_EXPERTS - 1)
    n_used = (pend[-1] // MOE_BLOCK).astype(jnp.int32).reshape(1)
    lane = jnp.arange(MOE_BLOCK, dtype=jnp.int32)[None, :]
    off = blk_start[:, None] + lane - pstart[blk_expert][:, None]
    valid = off < counts[blk_expert][:, None]
    pos = jnp.clip(start[blk_expert][:, None] + off, 0, n_assign - 1)
    asg = order[pos]
    row_tok = jnp.where(valid, asg // TOP_K, 0)
    row_dst = jnp.where(valid, asg, n_assign + lane)
    row_w = jnp.where(valid, gate[asg], 0.0)
    return blk_expert.astype(jnp.int32), n_used, row_tok, row_dst, row_w


def _combine_kernel(x_ref, y0_ref, y1_ref, gt_ref, o_ref):
    tm = x_ref.shape[1]
    y = _load_token_tiles(y0_ref.at[0], tm) + _load_token_tiles(y1_ref.at[0], tm)
    o_ref[0] = x_ref[0] + gt_ref[0] * y


def _combine(x, y2, gt, tm):
    B, S, D = x.shape
    nt = S // tm
    return pl.pallas_call(
        _combine_kernel,
        grid=(B, nt),
        in_specs=[pl.BlockSpec((1, tm, D), lambda b, s: (b, s, 0)),
                  pl.BlockSpec((1, tm * SUBLANE, LANE), lambda b, s: (0, b * nt + s, 0)),
                  pl.BlockSpec((1, tm * SUBLANE, LANE), lambda b, s: (1, b * nt + s, 0)),
                  pl.BlockSpec((1, 1, D), lambda b, s: (b, 0, 0))],
        out_specs=pl.BlockSpec((1, tm, D), lambda b, s: (b, s, 0)),
        out_shape=jax.ShapeDtypeStruct((B, S, D), F32),
        compiler_params=_cp(("parallel", "parallel")),
        name="moe_combine",
    )(x, y2, y2, gt)


def _tiles(S):
    big = S >= 2048
    return dict(tm=512 if big else 256, tl=512 if big else 256, ts_s5=64, ts_lru=128 if big else 64,
                tq=1024 if big else 256)


def kernel(x, c, pos_offset, ada_w, ada_b, norm1_g, w_in, rwkv_mu, rwkv_w0, rwkv_w2, rwkv_a0, rwkv_a2, rwkv_g2, rwkv_k_k, rwkv_k_a, rwkv_r_k, rwkv_ln_w, rwkv_ln_b, s5_lambda_re, s5_lambda_im, s5_b_re, s5_b_im, s5_c_re, s5_c_im, s5_d, s5_log_dt, s5_glu_w, s5_glu_b, mla_q_norm_g, mla_w_uq, mla_kv_norm_g, mla_w_ukv, mla_q_head_g, mla_k_head_g, lru_conv_w, lru_conv_b, lru_w_a, lru_b_a, lru_w_x, lru_b_x, lru_lambda, branch_norm_g, w_out, norm2_g, moe_w_group, moe_b_group, moe_w_expert, moe_b_expert, moe_w1, moe_w3, moe_w2):
    B, S, D = x.shape
    depth = ada_w.shape[0]
    T = B * S
    W = D // N_MIX
    assert B == SUBLANE, "time-major scans put the batch on the 8 sublanes"
    assert D == SUBLANE * LANE, "token-tile layout: one (8, 128) tile per token row"
    tiles = _tiles(S)
    mod = _adaln_mod(c, ada_w, ada_b)

    def to_tm(t):
        return jnp.swapaxes(t, 0, 1).reshape(S * B, t.shape[-1])

    def from_tm(t):
        return jnp.swapaxes(t.reshape(S, B, t.shape[-1]), 0, 1)

    for l in range(depth):
        sh1, sc1, gt1, sh2, sc2, gt2 = [mod[l, :, None, i * D:(i + 1) * D] for i in range(6)]
        z_rwkv, z_s5, q_a, kv_a, kpe, z_lru, z_gate = _inproj(
            x, sc1, sh1, norm1_g[l].reshape(1, D), _pad_w_in(w_in[l]), tiles['tm'])
        y_a = _rwkv(z_rwkv, rwkv_mu[l], rwkv_w0[l], rwkv_w2[l], rwkv_a0[l], rwkv_a2[l], rwkv_g2[l],
                    rwkv_k_k[l], rwkv_k_a[l], rwkv_r_k[l].reshape(-1), rwkv_ln_w[l], rwkv_ln_b[l], tiles['tl'])
        y_b = from_tm(_s5(to_tm(z_s5), s5_lambda_re[l], s5_lambda_im[l], s5_b_re[l], s5_b_im[l], s5_c_re[l],
                          s5_c_im[l], s5_d[l], s5_log_dt[l], s5_glu_w[l], s5_glu_b[l], branch_norm_g[l, 0],
                          tiles['ts_s5']))
        q, k, vt = _mla_prep(q_a, kv_a, kpe, pos_offset, mla_q_norm_g[l], mla_w_uq[l], mla_kv_norm_g[l],
                             mla_w_ukv[l], mla_q_head_g[l], mla_k_head_g[l], tiles['tm'])
        y_c = _attention(q, k, vt, branch_norm_g[l, 1], tiles['tq'])
        y_d = from_tm(_lru(to_tm(z_lru), to_tm(z_gate), lru_conv_w[l], lru_conv_b[l], lru_w_a[l], lru_b_a[l],
                           lru_w_x[l], lru_b_x[l], lru_lambda[l], branch_norm_g[l, 2], tiles['ts_lru']))
        w_router = jnp.zeros((D, LANE), F32).at[:, :N_GROUPS].set(moe_w_group[l])
        w_router = w_router.at[:, N_GROUPS:N_GROUPS + N_EXPERTS].set(moe_w_expert[l])
        b_router = jnp.zeros((1, LANE), F32).at[0, :N_GROUPS].set(moe_b_group[l])
        b_router = b_router.at[0, N_GROUPS:N_GROUPS + N_EXPERTS].set(moe_b_expert[l])
        x, h2, logits = _outproj(x, y_a, y_b, y_c, y_d, w_out[l], gt1, sc2, sh2, norm2_g[l].reshape(1, D),
                                 w_router, b_router, tiles['tm'])
        blk_expert, n_used, row_tok, row_dst, row_w = _route(logits.reshape(T, LANE), T)
        y2 = _moe_experts(h2.reshape(T * SUBLANE, LANE), blk_expert, n_used, row_tok, row_dst, row_w,
                          moe_w1[l].astype(BF16), moe_w3[l].astype(BF16), moe_w2[l].astype(BF16))
        x = _combine(x, y2.reshape(TOP_K, -1, LANE), gt2, tiles['tm'])
    return x
```

```python
import functools
import math

import numpy as np
import jax
import jax.numpy as jnp
from jax import lax
from jax.experimental import pallas as pl
from jax.experimental.pallas import tpu as pltpu

F32 = jnp.float32
BF16 = jnp.bfloat16
HI = lax.Precision.HIGHEST

N_MIX = 4
RWKV_HEAD = 64
RWKV_W_RANK, RWKV_A_RANK, RWKV_G_RANK = 32, 32, 64
RWKV_LN_EPS = 64e-5
S5_GROUP, S5_STATE = 16, 64
MLA_HEADS, MLA_NOPE, MLA_ROPE, MLA_V = 4, 64, 32, 64
MLA_QK = MLA_NOPE + MLA_ROPE
ROPE_THETA = 10000.0
LRU_C = 8.0
CONV_WIDTH = 4
ATTN_CHUNK = 64
N_GROUPS, EXPERTS_PER_GROUP, TOP_K = 4, 8, 2
N_EXPERTS = N_GROUPS * EXPERTS_PER_GROUP
MOE_BLOCK = 256
NORM_EPS = 1e-6

LANE = 128
SUBLANE = 8
RWKV_CHUNK = 64
VMEM_LIMIT = 48 * 1024 * 1024


def _cp(sem, vmem=VMEM_LIMIT):
    return pltpu.CompilerParams(dimension_semantics=sem, vmem_limit_bytes=vmem)


def _dot(a, b):
    return jnp.dot(a, b, preferred_element_type=F32)


def _dot_hi(a, b):
    return jnp.dot(a, b, precision=HI, preferred_element_type=F32)


def _dot_nt(a, b):
    return lax.dot_general(a, b, (((1,), (1,)), ((), ())), preferred_element_type=F32)


def _dot_nt_hi(a, b):
    return lax.dot_general(a, b, (((1,), (1,)), ((), ())), precision=HI, preferred_element_type=F32)


def _softplus(x):
    return jnp.maximum(x, 0.0) + jnp.log1p(jnp.exp(-jnp.abs(x)))


def _rms(x, g, eps=NORM_EPS):
    return x * lax.rsqrt(jnp.mean(x * x, axis=-1, keepdims=True) + eps) * g


def _mod_kernel(c_ref, w_ref, b_ref, o_ref):
    c = c_ref[...]
    cond = c * jax.nn.sigmoid(c)
    o_ref[0] = _dot_hi(cond, w_ref[0]) + b_ref[0]


def _adaln_mod(c, ada_w, ada_b):
    L, D, D6 = ada_w.shape
    B = c.shape[0]
    nj = D6 // D
    return pl.pallas_call(
        _mod_kernel,
        grid=(L, nj),
        in_specs=[pl.BlockSpec((B, D), lambda l, j: (0, 0)),
                  pl.BlockSpec((1, D, D), lambda l, j: (l, 0, j)),
                  pl.BlockSpec((1, 1, D), lambda l, j: (l, 0, j))],
        out_specs=pl.BlockSpec((1, B, D), lambda l, j: (l, 0, j)),
        out_shape=jax.ShapeDtypeStruct((L, B, D6), F32),
        compiler_params=_cp(("parallel", "parallel")),
        name="adaln_mod",
    )(c, ada_w, ada_b.reshape(L, 1, D6))


_RWKV_IN = 3 * 256 + RWKV_W_RANK + RWKV_A_RANK + RWKV_G_RANK
_IN_COLS = (_RWKV_IN, 256, 256, 128, LANE, 256, 256)
_IN_OFFS = tuple(int(v) for v in np.cumsum((0,) + _IN_COLS))
_KPE_LANE0 = MLA_NOPE


def _pad_w_in(w_in):
    D = w_in.shape[0]
    o = np.cumsum((0, _RWKV_IN, 256, 256, 128, MLA_ROPE, 256, 256))
    pieces = [w_in[:, o[i]:o[i + 1]] for i in range(7)]
    kpe = jnp.zeros((D, LANE), w_in.dtype).at[:, _KPE_LANE0:_KPE_LANE0 + MLA_ROPE].set(pieces[4])
    pieces[4] = kpe
    return jnp.concatenate(pieces, axis=1).astype(BF16)


def _inproj_kernel(x_ref, sc_ref, sh_ref, g_ref, w_ref, *outs):
    x = x_ref[0]
    h = _rms(x, g_ref[...]) * (1.0 + sc_ref[0]) + sh_ref[0]
    z = _dot(h.astype(BF16), w_ref[...])
    for i, o_ref in enumerate(outs):
        o_ref[0] = z[:, _IN_OFFS[i]:_IN_OFFS[i + 1]]


def _inproj(x, sc, sh, g, w_pad, tm):
    B, S, D = x.shape
    row = lambda b, s: (b, s, 0)
    vec = lambda b, s: (b, 0, 0)
    return pl.pallas_call(
        _inproj_kernel,
        grid=(B, S // tm),
        in_specs=[pl.BlockSpec((1, tm, D), row),
                  pl.BlockSpec((1, 1, D), vec),
                  pl.BlockSpec((1, 1, D), vec),
                  pl.BlockSpec((1, D), lambda b, s: (0, 0)),
                  pl.BlockSpec(w_pad.shape, lambda b, s: (0, 0))],
        out_specs=[pl.BlockSpec((1, tm, n), row) for n in _IN_COLS],
        out_shape=[jax.ShapeDtypeStruct((B, S, n), F32) for n in _IN_COLS],
        compiler_params=_cp(("parallel", "parallel")),
        name="inproj",
    )(x, sc, sh, g, w_pad)


def _split_bf16(x, parts):
    out = []
    for _ in range(parts - 1):
        hi = x.astype(BF16)
        out.append(hi)
        x = x - hi.astype(F32)
    out.append(x.astype(BF16))
    return out


def _rwkv_kernel(z_ref, mu_ref, w0_ref, a0_ref, wl_ref, kk_ref, ka_ref, rk_ref, lnw_ref, lnb_ref,
                 hsum_ref, tri_ref, o_ref, zprev_ref, g_ref, y_ref):
    W = 256
    NH = W // RWKV_HEAD
    L = RWKV_CHUNK
    TL = z_ref.shape[1]

    @pl.when(pl.program_id(1) == 0)
    def _():
        zprev_ref[...] = jnp.zeros_like(zprev_ref)
        g_ref[...] = jnp.zeros_like(g_ref)

    hsum = hsum_ref[...]

    def head_sum(t):
        hi, lo_ = _split_bf16(t, 2)
        return _dot(hi, hsum) + _dot(lo_, hsum)

    z = z_ref[0]
    rows = lax.broadcasted_iota(jnp.int32, z.shape, 0)
    zp = jnp.where(rows == 0, zprev_ref[...], pltpu.roll(z, 1, axis=0))
    zprev_ref[...] = z[TL - 1:TL, :]
    zs = z + (zp - z) * mu_ref[...]
    r, k, v, lo = zs[:, 0:W], zs[:, W:2 * W], zs[:, 2 * W:3 * W], zs[:, 3 * W:3 * W + LANE]
    lane = lax.broadcasted_iota(jnp.int32, lo.shape, 1)
    act = jnp.where(lane < RWKV_W_RANK, jnp.tanh(lo),
                    jnp.where(lane < RWKV_W_RANK + RWKV_A_RANK, lo, jax.nn.sigmoid(lo)))
    up = _dot(act.astype(BF16), wl_ref[...])
    w = -_softplus(-(w0_ref[...] + up[:, 0:W])) - 0.5
    ld = -jnp.exp(w)
    a = jax.nn.sigmoid(a0_ref[...] + up[:, W:2 * W])
    g = up[:, 2 * W:3 * W]
    kk = k * kk_ref[...]
    kk = kk / jnp.maximum(jnp.sqrt(head_sum(kk * kk)), 1e-12)
    k2 = k * (1.0 + (a - 1.0) * ka_ref[...])
    kka = kk * a

    tri_b = tri_ref[0]
    tri_incl = tri_b > 0
    tri_strict = tri_ref[1] > 0
    eye = (lax.broadcasted_iota(jnp.int32, (RWKV_HEAD, RWKV_HEAD), 0)
           == lax.broadcasted_iota(jnp.int32, (RWKV_HEAD, RWKV_HEAD), 1)).astype(BF16)

    NC = TL // L
    pairs = [(j, h) for j in range(NC) for h in range(NH)]
    hsl = [slice(h * RWKV_HEAD, (h + 1) * RWKV_HEAD) for h in range(NH)]
    csl = [slice(j * L, (j + 1) * L) for j in range(NC)]
    bt, rt, a_s, k_s, a_e, k_e, v_c, p_l = [], [], [], [], [], [], [], []
    for j in range(NC):
        ld_c = ld[csl[j]]
        cs = sum(_dot(tri_b, p) for p in _split_bf16(ld_c, 3))
        cs_l = cs[L - 1:L, :]
        p_inv = jnp.exp(-cs)
        p_end = jnp.exp(cs_l - cs)
        p_l.append(jnp.exp(cs_l))
        bt.append(-kk[csl[j]] * jnp.exp(cs - ld_c))
        rt.append(r[csl[j]] * jnp.exp(cs))
        a_s.append((kka[csl[j]] * p_inv).astype(BF16))
        k_s.append((k2[csl[j]] * p_inv).astype(BF16))
        a_e.append((kka[csl[j]] * p_end).astype(BF16))
        k_e.append((k2[csl[j]] * p_end).astype(BF16))
        v_c.append(v[csl[j]].astype(BF16))

    x = [jnp.concatenate([bt[j][:, hsl[h]], rt[j][:, hsl[h]]], axis=0).astype(BF16) for j, h in pairs]
    ga = [_dot_nt(x[i], a_s[j][:, hsl[h]]) for i, (j, h) in enumerate(pairs)]
    gk = [_dot_nt(x[i], k_s[j][:, hsl[h]]) for i, (j, h) in enumerate(pairs)]
    m_ba = [jnp.where(tri_strict, t[:L], 0.0).astype(BF16) for t in ga]
    m_ra = [jnp.where(tri_incl, t[L:], 0.0).astype(BF16) for t in ga]
    m_bk = [jnp.where(tri_strict, t[:L], 0.0).astype(BF16) for t in gk]
    m_rk = [jnp.where(tri_incl, t[L:], 0.0).astype(BF16) for t in gk]
    v_h = [v_c[j][:, hsl[h]] for j, h in pairs]
    wn = [jnp.concatenate([bt[j][:, hsl[h]], _dot(m_bk[i], v_h[i])], axis=1) for i, (j, h) in enumerate(pairs)]
    n_pow = m_ba
    wn = [w_ + _dot(n_, w_.astype(BF16)) for w_, n_ in zip(wn, n_pow)]
    for _ in range(5):
        n_pow = [_dot(n_, n_).astype(BF16) for n_ in n_pow]
        wn = [w_ + _dot(n_, w_.astype(BF16)) for w_, n_ in zip(wn, n_pow)]
    wn_b = [w_.astype(BF16) for w_ in wn]
    mw = [_dot(m_, w_) for m_, w_ in zip(m_ra, wn_b)]
    r2 = [(rt[j][:, hsl[h]] + mw[i][:, :RWKV_HEAD]).astype(BF16) for i, (j, h) in enumerate(pairs)]
    y0 = [mw[i][:, RWKV_HEAD:] + _dot(m_rk[i], v_h[i]) for i in range(len(pairs))]
    b2_t = [_dot_nt(eye, w_[:, :RWKV_HEAD]).astype(BF16) for w_ in wn_b]
    c_m = [_dot(b2_t[i], a_e[j][:, hsl[h]]).astype(BF16) for i, (j, h) in enumerate(pairs)]
    uv_t = [_dot_nt(eye, jnp.concatenate([wn_b[i][:, RWKV_HEAD:], v_h[i]], axis=0)).astype(BF16)
            for i in range(len(pairs))]
    d_t = [_dot(uv_t[i], jnp.concatenate([a_e[j][:, hsl[h]], k_e[j][:, hsl[h]]], axis=0))
           for i, (j, h) in enumerate(pairs)]
    for i, (j, h) in enumerate(pairs):
        g0 = g_ref[h]
        g_hi, g_lo = _split_bf16(g0, 2)
        y_ref[csl[j], hsl[h]] = _dot_nt(r2[i], g_hi) + _dot_nt(r2[i], g_lo) + y0[i]
        g_ref[h] = g0 * p_l[j][:, hsl[h]] + _dot(g_hi, c_m[i]) + _dot(g_lo, c_m[i]) + d_t[i]

    y = y_ref[...]
    inv_n = 1.0 / RWKV_HEAD
    mean = head_sum(y) * inv_n
    yc = y - mean
    var = head_sum(yc * yc) * inv_n
    yn = yc * lax.rsqrt(var + RWKV_LN_EPS) * lnw_ref[...] + lnb_ref[...]
    bonus = head_sum(r * k2 * rk_ref[...]) * v
    o_ref[0] = ((yn + bonus) * g).astype(o_ref.dtype)


def _rwkv(z, mu, w0, w2, a0, a2, g2, k_k, k_a, r_k, ln_w, ln_b, tl):
    B, S, _ = z.shape
    W = 256
    L = RWKV_CHUNK
    wl = jnp.zeros((LANE, 3 * W), F32)
    wl = wl.at[0:RWKV_W_RANK, 0:W].set(w2)
    wl = wl.at[RWKV_W_RANK:RWKV_W_RANK + RWKV_A_RANK, W:2 * W].set(a2)
    wl = wl.at[RWKV_W_RANK + RWKV_A_RANK:LANE, 2 * W:3 * W].set(g2).astype(BF16)
    hid = np.arange(W) // RWKV_HEAD
    hsum = jnp.asarray((hid[:, None] == hid[None, :]).astype(np.float32), dtype=BF16)
    t = np.arange(L)
    tri = jnp.asarray(np.stack([(t[None, :] <= t[:, None]), (t[None, :] < t[:, None])]).astype(np.float32), dtype=BF16)
    row2 = lambda v_: v_.reshape(1, -1)
    const = lambda shape: pl.BlockSpec(shape, lambda b, s: (0,) * len(shape))
    return pl.pallas_call(
        _rwkv_kernel,
        grid=(B, S // tl),
        in_specs=[pl.BlockSpec((1, tl, _RWKV_IN), lambda b, s: (b, s, 0)),
                  const((1, _RWKV_IN)), const((1, W)), const((1, W)), const((LANE, 3 * W)),
                  const((1, W)), const((1, W)), const((1, W)), const((1, W)), const((1, W)),
                  const((W, W)), const((2, L, L))],
        out_specs=pl.BlockSpec((1, tl, W), lambda b, s: (b, s, 0)),
        out_shape=jax.ShapeDtypeStruct((B, S, W), BF16),
        scratch_shapes=[pltpu.VMEM((1, _RWKV_IN), F32),
                        pltpu.VMEM((W // RWKV_HEAD, RWKV_HEAD, RWKV_HEAD), F32),
                        pltpu.VMEM((tl, W), F32)],
        compiler_params=_cp(("parallel", "arbitrary")),
        name="rwkv7",
    )(z, row2(mu), row2(w0), row2(a0), wl, row2(k_k), row2(k_a), row2(r_k), row2(ln_w), row2(ln_b), hsum, tri)


def _s5_kernel(u_ref, wb_ref, are_ref, aim_ref, wc_ref, d_ref, gw_ref, gb_ref, ng_ref, o_ref,
               hre_ref, him_ref, sre_ref, sim_ref):
    nb = SUBLANE
    rows = u_ref.shape[0]
    P = are_ref.shape[1]

    @pl.when(pl.program_id(0) == 0)
    def _():
        sre_ref[...] = jnp.zeros_like(sre_ref)
        sim_ref[...] = jnp.zeros_like(sim_ref)

    u = u_ref[...]
    bu = _dot(u.astype(BF16), wb_ref[...])
    hre_ref[...] = bu[:, :P]
    him_ref[...] = bu[:, P:]
    a_re = jnp.broadcast_to(are_ref[...], (nb, P))
    a_im = jnp.broadcast_to(aim_ref[...], (nb, P))

    def step(t, carry):
        h_re, h_im = carry
        i = pl.multiple_of(t * nb, nb)
        n_re = a_re * h_re - a_im * h_im + hre_ref[pl.ds(i, nb), :]
        n_im = a_re * h_im + a_im * h_re + him_ref[pl.ds(i, nb), :]
        hre_ref[pl.ds(i, nb), :] = n_re
        him_ref[pl.ds(i, nb), :] = n_im
        return n_re, n_im

    h_re, h_im = lax.fori_loop(0, rows // nb, step, (sre_ref[...], sim_ref[...]))
    sre_ref[...] = h_re
    sim_ref[...] = h_im
    y = _dot(hre_ref[...].astype(BF16), wc_ref[0]) + _dot(him_ref[...].astype(BF16), wc_ref[1])
    y = jax.nn.gelu(y + d_ref[...] * u)
    y = y * jax.nn.sigmoid(_dot(y.astype(BF16), gw_ref[...]) + gb_ref[...])
    o_ref[...] = _rms(y, ng_ref[...]).astype(o_ref.dtype)


def _s5(u_tm, lam_re, lam_im, b_re, b_im, c_re, c_im, d, log_dt, glu_w, glu_b, norm_g, ts):
    rows_total, W = u_tm.shape
    G, P = lam_re.shape
    dt = jnp.exp(log_dt)[:, None]
    mag = jnp.exp(lam_re * dt)
    a_re = mag * jnp.cos(lam_im * dt)
    a_im = mag * jnp.sin(lam_im * dt)
    den = lam_re * lam_re + lam_im * lam_im
    q_re = ((a_re - 1.0) * lam_re + a_im * lam_im) / den
    q_im = (a_im * lam_re - (a_re - 1.0) * lam_im) / den
    bb_re = q_re[..., None] * b_re - q_im[..., None] * b_im
    bb_im = q_re[..., None] * b_im + q_im[..., None] * b_re
    eye = jnp.eye(G, dtype=F32)
    wb_re = jnp.einsum('gpi,gh->gihp', bb_re, eye).reshape(W, G * P)
    wb_im = jnp.einsum('gpi,gh->gihp', bb_im, eye).reshape(W, G * P)
    wb = jnp.concatenate([wb_re, wb_im], axis=1).astype(BF16)
    wc_re = jnp.einsum('gip,gh->gphi', c_re, eye).reshape(G * P, W)
    wc_im = jnp.einsum('gip,gh->gphi', -c_im, eye).reshape(G * P, W)
    wc = jnp.stack([wc_re, wc_im]).astype(BF16)
    GP = G * P
    rows = ts * SUBLANE
    const = lambda shape: pl.BlockSpec(shape, lambda s: (0,) * len(shape))
    return pl.pallas_call(
        _s5_kernel,
        grid=(rows_total // rows,),
        in_specs=[pl.BlockSpec((rows, W), lambda s: (s, 0)),
                  const((W, 2 * GP)), const((1, GP)), const((1, GP)), const((2, GP, W)),
                  const((1, W)), const((W, W)), const((1, W)), const((1, W))],
        out_specs=pl.BlockSpec((rows, W), lambda s: (s, 0)),
        out_shape=jax.ShapeDtypeStruct((rows_total, W), BF16),
        scratch_shapes=[pltpu.VMEM((rows, GP), F32), pltpu.VMEM((rows, GP), F32),
                        pltpu.VMEM((SUBLANE, GP), F32), pltpu.VMEM((SUBLANE, GP), F32)],
        compiler_params=_cp(("arbitrary",)),
        name="s5",
    )(u_tm, wb, a_re.reshape(1, GP), a_im.reshape(1, GP), wc, d.reshape(1, W), glu_w.astype(BF16),
      glu_b.reshape(1, W), norm_g.reshape(1, W))


def _lru_kernel(x_ref, gate_ref, cw_ref, cb_ref, wax_ref, bax_ref, lam_ref, ng_ref, o_ref,
                xc_ref, a_ref, b_ref, hs_ref):
    nb = SUBLANE
    rows, W = x_ref.shape
    halo = (CONV_WIDTH - 1) * nb

    @pl.when(pl.program_id(0) == 0)
    def _():
        xc_ref[...] = jnp.zeros_like(xc_ref)
        hs_ref[...] = jnp.zeros_like(hs_ref)

    x = x_ref[...]
    xf = jnp.concatenate([xc_ref[...], x], axis=0)
    xc_ref[...] = x[rows - halo:, :]
    xc = cb_ref[...]
    for j in range(CONV_WIDTH):
        xc = xc + cw_ref[j:j + 1, :] * xf[j * nb:j * nb + rows, :]
    gates = _dot(xc.astype(BF16), wax_ref[...]) + bax_ref[...]
    r = jax.nn.sigmoid(gates[:, :W])
    i = jax.nn.sigmoid(gates[:, W:])
    log_a = -LRU_C * r * _softplus(-lam_ref[...])
    a = jnp.exp(log_a)
    a_ref[...] = a
    b_ref[...] = jnp.sqrt(1.0 - jnp.exp(2.0 * log_a)) * (i * xc)

    def step(t, h):
        i0 = pl.multiple_of(t * nb, nb)
        h = a_ref[pl.ds(i0, nb), :] * h + b_ref[pl.ds(i0, nb), :]
        b_ref[pl.ds(i0, nb), :] = h
        return h

    hs_ref[...] = lax.fori_loop(0, rows // nb, step, hs_ref[...])
    y = b_ref[...] * jax.nn.gelu(gate_ref[...])
    o_ref[...] = _rms(y, ng_ref[...]).astype(o_ref.dtype)


def _lru(x_tm, gate_tm, conv_w, conv_b, w_a, b_a, w_x, b_x, lam, norm_g, ts):
    rows_total, W = x_tm.shape
    nblk, bs, _ = w_a.shape
    eye = jnp.eye(nblk, dtype=F32)
    bd = lambda w_: jnp.einsum('nkj,nm->nkmj', w_, eye).reshape(W, W)
    wax = jnp.concatenate([bd(w_a), bd(w_x)], axis=1).astype(BF16)
    bax = jnp.concatenate([b_a, b_x]).reshape(1, 2 * W)
    rows = ts * SUBLANE
    const = lambda shape: pl.BlockSpec(shape, lambda s: (0,) * len(shape))
    return pl.pallas_call(
        _lru_kernel,
        grid=(rows_total // rows,),
        in_specs=[pl.BlockSpec((rows, W), lambda s: (s, 0)), pl.BlockSpec((rows, W), lambda s: (s, 0)),
                  const((CONV_WIDTH, W)), const((1, W)), const((W, 2 * W)), const((1, 2 * W)),
                  const((1, W)), const((1, W))],
        out_specs=pl.BlockSpec((rows, W), lambda s: (s, 0)),
        out_shape=jax.ShapeDtypeStruct((rows_total, W), BF16),
        scratch_shapes=[pltpu.VMEM(((CONV_WIDTH - 1) * SUBLANE, W), F32), pltpu.VMEM((rows, W), F32),
                        pltpu.VMEM((rows, W), F32), pltpu.VMEM((SUBLANE, W), F32)],
        compiler_params=_cp(("arbitrary",)),
        name="rglru",
    )(x_tm, gate_tm, conv_w, conv_b.reshape(1, W), wax, bax, lam.reshape(1, W), norm_g.reshape(1, W))


_VT_ROWS = MLA_V + 2 * SUBLANE


def _mla_prep_tile(pos0, q_a, kv_a, kpe, gq_ref, gkv_ref, wq_ref, wk_ref, wv_ref,
                   gqh_ref, gkh_ref, freq_ref, q_ref, k_ref, v_ref):
    tm = q_a.shape[0]
    pos = (pos0 + lax.broadcasted_iota(jnp.int32, (tm, LANE), 0)).astype(F32)
    ang = pos * freq_ref[...]
    cos_t = jnp.cos(ang)
    sin_t = jnp.sin(ang)
    lane = lax.broadcasted_iota(jnp.int32, (tm, LANE), 1)
    half = MLA_ROPE // 2
    first = (lane >= MLA_NOPE) & (lane < MLA_NOPE + half)
    second = (lane >= MLA_NOPE + half) & (lane < MLA_QK)

    def rope(t):
        rot = jnp.where(first, -pltpu.roll(t, LANE - half, axis=1),
                        jnp.where(second, pltpu.roll(t, half, axis=1), 0.0))
        return t * cos_t + rot * sin_t

    def head_norm(t, g):
        ms = jnp.sum(t * t, axis=-1, keepdims=True) * (1.0 / MLA_QK)
        return t * lax.rsqrt(ms + NORM_EPS) * g

    qn = _rms(q_a, gq_ref[...]).astype(BF16)
    kvn = _rms(kv_a, gkv_ref[...]).astype(BF16)
    q_all = _dot(qn, wq_ref[...])
    k_all = _dot(kvn, wk_ref[...])
    v_all = _dot(kvn, wv_ref[...])
    scale = MLA_QK ** -0.5 * math.log2(math.e)
    ones_row = lax.broadcasted_iota(jnp.int32, (_VT_ROWS, tm), 0) == MLA_V
    for h in range(MLA_HEADS):
        hs = slice(h * LANE, (h + 1) * LANE)
        q_ref[0, h] = (rope(head_norm(q_all[:, hs], gqh_ref[...])) * scale).astype(q_ref.dtype)
        k_ref[0, h] = rope(head_norm(k_all[:, hs] + kpe, gkh_ref[...])).astype(k_ref.dtype)
        v_ref[0, h] = jnp.where(ones_row, 1.0, v_all[:, hs].T[:_VT_ROWS]).astype(v_ref.dtype)


def _mla_prep_kernel(pos_ref, qa_ref, kva_ref, kpe_ref, *rest):
    pos0 = pos_ref[pl.program_id(0)] + pl.program_id(1) * qa_ref.shape[1]
    _mla_prep_tile(pos0, qa_ref[0], kva_ref[0], kpe_ref[0], *rest)


def _mla_prep(q_a, kv_a, kpe, pos_offset, q_norm_g, w_uq, kv_norm_g, w_ukv, q_head_g, k_head_g, tm):
    B, S, QR = q_a.shape
    KVR = kv_a.shape[2]
    H = MLA_HEADS
    wq = jnp.zeros((QR, H, LANE), F32).at[:, :, :MLA_QK].set(w_uq.reshape(QR, H, MLA_QK)).reshape(QR, H * LANE)
    wkv = w_ukv.reshape(KVR, H, MLA_NOPE + MLA_V)
    wk = jnp.zeros((KVR, H, LANE), F32).at[:, :, :MLA_NOPE].set(wkv[:, :, :MLA_NOPE]).reshape(KVR, H * LANE)
    wv = jnp.zeros((KVR, H, LANE), F32).at[:, :, :MLA_V].set(wkv[:, :, MLA_NOPE:]).reshape(KVR, H * LANE)
    pad_g = lambda g: jnp.zeros((1, LANE), F32).at[0, :MLA_QK].set(g)
    half = MLA_ROPE // 2
    inv_freq = np.power(np.float32(ROPE_THETA), -np.arange(half, dtype=np.float32) * np.float32(2.0) / np.float32(MLA_ROPE))
    freq = np.zeros((1, LANE), np.float32)
    freq[0, MLA_NOPE:MLA_NOPE + half] = inv_freq
    freq[0, MLA_NOPE + half:MLA_QK] = inv_freq
    const = lambda shape: pl.BlockSpec(shape, lambda b, s, p: (0,) * len(shape))
    row = lambda n: pl.BlockSpec((1, tm, n), lambda b, s, p: (b, s, 0))
    hout = pl.BlockSpec((1, H, tm, LANE), lambda b, s, p: (b, 0, s, 0))
    grid_spec = pltpu.PrefetchScalarGridSpec(
        num_scalar_prefetch=1, grid=(B, S // tm),
        in_specs=[row(QR), row(KVR), row(LANE), const((1, QR)), const((1, KVR)),
                  const((QR, H * LANE)), const((KVR, H * LANE)), const((KVR, H * LANE)),
                  const((1, LANE)), const((1, LANE)), const((1, LANE))],
        out_specs=[hout, hout, pl.BlockSpec((1, H, _VT_ROWS, tm), lambda b, s, p: (b, 0, 0, s))])
    return pl.pallas_call(
        _mla_prep_kernel,
        grid_spec=grid_spec,
        out_shape=[jax.ShapeDtypeStruct((B, H, S, LANE), BF16)] * 2 + [jax.ShapeDtypeStruct((B, H, _VT_ROWS, S), BF16)],
        compiler_params=_cp(("parallel", "parallel")),
        name="mla_prep",
    )(pos_offset, q_a, kv_a, kpe, q_norm_g.reshape(1, QR), kv_norm_g.reshape(1, KVR),
      wq.astype(BF16), wk.astype(BF16), wv.astype(BF16), pad_g(q_head_g), pad_g(k_head_g), jnp.asarray(freq))


def _attn_kernel(qtab_ref, ktab_ref, q_ref, k_ref, vt_ref, ng_ref, o_ref, m_ref, acc_ref):
    H = q_ref.shape[1]
    tq = q_ref.shape[2]
    j = pl.program_id(1)
    qi = qtab_ref[j]
    ki = ktab_ref[j]

    @pl.when(ki == 0)
    def _():
        m_ref[...] = jnp.full_like(m_ref, -1e30)
        acc_ref[...] = jnp.zeros_like(acc_ref)

    def update(masked):
        if masked:
            kc = lax.broadcasted_iota(jnp.int32, (tq, tq), 0) // ATTN_CHUNK
            qc = lax.broadcasted_iota(jnp.int32, (tq, tq), 1) // ATTN_CHUNK
            visible = kc <= qc
        for h in range(H):
            st = _dot_nt(k_ref[0, h], q_ref[0, h])
            if masked:
                st = jnp.where(visible, st, -1e30)
            m_old = m_ref[h]
            m_new = jnp.maximum(m_old, jnp.max(st, axis=0, keepdims=True))
            alpha = jnp.exp2(m_old - m_new)
            p = jnp.exp2(st - m_new).astype(BF16)
            acc_ref[h] = alpha * acc_ref[h] + _dot(vt_ref[0, h], p)
            m_ref[h] = m_new

    @pl.when(ki < qi)
    def _():
        update(False)

    @pl.when(ki == qi)
    def _():
        update(True)
        ot = jnp.concatenate([acc_ref[h, :MLA_V] / acc_ref[h, MLA_V:MLA_V + 1] for h in range(H)], axis=0)
        o_ref[0] = _rms(ot.T, ng_ref[...]).astype(o_ref.dtype)


def _attention(q, k, vt, norm_g, tq):
    B, H, S, _ = q.shape
    W = H * MLA_V
    nq = S // tq
    pairs = [(i, j) for i in range(nq) for j in range(i + 1)]
    qtab = jnp.asarray([p[0] for p in pairs], jnp.int32)
    ktab = jnp.asarray([p[1] for p in pairs], jnp.int32)
    grid_spec = pltpu.PrefetchScalarGridSpec(
        num_scalar_prefetch=2, grid=(B, len(pairs)),
        in_specs=[pl.BlockSpec((1, H, tq, LANE), lambda b, j, qt, kt: (b, 0, qt[j], 0)),
                  pl.BlockSpec((1, H, tq, LANE), lambda b, j, qt, kt: (b, 0, kt[j], 0)),
                  pl.BlockSpec((1, H, _VT_ROWS, tq), lambda b, j, qt, kt: (b, 0, 0, kt[j])),
                  pl.BlockSpec((1, W), lambda b, j, qt, kt: (0, 0))],
        out_specs=pl.BlockSpec((1, tq, W), lambda b, j, qt, kt: (b, qt[j], 0)),
        scratch_shapes=[pltpu.VMEM((H, 1, tq), F32), pltpu.VMEM((H, _VT_ROWS, tq), F32)])
    return pl.pallas_call(
        _attn_kernel,
        grid_spec=grid_spec,
        out_shape=jax.ShapeDtypeStruct((B, S, W), BF16),
        compiler_params=_cp(("parallel", "arbitrary")),
        name="mla_attn",
    )(qtab, ktab, q, k, vt, norm_g.reshape(1, W))


def _store_token_tiles(ref, val):
    n, d = val.shape
    for s in range(d // LANE):
        ref[pl.ds(s, n, stride=SUBLANE), :] = val[:, s * LANE:(s + 1) * LANE]


def _load_token_tiles(ref, n):
    return jnp.concatenate([ref[pl.ds(s, n, stride=SUBLANE), :] for s in range(SUBLANE)], axis=1)


def _outproj_kernel(x_ref, ya_ref, yb_ref, yc_ref, yd_ref, wo_ref, gt_ref, sc_ref, sh_ref, g_ref,
                    wr_ref, br_ref, xo_ref, h_ref, lg_ref):
    acc = _dot(ya_ref[0], wo_ref[0])
    acc = acc + _dot(yb_ref[0], wo_ref[1])
    acc = acc + _dot(yc_ref[0], wo_ref[2])
    acc = acc + _dot(yd_ref[0], wo_ref[3])
    x = x_ref[0] + gt_ref[0] * acc
    xo_ref[0] = x
    h = _rms(x, g_ref[...]) * (1.0 + sc_ref[0]) + sh_ref[0]
    _store_token_tiles(h_ref.at[0], h)
    h_hi, h_lo = _split_bf16(h, 2)
    lg_ref[0] = _dot(h_hi, wr_ref[0]) + (_dot(h_lo, wr_ref[0]) + _dot(h_hi, wr_ref[1])) + br_ref[...]


def _outproj(x, ya, yb, yc, yd, w_out, gt, sc, sh, g, w_router, b_router, tm):
    B, S, D = x.shape
    W = D // N_MIX
    row = lambda n: pl.BlockSpec((1, tm, n), lambda b, s: (b, s, 0))
    vec = pl.BlockSpec((1, 1, D), lambda b, s: (b, 0, 0))
    const = lambda shape: pl.BlockSpec(shape, lambda b, s: (0,) * len(shape))
    return pl.pallas_call(
        _outproj_kernel,
        grid=(B, S // tm),
        in_specs=[row(D), row(W), row(W), row(W), row(W), const((N_MIX, W, D)), vec, vec, vec,
                  const((1, D)), const((2, D, LANE)), const((1, LANE))],
        out_specs=[row(D), pl.BlockSpec((1, tm * SUBLANE, LANE), lambda b, s: (b, s, 0)), row(LANE)],
        out_shape=[jax.ShapeDtypeStruct((B, S, D), F32), jax.ShapeDtypeStruct((B, S * SUBLANE, LANE), F32),
                   jax.ShapeDtypeStruct((B, S, LANE), F32)],
        compiler_params=_cp(("parallel", "parallel")),
        name="outproj",
    )(x, ya, yb, yc, yd, w_out.reshape(N_MIX, W, D).astype(BF16), gt, sc, sh, g,
      jnp.stack(_split_bf16(w_router, 2)), b_router)


_MOE_PAD_ROWS = MOE_BLOCK // TOP_K


def _moe_kernel(be_ref, nused_ref, tok_ref, tokn_ref, dst_ref, w_ref, h_hbm, w1_ref, w3_ref, w2_ref, o_hbm,
                xbuf, ybuf, gsem, ssem):
    i = pl.program_id(0)
    R = MOE_BLOCK
    TR = SUBLANE
    nused = nused_ref[0]
    slot = lax.rem(i, 2)
    other = 1 - slot

    def gather_rows(idx_ref, s):
        def body(r, c):
            src = pl.multiple_of(idx_ref[0, 0, r], TR)
            pltpu.make_async_copy(h_hbm.at[pl.ds(src, TR)], xbuf.at[s, pl.ds(pl.multiple_of(r * TR, TR), TR)],
                                  gsem.at[s]).start()
            return c
        lax.fori_loop(0, R, body, 0, unroll=8)

    def scatter_rows(s):
        def body(r, c):
            dst = pl.multiple_of(dst_ref[0, 0, r], TR)
            pltpu.make_async_copy(ybuf.at[s, pl.ds(pl.multiple_of(r * TR, TR), TR)], o_hbm.at[pl.ds(dst, TR)],
                                  ssem.at[s]).start(priority=1)
            return c
        lax.fori_loop(0, R, body, 0, unroll=8)

    def wait_gather(s):
        pltpu.make_async_copy(h_hbm.at[pl.ds(0, R * TR)], xbuf.at[s], gsem.at[s]).wait()

    def wait_scatter(s):
        pltpu.make_async_copy(ybuf.at[s], o_hbm.at[pl.ds(0, R * TR)], ssem.at[s]).wait()

    @pl.when(i == 0)
    def _():
        plane = o_hbm.shape[0] // TOP_K
        npad = _MOE_PAD_ROWS * TR
        ybuf[1] = jnp.zeros(ybuf.shape[1:], ybuf.dtype)
        for j in range(TOP_K):
            pad = pltpu.make_async_copy(ybuf.at[1, pl.ds(0, npad)],
                                        o_hbm.at[pl.ds((j + 1) * plane - npad, npad)], ssem.at[1])
            pad.start()
            pad.wait()
        gather_rows(tok_ref, 0)

    @pl.when(i < nused)
    def _():
        @pl.when(i + 1 < nused)
        def _():
            gather_rows(tokn_ref, other)

        wait_gather(slot)

        @pl.when(i >= 2)
        def _():
            wait_scatter(slot)

        xb = _load_token_tiles(xbuf.at[slot], R).astype(BF16)
        hid = _dot(xb, w1_ref[0])
        hid = hid * jax.nn.sigmoid(hid) * _dot(xb, w3_ref[0])
        y = _dot(hid.astype(BF16), w2_ref[0])
        _store_token_tiles(ybuf.at[slot], y * w_ref[:, 0:1])
        scatter_rows(slot)

        @pl.when(i == nused - 1)
        def _():
            wait_scatter(slot)

            @pl.when(i >= 1)
            def _():
                wait_scatter(other)


def _moe_experts(h2, blk_expert, n_used, row_tok, row_dst, row_w, w1, w3, w2):
    E, D, F = w1.shape
    T = h2.shape[0] // SUBLANE
    nb = blk_expert.shape[0]
    R = MOE_BLOCK
    blk = lambda i, be, nu: (i, 0, 0)
    nxt = lambda i, be, nu: (jnp.minimum(i + 1, nb - 1), 0, 0)
    wsel = lambda i, be, nu: (be[i], 0, 0)
    plane = (T + _MOE_PAD_ROWS) * SUBLANE
    row_tok = (row_tok * SUBLANE).reshape(nb, 1, R)
    row_dst = ((row_dst % TOP_K) * plane + (row_dst // TOP_K) * SUBLANE).reshape(nb, 1, R)
    grid_spec = pltpu.PrefetchScalarGridSpec(
        num_scalar_prefetch=2, grid=(nb,),
        in_specs=[pl.BlockSpec((1, 1, R), blk, memory_space=pltpu.SMEM),
                  pl.BlockSpec((1, 1, R), nxt, memory_space=pltpu.SMEM),
                  pl.BlockSpec((1, 1, R), blk, memory_space=pltpu.SMEM),
                  pl.BlockSpec((R, LANE), lambda i, be, nu: (i, 0)),
                  pl.BlockSpec(memory_space=pl.ANY),
                  pl.BlockSpec((1, D, F), wsel), pl.BlockSpec((1, D, F), wsel), pl.BlockSpec((1, F, D), wsel)],
        out_specs=pl.BlockSpec(memory_space=pl.ANY),
        scratch_shapes=[pltpu.VMEM((2, R * SUBLANE, LANE), F32), pltpu.VMEM((2, R * SUBLANE, LANE), F32),
                        pltpu.SemaphoreType.DMA((2,)), pltpu.SemaphoreType.DMA((2,))])
    return pl.pallas_call(
        _moe_kernel,
        grid_spec=grid_spec,
        out_shape=jax.ShapeDtypeStruct((TOP_K * plane, LANE), F32),
        compiler_params=_cp(("arbitrary",)),
        name="moe_experts",
    )(blk_expert, n_used, row_tok, row_tok, row_dst, jnp.broadcast_to(row_w.reshape(nb * R, 1), (nb * R, LANE)),
      h2, w1, w3, w2)


def _route(logits, T):
    p_group = jax.nn.softmax(logits[:, :N_GROUPS], axis=-1)
    g_sel = jnp.argmax(p_group, axis=-1).astype(jnp.int32)
    g_prob = jnp.max(p_group, axis=-1)
    logit_e = logits[:, N_GROUPS:N_GROUPS + N_EXPERTS].reshape(T, N_GROUPS, EXPERTS_PER_GROUP)
    logit_e = jnp.take_along_axis(logit_e, g_sel[:, None, None], axis=1)[:, 0]
    p_e = jax.nn.softmax(logit_e, axis=-1)
    top_p, top_i = lax.top_k(p_e, TOP_K)
    gate = (g_prob[:, None] * top_p / jnp.sum(top_p, axis=-1, keepdims=True)).reshape(-1)
    expert = (g_sel[:, None] * EXPERTS_PER_GROUP + top_i.astype(jnp.int32)).reshape(-1)
    n_assign = T * TOP_K
    n_blocks = -(-n_assign // MOE_BLOCK) + N_EXPERTS
    order = jnp.argsort(expert).astype(jnp.int32)
    counts = jnp.sum((expert[:, None] == jnp.arange(N_EXPERTS, dtype=jnp.int32)[None, :]).astype(jnp.int32), axis=0)
    start = jnp.cumsum(counts) - counts
    pcounts = (counts + MOE_BLOCK - 1) // MOE_BLOCK * MOE_BLOCK
    pend = jnp.cumsum(pcounts)
    pstart = pend - pcounts
    blk_start = jnp.arange(n_blocks, dtype=jnp.int32) * MOE_BLOCK
    blk_expert = jnp.minimum(jnp.sum((pend[None, :] <= blk_start[:, None]).astype(jnp.int32), axis=1), N_EXPERTS - 1)
    n_used = (pend[-1] // MOE_BLOCK).astype(jnp.int32).reshape(1)
    lane = jnp.arange(MOE_BLOCK, dtype=jnp.int32)[None, :]
    off = blk_start[:, None] + lane - pstart[blk_expert][:, None]
    valid = off < counts[blk_expert][:, None]
    pos = jnp.clip(start[blk_expert][:, None] + off, 0, n_assign - 1)
    asg = order[pos]
    row_tok = jnp.where(valid, asg // TOP_K, 0)
    row_dst = jnp.where(valid, asg, n_assign + lane)
    row_w = jnp.where(valid, gate[asg], 0.0)
    return blk_expert.astype(jnp.int32), n_used, row_tok, row_dst, row_w


def _combine_kernel(x_ref, y0_ref, y1_ref, gt_ref, o_ref):
    tm = x_ref.shape[1]
    y = _load_token_tiles(y0_ref.at[0], tm) + _load_token_tiles(y1_ref.at[0], tm)
    o_ref[0] = x_ref[0] + gt_ref[0] * y


def _combine(x, y2, gt, tm):
    B, S, D = x.shape
    nt = S // tm
    return pl.pallas_call(
        _combine_kernel,
        grid=(B, nt),
        in_specs=[pl.BlockSpec((1, tm, D), lambda b, s: (b, s, 0)),
                  pl.BlockSpec((1, tm * SUBLANE, LANE), lambda b, s: (0, b * nt + s, 0)),
                  pl.BlockSpec((1, tm * SUBLANE, LANE), lambda b, s: (1, b * nt + s, 0)),
                  pl.BlockSpec((1, 1, D), lambda b, s: (b, 0, 0))],
        out_specs=pl.BlockSpec((1, tm, D), lambda b, s: (b, s, 0)),
        out_shape=jax.ShapeDtypeStruct((B, S, D), F32),
        compiler_params=_cp(("parallel", "parallel")),
        name="moe_combine",
    )(x, y2, y2, gt)


def _tiles(S):
    big = S >= 2048
    return dict(tm=1024 if big else 256, tl=1024 if big else 256, ts_s5=64, ts_lru=128 if big else 64,
                tq=1024 if big else 256)


def kernel(x, c, pos_offset, ada_w, ada_b, norm1_g, w_in, rwkv_mu, rwkv_w0, rwkv_w2, rwkv_a0, rwkv_a2, rwkv_g2, rwkv_k_k, rwkv_k_a, rwkv_r_k, rwkv_ln_w, rwkv_ln_b, s5_lambda_re, s5_lambda_im, s5_b_re, s5_b_im, s5_c_re, s5_c_im, s5_d, s5_log_dt, s5_glu_w, s5_glu_b, mla_q_norm_g, mla_w_uq, mla_kv_norm_g, mla_w_ukv, mla_q_head_g, mla_k_head_g, lru_conv_w, lru_conv_b, lru_w_a, lru_b_a, lru_w_x, lru_b_x, lru_lambda, branch_norm_g, w_out, norm2_g, moe_w_group, moe_b_group, moe_w_expert, moe_b_expert, moe_w1, moe_w3, moe_w2):
    B, S, D = x.shape
    depth = ada_w.shape[0]
    T = B * S
    W = D // N_MIX
    assert B == SUBLANE, "time-major scans put the batch on the 8 sublanes"
    assert D == SUBLANE * LANE, "token-tile layout: one (8, 128) tile per token row"
    tiles = _tiles(S)
    mod = _adaln_mod(c, ada_w, ada_b)

    def to_tm(t):
        return jnp.swapaxes(t, 0, 1).reshape(S * B, t.shape[-1])

    def from_tm(t):
        return jnp.swapaxes(t.reshape(S, B, t.shape[-1]), 0, 1)

    for l in range(depth):
        sh1, sc1, gt1, sh2, sc2, gt2 = [mod[l, :, None, i * D:(i + 1) * D] for i in range(6)]
        z_rwkv, z_s5, q_a, kv_a, kpe, z_lru, z_gate = _inproj(
            x, sc1, sh1, norm1_g[l].reshape(1, D), _pad_w_in(w_in[l]), tiles['tm'])
        y_a = _rwkv(z_rwkv, rwkv_mu[l], rwkv_w0[l], rwkv_w2[l], rwkv_a0[l], rwkv_a2[l], rwkv_g2[l],
                    rwkv_k_k[l], rwkv_k_a[l], rwkv_r_k[l].reshape(-1), rwkv_ln_w[l], rwkv_ln_b[l], tiles['tl'])
        y_b = from_tm(_s5(to_tm(z_s5), s5_lambda_re[l], s5_lambda_im[l], s5_b_re[l], s5_b_im[l], s5_c_re[l],
                          s5_c_im[l], s5_d[l], s5_log_dt[l], s5_glu_w[l], s5_glu_b[l], branch_norm_g[l, 0],
                          tiles['ts_s5']))
        q, k, vt = _mla_prep(q_a, kv_a, kpe, pos_offset, mla_q_norm_g[l], mla_w_uq[l], mla_kv_norm_g[l],
                             mla_w_ukv[l], mla_q_head_g[l], mla_k_head_g[l], tiles['tm'])
        y_c = _attention(q, k, vt, branch_norm_g[l, 1], tiles['tq'])
        y_d = from_tm(_lru(to_tm(z_lru), to_tm(z_gate), lru_conv_w[l], lru_conv_b[l], lru_w_a[l], lru_b_a[l],
                           lru_w_x[l], lru_b_x[l], lru_lambda[l], branch_norm_g[l, 2], tiles['ts_lru']))
        w_router = jnp.zeros((D, LANE), F32).at[:, :N_GROUPS].set(moe_w_group[l])
        w_router = w_router.at[:, N_GROUPS:N_GROUPS + N_EXPERTS].set(moe_w_expert[l])
        b_router = jnp.zeros((1, LANE), F32).at[0, :N_GROUPS].set(moe_b_group[l])
        b_router = b_router.at[0, N_GROUPS:N_GROUPS + N_EXPERTS].set(moe_b_expert[l])
        x, h2, logits = _outproj(x, y_a, y_b, y_c, y_d, w_out[l], gt1, sc2, sh2, norm2_g[l].reshape(1, D),
                                 w_router, b_router, tiles['tm'])
        blk_expert, n_used, row_tok, row_dst, row_w = _route(logits.reshape(T, LANE), T)
        y2 = _moe_experts(h2.reshape(T * SUBLANE, LANE), blk_expert, n_used, row_tok, row_dst, row_w,
                          moe_w1[l].astype(BF16), moe_w3[l].astype(BF16), moe_w2[l].astype(BF16))
        x = _combine(x, y2.reshape(TOP_K, -1, LANE), gt2, tiles['tm'])
    return x
```

```python
import functools
import math

import numpy as np
import jax
import jax.numpy as jnp
from jax import lax
from jax.experimental import pallas as pl
from jax.experimental.pallas import tpu as pltpu

F32 = jnp.float32
BF16 = jnp.bfloat16
HI = lax.Precision.HIGHEST

N_MIX = 4
RWKV_HEAD = 64
RWKV_W_RANK, RWKV_A_RANK, RWKV_G_RANK = 32, 32, 64
RWKV_LN_EPS = 64e-5
S5_GROUP, S5_STATE = 16, 64
MLA_HEADS, MLA_NOPE, MLA_ROPE, MLA_V = 4, 64, 32, 64
MLA_QK = MLA_NOPE + MLA_ROPE
ROPE_THETA = 10000.0
LRU_C = 8.0
CONV_WIDTH = 4
ATTN_CHUNK = 64
N_GROUPS, EXPERTS_PER_GROUP, TOP_K = 4, 8, 2
N_EXPERTS = N_GROUPS * EXPERTS_PER_GROUP
MOE_BLOCK = 256
NORM_EPS = 1e-6

LANE = 128
SUBLANE = 8
RWKV_CHUNK = 64
VMEM_LIMIT = 48 * 1024 * 1024


def _cp(sem, vmem=VMEM_LIMIT):
    return pltpu.CompilerParams(dimension_semantics=sem, vmem_limit_bytes=vmem)


def _dot(a, b):
    return jnp.dot(a, b, preferred_element_type=F32)


def _dot_hi(a, b):
    return jnp.dot(a, b, precision=HI, preferred_element_type=F32)


def _dot_nt(a, b):
    return lax.dot_general(a, b, (((1,), (1,)), ((), ())), preferred_element_type=F32)


def _dot_nt_hi(a, b):
    return lax.dot_general(a, b, (((1,), (1,)), ((), ())), precision=HI, preferred_element_type=F32)


def _softplus(x):
    return jnp.maximum(x, 0.0) + jnp.log1p(jnp.exp(-jnp.abs(x)))


def _rms(x, g, eps=NORM_EPS):
    return x * lax.rsqrt(jnp.mean(x * x, axis=-1, keepdims=True) + eps) * g


def _mod_kernel(c_ref, w_ref, b_ref, o_ref):
    c = c_ref[...]
    cond = c * jax.nn.sigmoid(c)
    o_ref[0] = _dot_hi(cond, w_ref[0]) + b_ref[0]


def _adaln_mod(c, ada_w, ada_b):
    L, D, D6 = ada_w.shape
    B = c.shape[0]
    nj = D6 // D
    return pl.pallas_call(
        _mod_kernel,
        grid=(L, nj),
        in_specs=[pl.BlockSpec((B, D), lambda l, j: (0, 0)),
                  pl.BlockSpec((1, D, D), lambda l, j: (l, 0, j)),
                  pl.BlockSpec((1, 1, D), lambda l, j: (l, 0, j))],
        out_specs=pl.BlockSpec((1, B, D), lambda l, j: (l, 0, j)),
        out_shape=jax.ShapeDtypeStruct((L, B, D6), F32),
        compiler_params=_cp(("parallel", "parallel")),
        name="adaln_mod",
    )(c, ada_w, ada_b.reshape(L, 1, D6))


_RWKV_IN = 3 * 256 + RWKV_W_RANK + RWKV_A_RANK + RWKV_G_RANK
_IN_COLS = (_RWKV_IN, 256, 256, 128, LANE, 256, 256)
_IN_OFFS = tuple(int(v) for v in np.cumsum((0,) + _IN_COLS))
_KPE_LANE0 = MLA_NOPE


def _pad_w_in(w_in):
    D = w_in.shape[0]
    o = np.cumsum((0, _RWKV_IN, 256, 256, 128, MLA_ROPE, 256, 256))
    pieces = [w_in[:, o[i]:o[i + 1]] for i in range(7)]
    kpe = jnp.zeros((D, LANE), w_in.dtype).at[:, _KPE_LANE0:_KPE_LANE0 + MLA_ROPE].set(pieces[4])
    pieces[4] = kpe
    return jnp.concatenate(pieces, axis=1).astype(BF16)


_IN_TIME_MAJOR = (1, 5, 6)


def _store_time_major(ref, val, nb):
    ts = val.shape[0] // nb
    for j in range(val.shape[1] // LANE):
        for b in range(nb):
            ref[j, pl.ds(b, ts, stride=nb), :] = val[b * ts:(b + 1) * ts, j * LANE:(j + 1) * LANE]


def _load_batch_major(ref, nb):
    ts = ref.shape[1] // nb
    return jnp.concatenate(
        [jnp.concatenate([ref[j, pl.ds(b, ts, stride=nb), :] for b in range(nb)], axis=0)
         for j in range(ref.shape[0])], axis=1)


def _lanes(ref):
    return jnp.concatenate([ref[j] for j in range(ref.shape[0])], axis=1)


def _store_lanes(ref, val):
    for j in range(ref.shape[0]):
        ref[j] = val[:, j * LANE:(j + 1) * LANE].astype(ref.dtype)


def _inproj_kernel(x_ref, sc_ref, sh_ref, g_ref, w_ref, *outs):
    B, ts, D = x_ref.shape
    h = _rms(x_ref[...], g_ref[...]) * (1.0 + sc_ref[...]) + sh_ref[...]
    z = _dot(h.reshape(B * ts, D).astype(BF16), w_ref[...])
    for i, o_ref in enumerate(outs):
        piece = z[:, _IN_OFFS[i]:_IN_OFFS[i + 1]]
        if i in _IN_TIME_MAJOR:
            _store_time_major(o_ref, piece, B)
        else:
            o_ref[...] = piece.reshape(B, ts, piece.shape[1])


def _inproj(x, sc, sh, g, w_pad, tm):
    B, S, D = x.shape
    ts = tm // B
    bm = lambda n: pl.BlockSpec((B, ts, n), lambda s: (0, s, 0))
    const = lambda shape: pl.BlockSpec(shape, lambda s: (0,) * len(shape))
    return pl.pallas_call(
        _inproj_kernel,
        grid=(S // ts,),
        in_specs=[bm(D), const((B, 1, D)), const((B, 1, D)), const((1, D)), const(w_pad.shape)],
        out_specs=[pl.BlockSpec((n // LANE, ts * B, LANE), lambda s: (0, s, 0)) if i in _IN_TIME_MAJOR else bm(n)
                   for i, n in enumerate(_IN_COLS)],
        out_shape=[jax.ShapeDtypeStruct((n // LANE, S * B, LANE) if i in _IN_TIME_MAJOR else (B, S, n), F32)
                   for i, n in enumerate(_IN_COLS)],
        compiler_params=_cp(("parallel",)),
        name="inproj",
    )(x, sc, sh, g, w_pad)


def _split_bf16(x, parts):
    out = []
    for _ in range(parts - 1):
        hi = x.astype(BF16)
        out.append(hi)
        x = x - hi.astype(F32)
    out.append(x.astype(BF16))
    return out


def _rwkv_kernel(z_ref, mu_ref, w0_ref, a0_ref, wl_ref, kk_ref, ka_ref, rk_ref, lnw_ref, lnb_ref,
                 hsum_ref, tri_ref, o_ref, zprev_ref, g_ref, y_ref):
    W = 256
    NH = W // RWKV_HEAD
    L = RWKV_CHUNK
    TL = z_ref.shape[1]

    @pl.when(pl.program_id(1) == 0)
    def _():
        zprev_ref[...] = jnp.zeros_like(zprev_ref)
        g_ref[...] = jnp.zeros_like(g_ref)

    hsum = hsum_ref[...]

    def head_sum(t):
        hi, lo_ = _split_bf16(t, 2)
        return _dot(hi, hsum) + _dot(lo_, hsum)

    z = z_ref[0]
    rows = lax.broadcasted_iota(jnp.int32, z.shape, 0)
    zp = jnp.where(rows == 0, zprev_ref[...], pltpu.roll(z, 1, axis=0))
    zprev_ref[...] = z[TL - 1:TL, :]
    zs = z + (zp - z) * mu_ref[...]
    r, k, v, lo = zs[:, 0:W], zs[:, W:2 * W], zs[:, 2 * W:3 * W], zs[:, 3 * W:3 * W + LANE]
    lane = lax.broadcasted_iota(jnp.int32, lo.shape, 1)
    act = jnp.where(lane < RWKV_W_RANK, jnp.tanh(lo),
                    jnp.where(lane < RWKV_W_RANK + RWKV_A_RANK, lo, jax.nn.sigmoid(lo)))
    up = _dot(act.astype(BF16), wl_ref[...])
    w = -_softplus(-(w0_ref[...] + up[:, 0:W])) - 0.5
    ld = -jnp.exp(w)
    a = jax.nn.sigmoid(a0_ref[...] + up[:, W:2 * W])
    g = up[:, 2 * W:3 * W]
    kk = k * kk_ref[...]
    kk = kk / jnp.maximum(jnp.sqrt(head_sum(kk * kk)), 1e-12)
    k2 = k * (1.0 + (a - 1.0) * ka_ref[...])
    kka = kk * a

    tri_b = tri_ref[0]
    tri_incl = tri_b > 0
    tri_strict = tri_ref[1] > 0
    eye = (lax.broadcasted_iota(jnp.int32, (RWKV_HEAD, RWKV_HEAD), 0)
           == lax.broadcasted_iota(jnp.int32, (RWKV_HEAD, RWKV_HEAD), 1)).astype(BF16)

    NC = TL // L
    pairs = [(j, h) for j in range(NC) for h in range(NH)]
    hsl = [slice(h * RWKV_HEAD, (h + 1) * RWKV_HEAD) for h in range(NH)]
    csl = [slice(j * L, (j + 1) * L) for j in range(NC)]
    bt, rt, a_s, k_s, a_e, k_e, v_c, p_l = [], [], [], [], [], [], [], []
    for j in range(NC):
        ld_c = ld[csl[j]]
        cs = sum(_dot(tri_b, p) for p in _split_bf16(ld_c, 3))
        cs_l = cs[L - 1:L, :]
        p_inv = jnp.exp(-cs)
        p_end = jnp.exp(cs_l - cs)
        p_l.append(jnp.exp(cs_l))
        bt.append(-kk[csl[j]] * jnp.exp(cs - ld_c))
        rt.append(r[csl[j]] * jnp.exp(cs))
        a_s.append((kka[csl[j]] * p_inv).astype(BF16))
        k_s.append((k2[csl[j]] * p_inv).astype(BF16))
        a_e.append((kka[csl[j]] * p_end).astype(BF16))
        k_e.append((k2[csl[j]] * p_end).astype(BF16))
        v_c.append(v[csl[j]].astype(BF16))

    x = [jnp.concatenate([bt[j][:, hsl[h]], rt[j][:, hsl[h]]], axis=0).astype(BF16) for j, h in pairs]
    ga = [_dot_nt(x[i], a_s[j][:, hsl[h]]) for i, (j, h) in enumerate(pairs)]
    gk = [_dot_nt(x[i], k_s[j][:, hsl[h]]) for i, (j, h) in enumerate(pairs)]
    m_ba = [jnp.where(tri_strict, t[:L], 0.0).astype(BF16) for t in ga]
    m_ra = [jnp.where(tri_incl, t[L:], 0.0).astype(BF16) for t in ga]
    m_bk = [jnp.where(tri_strict, t[:L], 0.0).astype(BF16) for t in gk]
    m_rk = [jnp.where(tri_incl, t[L:], 0.0).astype(BF16) for t in gk]
    v_h = [v_c[j][:, hsl[h]] for j, h in pairs]
    wn = [jnp.concatenate([bt[j][:, hsl[h]], _dot(m_bk[i], v_h[i])], axis=1) for i, (j, h) in enumerate(pairs)]
    n_pow = m_ba
    wn = [w_ + _dot(n_, w_.astype(BF16)) for w_, n_ in zip(wn, n_pow)]
    for _ in range(5):
        n_pow = [_dot(n_, n_).astype(BF16) for n_ in n_pow]
        wn = [w_ + _dot(n_, w_.astype(BF16)) for w_, n_ in zip(wn, n_pow)]
    wn_b = [w_.astype(BF16) for w_ in wn]
    mw = [_dot(m_, w_) for m_, w_ in zip(m_ra, wn_b)]
    r2 = [(rt[j][:, hsl[h]] + mw[i][:, :RWKV_HEAD]).astype(BF16) for i, (j, h) in enumerate(pairs)]
    y0 = [mw[i][:, RWKV_HEAD:] + _dot(m_rk[i], v_h[i]) for i in range(len(pairs))]
    b2_t = [_dot_nt(eye, w_[:, :RWKV_HEAD]).astype(BF16) for w_ in wn_b]
    c_m = [_dot(b2_t[i], a_e[j][:, hsl[h]]).astype(BF16) for i, (j, h) in enumerate(pairs)]
    uv_t = [_dot_nt(eye, jnp.concatenate([wn_b[i][:, RWKV_HEAD:], v_h[i]], axis=0)).astype(BF16)
            for i in range(len(pairs))]
    d_t = [_dot(uv_t[i], jnp.concatenate([a_e[j][:, hsl[h]], k_e[j][:, hsl[h]]], axis=0))
           for i, (j, h) in enumerate(pairs)]
    for i, (j, h) in enumerate(pairs):
        g0 = g_ref[h]
        g_hi, g_lo = _split_bf16(g0, 2)
        y_ref[csl[j], hsl[h]] = _dot_nt(r2[i], g_hi) + _dot_nt(r2[i], g_lo) + y0[i]
        g_ref[h] = g0 * p_l[j][:, hsl[h]] + _dot(g_hi, c_m[i]) + _dot(g_lo, c_m[i]) + d_t[i]

    y = y_ref[...]
    inv_n = 1.0 / RWKV_HEAD
    mean = head_sum(y) * inv_n
    yc = y - mean
    var = head_sum(yc * yc) * inv_n
    yn = yc * lax.rsqrt(var + RWKV_LN_EPS) * lnw_ref[...] + lnb_ref[...]
    bonus = head_sum(r * k2 * rk_ref[...]) * v
    o_ref[0] = ((yn + bonus) * g).astype(o_ref.dtype)


def _rwkv(z, mu, w0, w2, a0, a2, g2, k_k, k_a, r_k, ln_w, ln_b, tl):
    B, S, _ = z.shape
    W = 256
    L = RWKV_CHUNK
    wl = jnp.zeros((LANE, 3 * W), F32)
    wl = wl.at[0:RWKV_W_RANK, 0:W].set(w2)
    wl = wl.at[RWKV_W_RANK:RWKV_W_RANK + RWKV_A_RANK, W:2 * W].set(a2)
    wl = wl.at[RWKV_W_RANK + RWKV_A_RANK:LANE, 2 * W:3 * W].set(g2).astype(BF16)
    hid = np.arange(W) // RWKV_HEAD
    hsum = jnp.asarray((hid[:, None] == hid[None, :]).astype(np.float32), dtype=BF16)
    t = np.arange(L)
    tri = jnp.asarray(np.stack([(t[None, :] <= t[:, None]), (t[None, :] < t[:, None])]).astype(np.float32), dtype=BF16)
    row2 = lambda v_: v_.reshape(1, -1)
    const = lambda shape: pl.BlockSpec(shape, lambda b, s: (0,) * len(shape))
    return pl.pallas_call(
        _rwkv_kernel,
        grid=(B, S // tl),
        in_specs=[pl.BlockSpec((1, tl, _RWKV_IN), lambda b, s: (b, s, 0)),
                  const((1, _RWKV_IN)), const((1, W)), const((1, W)), const((LANE, 3 * W)),
                  const((1, W)), const((1, W)), const((1, W)), const((1, W)), const((1, W)),
                  const((W, W)), const((2, L, L))],
        out_specs=pl.BlockSpec((1, tl, W), lambda b, s: (b, s, 0)),
        out_shape=jax.ShapeDtypeStruct((B, S, W), BF16),
        scratch_shapes=[pltpu.VMEM((1, _RWKV_IN), F32),
                        pltpu.VMEM((W // RWKV_HEAD, RWKV_HEAD, RWKV_HEAD), F32),
                        pltpu.VMEM((tl, W), F32)],
        compiler_params=_cp(("parallel", "arbitrary")),
        name="rwkv7",
    )(z, row2(mu), row2(w0), row2(a0), wl, row2(k_k), row2(k_a), row2(r_k), row2(ln_w), row2(ln_b), hsum, tri)


def _s5_kernel(u_ref, wb_ref, are_ref, aim_ref, wc_ref, d_ref, gw_ref, gb_ref, ng_ref, o_ref,
               hre_ref, him_ref, sre_ref, sim_ref):
    nb = SUBLANE
    rows = u_ref.shape[1]
    P = are_ref.shape[1]

    @pl.when(pl.program_id(0) == 0)
    def _():
        sre_ref[...] = jnp.zeros_like(sre_ref)
        sim_ref[...] = jnp.zeros_like(sim_ref)

    u = _lanes(u_ref)
    bu = _dot(u.astype(BF16), wb_ref[...])
    hre_ref[...] = bu[:, :P]
    him_ref[...] = bu[:, P:]
    a_re = jnp.broadcast_to(are_ref[...], (nb, P))
    a_im = jnp.broadcast_to(aim_ref[...], (nb, P))

    def step(t, carry):
        h_re, h_im = carry
        i = pl.multiple_of(t * nb, nb)
        n_re = a_re * h_re - a_im * h_im + hre_ref[pl.ds(i, nb), :]
        n_im = a_re * h_im + a_im * h_re + him_ref[pl.ds(i, nb), :]
        hre_ref[pl.ds(i, nb), :] = n_re
        him_ref[pl.ds(i, nb), :] = n_im
        return n_re, n_im

    h_re, h_im = lax.fori_loop(0, rows // nb, step, (sre_ref[...], sim_ref[...]))
    sre_ref[...] = h_re
    sim_ref[...] = h_im
    y = _dot(hre_ref[...].astype(BF16), wc_ref[0]) + _dot(him_ref[...].astype(BF16), wc_ref[1])
    y = jax.nn.gelu(y + d_ref[...] * u)
    y = y * jax.nn.sigmoid(_dot(y.astype(BF16), gw_ref[...]) + gb_ref[...])
    _store_lanes(o_ref, _rms(y, ng_ref[...]))


def _s5(u_tm, lam_re, lam_im, b_re, b_im, c_re, c_im, d, log_dt, glu_w, glu_b, norm_g, ts):
    nl, rows_total, _ = u_tm.shape
    W = nl * LANE
    tmaj = lambda r: pl.BlockSpec((nl, r, LANE), lambda s: (0, s, 0))
    G, P = lam_re.shape
    dt = jnp.exp(log_dt)[:, None]
    mag = jnp.exp(lam_re * dt)
    a_re = mag * jnp.cos(lam_im * dt)
    a_im = mag * jnp.sin(lam_im * dt)
    den = lam_re * lam_re + lam_im * lam_im
    q_re = ((a_re - 1.0) * lam_re + a_im * lam_im) / den
    q_im = (a_im * lam_re - (a_re - 1.0) * lam_im) / den
    bb_re = q_re[..., None] * b_re - q_im[..., None] * b_im
    bb_im = q_re[..., None] * b_im + q_im[..., None] * b_re
    eye = jnp.eye(G, dtype=F32)
    wb_re = jnp.einsum('gpi,gh->gihp', bb_re, eye).reshape(W, G * P)
    wb_im = jnp.einsum('gpi,gh->gihp', bb_im, eye).reshape(W, G * P)
    wb = jnp.concatenate([wb_re, wb_im], axis=1).astype(BF16)
    wc_re = jnp.einsum('gip,gh->gphi', c_re, eye).reshape(G * P, W)
    wc_im = jnp.einsum('gip,gh->gphi', -c_im, eye).reshape(G * P, W)
    wc = jnp.stack([wc_re, wc_im]).astype(BF16)
    GP = G * P
    rows = ts * SUBLANE
    const = lambda shape: pl.BlockSpec(shape, lambda s: (0,) * len(shape))
    return pl.pallas_call(
        _s5_kernel,
        grid=(rows_total // rows,),
        in_specs=[tmaj(rows),
                  const((W, 2 * GP)), const((1, GP)), const((1, GP)), const((2, GP, W)),
                  const((1, W)), const((W, W)), const((1, W)), const((1, W))],
        out_specs=tmaj(rows),
        out_shape=jax.ShapeDtypeStruct((nl, rows_total, LANE), F32),
        scratch_shapes=[pltpu.VMEM((rows, GP), F32), pltpu.VMEM((rows, GP), F32),
                        pltpu.VMEM((SUBLANE, GP), F32), pltpu.VMEM((SUBLANE, GP), F32)],
        compiler_params=_cp(("arbitrary",)),
        name="s5",
    )(u_tm, wb, a_re.reshape(1, GP), a_im.reshape(1, GP), wc, d.reshape(1, W), glu_w.astype(BF16),
      glu_b.reshape(1, W), norm_g.reshape(1, W))


def _lru_kernel(x_ref, gate_ref, cw_ref, cb_ref, wax_ref, bax_ref, lam_ref, ng_ref, o_ref,
                xc_ref, a_ref, b_ref, hs_ref):
    nb = SUBLANE
    rows = x_ref.shape[1]
    W = x_ref.shape[0] * LANE
    halo = (CONV_WIDTH - 1) * nb

    @pl.when(pl.program_id(0) == 0)
    def _():
        xc_ref[...] = jnp.zeros_like(xc_ref)
        hs_ref[...] = jnp.zeros_like(hs_ref)

    x = _lanes(x_ref)
    xf = jnp.concatenate([xc_ref[...], x], axis=0)
    xc_ref[...] = x[rows - halo:, :]
    xc = cb_ref[...]
    for j in range(CONV_WIDTH):
        xc = xc + cw_ref[j:j + 1, :] * xf[j * nb:j * nb + rows, :]
    gates = _dot(xc.astype(BF16), wax_ref[...]) + bax_ref[...]
    r = jax.nn.sigmoid(gates[:, :W])
    i = jax.nn.sigmoid(gates[:, W:])
    log_a = -LRU_C * r * _softplus(-lam_ref[...])
    a = jnp.exp(log_a)
    a_ref[...] = a
    b_ref[...] = jnp.sqrt(1.0 - jnp.exp(2.0 * log_a)) * (i * xc)

    def step(t, h):
        i0 = pl.multiple_of(t * nb, nb)
        h = a_ref[pl.ds(i0, nb), :] * h + b_ref[pl.ds(i0, nb), :]
        b_ref[pl.ds(i0, nb), :] = h
        return h

    hs_ref[...] = lax.fori_loop(0, rows // nb, step, hs_ref[...])
    y = b_ref[...] * jax.nn.gelu(_lanes(gate_ref))
    _store_lanes(o_ref, _rms(y, ng_ref[...]))


def _lru(x_tm, gate_tm, conv_w, conv_b, w_a, b_a, w_x, b_x, lam, norm_g, ts):
    nl, rows_total, _ = x_tm.shape
    W = nl * LANE
    tmaj = lambda r: pl.BlockSpec((nl, r, LANE), lambda s: (0, s, 0))
    nblk, bs, _ = w_a.shape
    eye = jnp.eye(nblk, dtype=F32)
    bd = lambda w_: jnp.einsum('nkj,nm->nkmj', w_, eye).reshape(W, W)
    wax = jnp.concatenate([bd(w_a), bd(w_x)], axis=1).astype(BF16)
    bax = jnp.concatenate([b_a, b_x]).reshape(1, 2 * W)
    rows = ts * SUBLANE
    const = lambda shape: pl.BlockSpec(shape, lambda s: (0,) * len(shape))
    return pl.pallas_call(
        _lru_kernel,
        grid=(rows_total // rows,),
        in_specs=[tmaj(rows), tmaj(rows),
                  const((CONV_WIDTH, W)), const((1, W)), const((W, 2 * W)), const((1, 2 * W)),
                  const((1, W)), const((1, W))],
        out_specs=tmaj(rows),
        out_shape=jax.ShapeDtypeStruct((nl, rows_total, LANE), F32),
        scratch_shapes=[pltpu.VMEM(((CONV_WIDTH - 1) * SUBLANE, W), F32), pltpu.VMEM((rows, W), F32),
                        pltpu.VMEM((rows, W), F32), pltpu.VMEM((SUBLANE, W), F32)],
        compiler_params=_cp(("arbitrary",)),
        name="rglru",
    )(x_tm, gate_tm, conv_w, conv_b.reshape(1, W), wax, bax, lam.reshape(1, W), norm_g.reshape(1, W))


_VT_ROWS = MLA_V + 2 * SUBLANE


def _mla_prep_tile(pos0, q_a, kv_a, kpe, gq_ref, gkv_ref, wq_ref, wk_ref, wv_ref,
                   gqh_ref, gkh_ref, freq_ref, q_ref, k_ref, v_ref):
    tm = q_a.shape[0]
    pos = (pos0 + lax.broadcasted_iota(jnp.int32, (tm, LANE), 0)).astype(F32)
    ang = pos * freq_ref[...]
    cos_t = jnp.cos(ang)
    sin_t = jnp.sin(ang)
    lane = lax.broadcasted_iota(jnp.int32, (tm, LANE), 1)
    half = MLA_ROPE // 2
    first = (lane >= MLA_NOPE) & (lane < MLA_NOPE + half)
    second = (lane >= MLA_NOPE + half) & (lane < MLA_QK)

    def rope(t):
        rot = jnp.where(first, -pltpu.roll(t, LANE - half, axis=1),
                        jnp.where(second, pltpu.roll(t, half, axis=1), 0.0))
        return t * cos_t + rot * sin_t

    def head_norm(t, g):
        ms = jnp.sum(t * t, axis=-1, keepdims=True) * (1.0 / MLA_QK)
        return t * lax.rsqrt(ms + NORM_EPS) * g

    qn = _rms(q_a, gq_ref[...]).astype(BF16)
    kvn = _rms(kv_a, gkv_ref[...]).astype(BF16)
    q_all = _dot(qn, wq_ref[...])
    k_all = _dot(kvn, wk_ref[...])
    v_all = _dot(kvn, wv_ref[...])
    scale = MLA_QK ** -0.5 * math.log2(math.e)
    ones_row = lax.broadcasted_iota(jnp.int32, (_VT_ROWS, tm), 0) == MLA_V
    for h in range(MLA_HEADS):
        hs = slice(h * LANE, (h + 1) * LANE)
        q_ref[0, h] = (rope(head_norm(q_all[:, hs], gqh_ref[...])) * scale).astype(q_ref.dtype)
        k_ref[0, h] = rope(head_norm(k_all[:, hs] + kpe, gkh_ref[...])).astype(k_ref.dtype)
        v_ref[0, h] = jnp.where(ones_row, 1.0, v_all[:, hs].T[:_VT_ROWS]).astype(v_ref.dtype)


def _mla_prep_kernel(pos_ref, qa_ref, kva_ref, kpe_ref, *rest):
    pos0 = pos_ref[pl.program_id(0)] + pl.program_id(1) * qa_ref.shape[1]
    _mla_prep_tile(pos0, qa_ref[0], kva_ref[0], kpe_ref[0], *rest)


def _mla_prep(q_a, kv_a, kpe, pos_offset, q_norm_g, w_uq, kv_norm_g, w_ukv, q_head_g, k_head_g, tm):
    B, S, QR = q_a.shape
    KVR = kv_a.shape[2]
    H = MLA_HEADS
    wq = jnp.zeros((QR, H, LANE), F32).at[:, :, :MLA_QK].set(w_uq.reshape(QR, H, MLA_QK)).reshape(QR, H * LANE)
    wkv = w_ukv.reshape(KVR, H, MLA_NOPE + MLA_V)
    wk = jnp.zeros((KVR, H, LANE), F32).at[:, :, :MLA_NOPE].set(wkv[:, :, :MLA_NOPE]).reshape(KVR, H * LANE)
    wv = jnp.zeros((KVR, H, LANE), F32).at[:, :, :MLA_V].set(wkv[:, :, MLA_NOPE:]).reshape(KVR, H * LANE)
    pad_g = lambda g: jnp.zeros((1, LANE), F32).at[0, :MLA_QK].set(g)
    half = MLA_ROPE // 2
    inv_freq = np.power(np.float32(ROPE_THETA), -np.arange(half, dtype=np.float32) * np.float32(2.0) / np.float32(MLA_ROPE))
    freq = np.zeros((1, LANE), np.float32)
    freq[0, MLA_NOPE:MLA_NOPE + half] = inv_freq
    freq[0, MLA_NOPE + half:MLA_QK] = inv_freq
    const = lambda shape: pl.BlockSpec(shape, lambda b, s, p: (0,) * len(shape))
    row = lambda n: pl.BlockSpec((1, tm, n), lambda b, s, p: (b, s, 0))
    hout = pl.BlockSpec((1, H, tm, LANE), lambda b, s, p: (b, 0, s, 0))
    grid_spec = pltpu.PrefetchScalarGridSpec(
        num_scalar_prefetch=1, grid=(B, S // tm),
        in_specs=[row(QR), row(KVR), row(LANE), const((1, QR)), const((1, KVR)),
                  const((QR, H * LANE)), const((KVR, H * LANE)), const((KVR, H * LANE)),
                  const((1, LANE)), const((1, LANE)), const((1, LANE))],
        out_specs=[hout, hout, pl.BlockSpec((1, H, _VT_ROWS, tm), lambda b, s, p: (b, 0, 0, s))])
    return pl.pallas_call(
        _mla_prep_kernel,
        grid_spec=grid_spec,
        out_shape=[jax.ShapeDtypeStruct((B, H, S, LANE), BF16)] * 2 + [jax.ShapeDtypeStruct((B, H, _VT_ROWS, S), BF16)],
        compiler_params=_cp(("parallel", "parallel")),
        name="mla_prep",
    )(pos_offset, q_a, kv_a, kpe, q_norm_g.reshape(1, QR), kv_norm_g.reshape(1, KVR),
      wq.astype(BF16), wk.astype(BF16), wv.astype(BF16), pad_g(q_head_g), pad_g(k_head_g), jnp.asarray(freq))


def _attn_kernel(qtab_ref, ktab_ref, q_ref, k_ref, vt_ref, ng_ref, o_ref, m_ref, acc_ref):
    H = q_ref.shape[1]
    tq = q_ref.shape[2]
    j = pl.program_id(1)
    qi = qtab_ref[j]
    ki = ktab_ref[j]

    @pl.when(ki == 0)
    def _():
        m_ref[...] = jnp.full_like(m_ref, -1e30)
        acc_ref[...] = jnp.zeros_like(acc_ref)

    def update(masked):
        if masked:
            kc = lax.broadcasted_iota(jnp.int32, (tq, tq), 0) // ATTN_CHUNK
            qc = lax.broadcasted_iota(jnp.int32, (tq, tq), 1) // ATTN_CHUNK
            visible = kc <= qc
        for h in range(H):
            st = _dot_nt(k_ref[0, h], q_ref[0, h])
            if masked:
                st = jnp.where(visible, st, -1e30)
            m_old = m_ref[h]
            m_new = jnp.maximum(m_old, jnp.max(st, axis=0, keepdims=True))
            alpha = jnp.exp2(m_old - m_new)
            p = jnp.exp2(st - m_new).astype(BF16)
            acc_ref[h] = alpha * acc_ref[h] + _dot(vt_ref[0, h], p)
            m_ref[h] = m_new

    @pl.when(ki < qi)
    def _():
        update(False)

    @pl.when(ki == qi)
    def _():
        update(True)
        ot = jnp.concatenate([acc_ref[h, :MLA_V] / acc_ref[h, MLA_V:MLA_V + 1] for h in range(H)], axis=0)
        o_ref[0] = _rms(ot.T, ng_ref[...]).astype(o_ref.dtype)


def _attention(q, k, vt, norm_g, tq):
    B, H, S, _ = q.shape
    W = H * MLA_V
    nq = S // tq
    pairs = [(i, j) for i in range(nq) for j in range(i + 1)]
    qtab = jnp.asarray([p[0] for p in pairs], jnp.int32)
    ktab = jnp.asarray([p[1] for p in pairs], jnp.int32)
    grid_spec = pltpu.PrefetchScalarGridSpec(
        num_scalar_prefetch=2, grid=(B, len(pairs)),
        in_specs=[pl.BlockSpec((1, H, tq, LANE), lambda b, j, qt, kt: (b, 0, qt[j], 0)),
                  pl.BlockSpec((1, H, tq, LANE), lambda b, j, qt, kt: (b, 0, kt[j], 0)),
                  pl.BlockSpec((1, H, _VT_ROWS, tq), lambda b, j, qt, kt: (b, 0, 0, kt[j])),
                  pl.BlockSpec((1, W), lambda b, j, qt, kt: (0, 0))],
        out_specs=pl.BlockSpec((1, tq, W), lambda b, j, qt, kt: (b, qt[j], 0)),
        scratch_shapes=[pltpu.VMEM((H, 1, tq), F32), pltpu.VMEM((H, _VT_ROWS, tq), F32)])
    return pl.pallas_call(
        _attn_kernel,
        grid_spec=grid_spec,
        out_shape=jax.ShapeDtypeStruct((B, S, W), BF16),
        compiler_params=_cp(("parallel", "arbitrary")),
        name="mla_attn",
    )(qtab, ktab, q, k, vt, norm_g.reshape(1, W))


def _store_token_tiles(ref, val):
    n, d = val.shape
    for s in range(d // LANE):
        ref[pl.ds(s, n, stride=SUBLANE), :] = val[:, s * LANE:(s + 1) * LANE]


def _load_token_tiles(ref, n):
    return jnp.concatenate([ref[pl.ds(s, n, stride=SUBLANE), :] for s in range(SUBLANE)], axis=1)


def _outproj_kernel(x_ref, ya_ref, yb_ref, yc_ref, yd_ref, wo_ref, gt_ref, sc_ref, sh_ref, g_ref,
                    wr_ref, br_ref, xo_ref, h_ref, lg_ref):
    B, ts, D = x_ref.shape
    n = B * ts

    acc = _dot(ya_ref[...].reshape(n, -1), wo_ref[0])
    acc = acc + _dot(_load_batch_major(yb_ref, B).astype(BF16), wo_ref[1])
    acc = acc + _dot(yc_ref[...].reshape(n, -1), wo_ref[2])
    acc = acc + _dot(_load_batch_major(yd_ref, B).astype(BF16), wo_ref[3])
    x = x_ref[...] + gt_ref[...] * acc.reshape(B, ts, D)
    xo_ref[...] = x
    h = (_rms(x, g_ref[...]) * (1.0 + sc_ref[...]) + sh_ref[...]).reshape(n, D)
    for b in range(B):
        _store_token_tiles(h_ref.at[b], h[b * ts:(b + 1) * ts])
    h_hi, h_lo = _split_bf16(h, 2)
    lg = _dot(h_hi, wr_ref[0]) + (_dot(h_lo, wr_ref[0]) + _dot(h_hi, wr_ref[1])) + br_ref[...]
    lg_ref[...] = lg.reshape(B, ts, LANE)


def _outproj(x, ya, yb, yc, yd, w_out, gt, sc, sh, g, w_router, b_router, tm):
    B, S, D = x.shape
    W = D // N_MIX
    ts = tm // B
    bm = lambda n: pl.BlockSpec((B, ts, n), lambda s: (0, s, 0))
    tmaj = pl.BlockSpec((W // LANE, ts * B, LANE), lambda s: (0, s, 0))
    const = lambda shape: pl.BlockSpec(shape, lambda s: (0,) * len(shape))
    vec = const((B, 1, D))
    return pl.pallas_call(
        _outproj_kernel,
        grid=(S // ts,),
        in_specs=[bm(D), bm(W), tmaj, bm(W), tmaj, const((N_MIX, W, D)), vec, vec, vec,
                  const((1, D)), const((2, D, LANE)), const((1, LANE))],
        out_specs=[bm(D), pl.BlockSpec((B, ts * SUBLANE, LANE), lambda s: (0, s, 0)), bm(LANE)],
        out_shape=[jax.ShapeDtypeStruct((B, S, D), F32), jax.ShapeDtypeStruct((B, S * SUBLANE, LANE), F32),
                   jax.ShapeDtypeStruct((B, S, LANE), F32)],
        compiler_params=_cp(("parallel",)),
        name="outproj",
    )(x, ya, yb, yc, yd, w_out.reshape(N_MIX, W, D).astype(BF16), gt, sc, sh, g,
      jnp.stack(_split_bf16(w_router, 2)), b_router)


_MOE_PAD_ROWS = MOE_BLOCK // TOP_K


def _moe_kernel(be_ref, nused_ref, tok_ref, tokn_ref, dst_ref, w_ref, h_hbm, w1_ref, w3_ref, w2_ref, o_hbm,
                xbuf, ybuf, gsem, ssem):
    i = pl.program_id(0)
    R = MOE_BLOCK
    TR = SUBLANE
    nused = nused_ref[0]
    slot = lax.rem(i, 2)
    other = 1 - slot

    def gather_rows(idx_ref, s):
        def body(r, c):
            src = pl.multiple_of(idx_ref[0, 0, r], TR)
            pltpu.make_async_copy(h_hbm.at[pl.ds(src, TR)], xbuf.at[s, pl.ds(pl.multiple_of(r * TR, TR), TR)],
                                  gsem.at[s]).start()
            return c
        lax.fori_loop(0, R, body, 0, unroll=8)

    def scatter_rows(s):
        def body(r, c):
            dst = pl.multiple_of(dst_ref[0, 0, r], TR)
            pltpu.make_async_copy(ybuf.at[s, pl.ds(pl.multiple_of(r * TR, TR), TR)], o_hbm.at[pl.ds(dst, TR)],
                                  ssem.at[s]).start(priority=1)
            return c
        lax.fori_loop(0, R, body, 0, unroll=8)

    def wait_gather(s):
        pltpu.make_async_copy(h_hbm.at[pl.ds(0, R * TR)], xbuf.at[s], gsem.at[s]).wait()

    def wait_scatter(s):
        pltpu.make_async_copy(ybuf.at[s], o_hbm.at[pl.ds(0, R * TR)], ssem.at[s]).wait()

    @pl.when(i == 0)
    def _():
        plane = o_hbm.shape[0] // TOP_K
        npad = _MOE_PAD_ROWS * TR
        ybuf[1] = jnp.zeros(ybuf.shape[1:], ybuf.dtype)
        for j in range(TOP_K):
            pad = pltpu.make_async_copy(ybuf.at[1, pl.ds(0, npad)],
                                        o_hbm.at[pl.ds((j + 1) * plane - npad, npad)], ssem.at[1])
            pad.start()
            pad.wait()
        gather_rows(tok_ref, 0)

    @pl.when(i < nused)
    def _():
        @pl.when(i + 1 < nused)
        def _():
            gather_rows(tokn_ref, other)

        wait_gather(slot)

        @pl.when(i >= 2)
        def _():
            wait_scatter(slot)

        xb = _load_token_tiles(xbuf.at[slot], R).astype(BF16)
        hid = _dot(xb, w1_ref[0])
        hid = hid * jax.nn.sigmoid(hid) * _dot(xb, w3_ref[0])
        y = _dot(hid.astype(BF16), w2_ref[0])
        _store_token_tiles(ybuf.at[slot], y * w_ref[:, 0:1])
        scatter_rows(slot)

        @pl.when(i == nused - 1)
        def _():
            wait_scatter(slot)

            @pl.when(i >= 1)
            def _():
                wait_scatter(other)


def _moe_experts(h2, blk_expert, n_used, row_tok, row_dst, row_w, w1, w3, w2):
    E, D, F = w1.shape
    T = h2.shape[0] // SUBLANE
    nb = blk_expert.shape[0]
    R = MOE_BLOCK
    blk = lambda i, be, nu: (i, 0, 0)
    nxt = lambda i, be, nu: (jnp.minimum(i + 1, nb - 1), 0, 0)
    wsel = lambda i, be, nu: (be[i], 0, 0)
    plane = (T + _MOE_PAD_ROWS) * SUBLANE
    row_tok = (row_tok * SUBLANE).reshape(nb, 1, R)
    row_dst = ((row_dst % TOP_K) * plane + (row_dst // TOP_K) * SUBLANE).reshape(nb, 1, R)
    grid_spec = pltpu.PrefetchScalarGridSpec(
        num_scalar_prefetch=2, grid=(nb,),
        in_specs=[pl.BlockSpec((1, 1, R), blk, memory_space=pltpu.SMEM),
                  pl.BlockSpec((1, 1, R), nxt, memory_space=pltpu.SMEM),
                  pl.BlockSpec((1, 1, R), blk, memory_space=pltpu.SMEM),
                  pl.BlockSpec((R, LANE), lambda i, be, nu: (i, 0)),
                  pl.BlockSpec(memory_space=pl.ANY),
                  pl.BlockSpec((1, D, F), wsel), pl.BlockSpec((1, D, F), wsel), pl.BlockSpec((1, F, D), wsel)],
        out_specs=pl.BlockSpec(memory_space=pl.ANY),
        scratch_shapes=[pltpu.VMEM((2, R * SUBLANE, LANE), F32), pltpu.VMEM((2, R * SUBLANE, LANE), F32),
                        pltpu.SemaphoreType.DMA((2,)), pltpu.SemaphoreType.DMA((2,))])
    return pl.pallas_call(
        _moe_kernel,
        grid_spec=grid_spec,
        out_shape=jax.ShapeDtypeStruct((TOP_K * plane, LANE), F32),
        compiler_params=_cp(("arbitrary",)),
        name="moe_experts",
    )(blk_expert, n_used, row_tok, row_tok, row_dst, jnp.broadcast_to(row_w.reshape(nb * R, 1), (nb * R, LANE)),
      h2, w1, w3, w2)


def _route(logits, T):
    p_group = jax.nn.softmax(logits[:, :N_GROUPS], axis=-1)
    g_sel = jnp.argmax(p_group, axis=-1).astype(jnp.int32)
    g_prob = jnp.max(p_group, axis=-1)
    logit_e = logits[:, N_GROUPS:N_GROUPS + N_EXPERTS].reshape(T, N_GROUPS, EXPERTS_PER_GROUP)
    logit_e = jnp.take_along_axis(logit_e, g_sel[:, None, None], axis=1)[:, 0]
    p_e = jax.nn.softmax(logit_e, axis=-1)
    top_p, top_i = lax.top_k(p_e, TOP_K)
    gate = (g_prob[:, None] * top_p / jnp.sum(top_p, axis=-1, keepdims=True)).reshape(-1)
    expert = (g_sel[:, None] * EXPERTS_PER_GROUP + top_i.astype(jnp.int32)).reshape(-1)
    n_assign = T * TOP_K
    n_blocks = -(-n_assign // MOE_BLOCK) + N_EXPERTS
    order = jnp.argsort(expert).astype(jnp.int32)
    counts = jnp.sum((expert[:, None] == jnp.arange(N_EXPERTS, dtype=jnp.int32)[None, :]).astype(jnp.int32), axis=0)
    start = jnp.cumsum(counts) - counts
    pcounts = (counts + MOE_BLOCK - 1) // MOE_BLOCK * MOE_BLOCK
    pend = jnp.cumsum(pcounts)
    pstart = pend - pcounts
    blk_start = jnp.arange(n_blocks, dtype=jnp.int32) * MOE_BLOCK
    blk_expert = jnp.minimum(jnp.sum((pend[None, :] <= blk_start[:, None]).astype(jnp.int32), axis=1), N_EXPERTS - 1)
    n_used = (pend[-1] // MOE_BLOCK).astype(jnp.int32).reshape(1)
    lane = jnp.arange(MOE_BLOCK, dtype=jnp.int32)[None, :]
    off = blk_start[:, None] + lane - pstart[blk_expert][:, None]
    valid = off < counts[blk_expert][:, None]
    pos = jnp.clip(start[blk_expert][:, None] + off, 0, n_assign - 1)
    asg = order[pos]
    row_tok = jnp.where(valid, asg // TOP_K, 0)
    row_dst = jnp.where(valid, asg, n_assign + lane)
    row_w = jnp.where(valid, gate[asg], 0.0)
    return blk_expert.astype(jnp.int32), n_used, row_tok, row_dst, row_w


def _combine_kernel(x_ref, y0_ref, y1_ref, gt_ref, o_ref):
    tm = x_ref.shape[1]
    y = _load_token_tiles(y0_ref.at[0], tm) + _load_token_tiles(y1_ref.at[0], tm)
    o_ref[0] = x_ref[0] + gt_ref[0] * y


def _combine(x, y2, gt, tm):
    B, S, D = x.shape
    nt = S // tm
    return pl.pallas_call(
        _combine_kernel,
        grid=(B, nt),
        in_specs=[pl.BlockSpec((1, tm, D), lambda b, s: (b, s, 0)),
                  pl.BlockSpec((1, tm * SUBLANE, LANE), lambda b, s: (0, b * nt + s, 0)),
                  pl.BlockSpec((1, tm * SUBLANE, LANE), lambda b, s: (1, b * nt + s, 0)),
                  pl.BlockSpec((1, 1, D), lambda b, s: (b, 0, 0))],
        out_specs=pl.BlockSpec((1, tm, D), lambda b, s: (b, s, 0)),
        out_shape=jax.ShapeDtypeStruct((B, S, D), F32),
        compiler_params=_cp(("parallel", "parallel")),
        name="moe_combine",
    )(x, y2, y2, gt)


def _tiles(S):
    big = S >= 2048
    return dict(tm=1024 if big else 256, tl=1024 if big else 256, ts_s5=64, ts_lru=128 if big else 64,
                tq=1024 if big else 256)


def kernel(x, c, pos_offset, ada_w, ada_b, norm1_g, w_in, rwkv_mu, rwkv_w0, rwkv_w2, rwkv_a0, rwkv_a2, rwkv_g2, rwkv_k_k, rwkv_k_a, rwkv_r_k, rwkv_ln_w, rwkv_ln_b, s5_lambda_re, s5_lambda_im, s5_b_re, s5_b_im, s5_c_re, s5_c_im, s5_d, s5_log_dt, s5_glu_w, s5_glu_b, mla_q_norm_g, mla_w_uq, mla_kv_norm_g, mla_w_ukv, mla_q_head_g, mla_k_head_g, lru_conv_w, lru_conv_b, lru_w_a, lru_b_a, lru_w_x, lru_b_x, lru_lambda, branch_norm_g, w_out, norm2_g, moe_w_group, moe_b_group, moe_w_expert, moe_b_expert, moe_w1, moe_w3, moe_w2):
    B, S, D = x.shape
    depth = ada_w.shape[0]
    T = B * S
    W = D // N_MIX
    assert B == SUBLANE, "time-major scans put the batch on the 8 sublanes"
    assert D == SUBLANE * LANE, "token-tile layout: one (8, 128) tile per token row"
    tiles = _tiles(S)
    mod = _adaln_mod(c, ada_w, ada_b)

    for l in range(depth):
        sh1, sc1, gt1, sh2, sc2, gt2 = [mod[l, :, None, i * D:(i + 1) * D] for i in range(6)]
        z_rwkv, z_s5, q_a, kv_a, kpe, z_lru, z_gate = _inproj(
            x, sc1, sh1, norm1_g[l].reshape(1, D), _pad_w_in(w_in[l]), tiles['tm'])
        y_a = _rwkv(z_rwkv, rwkv_mu[l], rwkv_w0[l], rwkv_w2[l], rwkv_a0[l], rwkv_a2[l], rwkv_g2[l],
                    rwkv_k_k[l], rwkv_k_a[l], rwkv_r_k[l].reshape(-1), rwkv_ln_w[l], rwkv_ln_b[l], tiles['tl'])
        y_b = _s5(z_s5, s5_lambda_re[l], s5_lambda_im[l], s5_b_re[l], s5_b_im[l], s5_c_re[l],
                  s5_c_im[l], s5_d[l], s5_log_dt[l], s5_glu_w[l], s5_glu_b[l], branch_norm_g[l, 0],
                  tiles['ts_s5'])
        q, k, vt = _mla_prep(q_a, kv_a, kpe, pos_offset, mla_q_norm_g[l], mla_w_uq[l], mla_kv_norm_g[l],
                             mla_w_ukv[l], mla_q_head_g[l], mla_k_head_g[l], tiles['tm'])
        y_c = _attention(q, k, vt, branch_norm_g[l, 1], tiles['tq'])
        y_d = _lru(z_lru, z_gate, lru_conv_w[l], lru_conv_b[l], lru_w_a[l], lru_b_a[l],
                   lru_w_x[l], lru_b_x[l], lru_lambda[l], branch_norm_g[l, 2], tiles['ts_lru'])
        w_router = jnp.zeros((D, LANE), F32).at[:, :N_GROUPS].set(moe_w_group[l])
        w_router = w_router.at[:, N_GROUPS:N_GROUPS + N_EXPERTS].set(moe_w_expert[l])
        b_router = jnp.zeros((1, LANE), F32).at[0, :N_GROUPS].set(moe_b_group[l])
        b_router = b_router.at[0, N_GROUPS:N_GROUPS + N_EXPERTS].set(moe_b_expert[l])
        x, h2, logits = _outproj(x, y_a, y_b, y_c, y_d, w_out[l], gt1, sc2, sh2, norm2_g[l].reshape(1, D),
                                 w_router, b_router, tiles['tm'])
        blk_expert, n_used, row_tok, row_dst, row_w = _route(logits.reshape(T, LANE), T)
        y2 = _moe_experts(h2.reshape(T * SUBLANE, LANE), blk_expert, n_used, row_tok, row_dst, row_w,
                          moe_w1[l].astype(BF16), moe_w3[l].astype(BF16), moe_w2[l].astype(BF16))
        x = _combine(x, y2.reshape(TOP_K, -1, LANE), gt2, tiles['tm'])
    return x
```

```python
import functools
import math

import numpy as np
import jax
import jax.numpy as jnp
from jax import lax
from jax.experimental import pallas as pl
from jax.experimental.pallas import tpu as pltpu

F32 = jnp.float32
BF16 = jnp.bfloat16
HI = lax.Precision.HIGHEST

N_MIX = 4
RWKV_HEAD = 64
RWKV_W_RANK, RWKV_A_RANK, RWKV_G_RANK = 32, 32, 64
RWKV_LN_EPS = 64e-5
S5_GROUP, S5_STATE = 16, 64
MLA_HEADS, MLA_NOPE, MLA_ROPE, MLA_V = 4, 64, 32, 64
MLA_QK = MLA_NOPE + MLA_ROPE
ROPE_THETA = 10000.0
LRU_C = 8.0
CONV_WIDTH = 4
ATTN_CHUNK = 64
N_GROUPS, EXPERTS_PER_GROUP, TOP_K = 4, 8, 2
N_EXPERTS = N_GROUPS * EXPERTS_PER_GROUP
MOE_BLOCK = 256
NORM_EPS = 1e-6

LANE = 128
SUBLANE = 8
RWKV_CHUNK = 64
VMEM_LIMIT = 48 * 1024 * 1024


def _cp(sem, vmem=VMEM_LIMIT):
    return pltpu.CompilerParams(dimension_semantics=sem, vmem_limit_bytes=vmem)


def _dot(a, b):
    return jnp.dot(a, b, preferred_element_type=F32)


def _dot_hi(a, b):
    return jnp.dot(a, b, precision=HI, preferred_element_type=F32)


def _dot_nt(a, b):
    return lax.dot_general(a, b, (((1,), (1,)), ((), ())), preferred_element_type=F32)


def _dot_nt_hi(a, b):
    return lax.dot_general(a, b, (((1,), (1,)), ((), ())), precision=HI, preferred_element_type=F32)


def _softplus(x):
    return jnp.maximum(x, 0.0) + jnp.log1p(jnp.exp(-jnp.abs(x)))


def _rms(x, g, eps=NORM_EPS):
    return x * lax.rsqrt(jnp.mean(x * x, axis=-1, keepdims=True) + eps) * g


def _mod_kernel(c_ref, w_ref, b_ref, o_ref):
    c = c_ref[...]
    cond = c * jax.nn.sigmoid(c)
    o_ref[0] = _dot_hi(cond, w_ref[0]) + b_ref[0]


def _adaln_mod(c, ada_w, ada_b):
    L, D, D6 = ada_w.shape
    B = c.shape[0]
    nj = D6 // D
    return pl.pallas_call(
        _mod_kernel,
        grid=(L, nj),
        in_specs=[pl.BlockSpec((B, D), lambda l, j: (0, 0)),
                  pl.BlockSpec((1, D, D), lambda l, j: (l, 0, j)),
                  pl.BlockSpec((1, 1, D), lambda l, j: (l, 0, j))],
        out_specs=pl.BlockSpec((1, B, D), lambda l, j: (l, 0, j)),
        out_shape=jax.ShapeDtypeStruct((L, B, D6), F32),
        compiler_params=_cp(("parallel", "parallel")),
        name="adaln_mod",
    )(c, ada_w, ada_b.reshape(L, 1, D6))


_RWKV_IN = 3 * 256 + RWKV_W_RANK + RWKV_A_RANK + RWKV_G_RANK
_IN_COLS = (_RWKV_IN, 256, 256, 128, LANE, 256, 256)
_IN_OFFS = tuple(int(v) for v in np.cumsum((0,) + _IN_COLS))
_KPE_LANE0 = MLA_NOPE


def _pad_w_in(w_in):
    D = w_in.shape[0]
    o = np.cumsum((0, _RWKV_IN, 256, 256, 128, MLA_ROPE, 256, 256))
    pieces = [w_in[:, o[i]:o[i + 1]] for i in range(7)]
    kpe = jnp.zeros((D, LANE), w_in.dtype).at[:, _KPE_LANE0:_KPE_LANE0 + MLA_ROPE].set(pieces[4])
    pieces[4] = kpe
    return jnp.concatenate(pieces, axis=1).astype(BF16)


_IN_TIME_MAJOR = (1, 5, 6)


def _store_time_major(ref, val, nb):
    ts = val.shape[0] // nb
    for j in range(val.shape[1] // LANE):
        for b in range(nb):
            ref[j, pl.ds(b, ts, stride=nb), :] = val[b * ts:(b + 1) * ts, j * LANE:(j + 1) * LANE]


def _load_batch_major(ref, nb):
    ts = ref.shape[1] // nb
    return jnp.concatenate(
        [jnp.concatenate([ref[j, pl.ds(b, ts, stride=nb), :] for b in range(nb)], axis=0)
         for j in range(ref.shape[0])], axis=1)


def _lanes(ref):
    return jnp.concatenate([ref[j] for j in range(ref.shape[0])], axis=1)


def _store_lanes(ref, val):
    for j in range(ref.shape[0]):
        ref[j] = val[:, j * LANE:(j + 1) * LANE].astype(ref.dtype)


def _inproj_kernel(x_ref, sc_ref, sh_ref, g_ref, w_ref, *outs):
    B, ts, D = x_ref.shape
    h = _rms(x_ref[...], g_ref[...]) * (1.0 + sc_ref[...]) + sh_ref[...]
    z = _dot(h.reshape(B * ts, D).astype(BF16), w_ref[...])
    for i, o_ref in enumerate(outs):
        piece = z[:, _IN_OFFS[i]:_IN_OFFS[i + 1]]
        if i in _IN_TIME_MAJOR:
            _store_time_major(o_ref, piece, B)
        else:
            o_ref[...] = piece.reshape(B, ts, piece.shape[1])


def _inproj(x, sc, sh, g, w_pad, tm):
    B, S, D = x.shape
    ts = tm // B
    bm = lambda n: pl.BlockSpec((B, ts, n), lambda s: (0, s, 0))
    const = lambda shape: pl.BlockSpec(shape, lambda s: (0,) * len(shape))
    return pl.pallas_call(
        _inproj_kernel,
        grid=(S // ts,),
        in_specs=[bm(D), const((B, 1, D)), const((B, 1, D)), const((1, D)), const(w_pad.shape)],
        out_specs=[pl.BlockSpec((n // LANE, ts * B, LANE), lambda s: (0, s, 0)) if i in _IN_TIME_MAJOR else bm(n)
                   for i, n in enumerate(_IN_COLS)],
        out_shape=[jax.ShapeDtypeStruct((n // LANE, S * B, LANE) if i in _IN_TIME_MAJOR else (B, S, n), F32)
                   for i, n in enumerate(_IN_COLS)],
        compiler_params=_cp(("parallel",)),
        name="inproj",
    )(x, sc, sh, g, w_pad)


def _split_bf16(x, parts):
    out = []
    for _ in range(parts - 1):
        hi = x.astype(BF16)
        out.append(hi)
        x = x - hi.astype(F32)
    out.append(x.astype(BF16))
    return out


def _rwkv_kernel(z_ref, mu_ref, w0_ref, a0_ref, wl_ref, kk_ref, ka_ref, rk_ref, lnw_ref, lnb_ref,
                 hsum_ref, tri_ref, o_ref, zprev_ref, g_ref, y_ref):
    W = 256
    NH = W // RWKV_HEAD
    L = RWKV_CHUNK
    TL = z_ref.shape[1]

    @pl.when(pl.program_id(1) == 0)
    def _():
        zprev_ref[...] = jnp.zeros_like(zprev_ref)
        g_ref[...] = jnp.zeros_like(g_ref)

    hsum = hsum_ref[...]

    def head_sum(t):
        hi, lo_ = _split_bf16(t, 2)
        return _dot(hi, hsum) + _dot(lo_, hsum)

    z = z_ref[0]
    rows = lax.broadcasted_iota(jnp.int32, z.shape, 0)
    zp = jnp.where(rows == 0, zprev_ref[...], pltpu.roll(z, 1, axis=0))
    zprev_ref[...] = z[TL - 1:TL, :]
    zs = z + (zp - z) * mu_ref[...]
    r, k, v, lo = zs[:, 0:W], zs[:, W:2 * W], zs[:, 2 * W:3 * W], zs[:, 3 * W:3 * W + LANE]
    lane = lax.broadcasted_iota(jnp.int32, lo.shape, 1)
    act = jnp.where(lane < RWKV_W_RANK, jnp.tanh(lo),
                    jnp.where(lane < RWKV_W_RANK + RWKV_A_RANK, lo, jax.nn.sigmoid(lo)))
    up = _dot(act.astype(BF16), wl_ref[...])
    w = -_softplus(-(w0_ref[...] + up[:, 0:W])) - 0.5
    ld = -jnp.exp(w)
    a = jax.nn.sigmoid(a0_ref[...] + up[:, W:2 * W])
    g = up[:, 2 * W:3 * W]
    kk = k * kk_ref[...]
    kk = kk / jnp.maximum(jnp.sqrt(head_sum(kk * kk)), 1e-12)
    k2 = k * (1.0 + (a - 1.0) * ka_ref[...])
    kka = kk * a

    tri_b = tri_ref[0]
    tri_incl = tri_b > 0
    tri_strict = tri_ref[1] > 0
    eye = (lax.broadcasted_iota(jnp.int32, (RWKV_HEAD, RWKV_HEAD), 0)
           == lax.broadcasted_iota(jnp.int32, (RWKV_HEAD, RWKV_HEAD), 1)).astype(BF16)

    NC = TL // L
    pairs = [(j, h) for j in range(NC) for h in range(NH)]
    hsl = [slice(h * RWKV_HEAD, (h + 1) * RWKV_HEAD) for h in range(NH)]
    csl = [slice(j * L, (j + 1) * L) for j in range(NC)]
    bt, rt, a_s, k_s, a_e, k_e, v_c, p_l = [], [], [], [], [], [], [], []
    for j in range(NC):
        ld_c = ld[csl[j]]
        cs = sum(_dot(tri_b, p) for p in _split_bf16(ld_c, 3))
        cs_l = cs[L - 1:L, :]
        p_inv = jnp.exp(-cs)
        p_end = jnp.exp(cs_l - cs)
        p_l.append(jnp.exp(cs_l))
        bt.append(-kk[csl[j]] * jnp.exp(cs - ld_c))
        rt.append(r[csl[j]] * jnp.exp(cs))
        a_s.append((kka[csl[j]] * p_inv).astype(BF16))
        k_s.append((k2[csl[j]] * p_inv).astype(BF16))
        a_e.append((kka[csl[j]] * p_end).astype(BF16))
        k_e.append((k2[csl[j]] * p_end).astype(BF16))
        v_c.append(v[csl[j]].astype(BF16))

    x = [jnp.concatenate([bt[j][:, hsl[h]], rt[j][:, hsl[h]]], axis=0).astype(BF16) for j, h in pairs]
    ga = [_dot_nt(x[i], a_s[j][:, hsl[h]]) for i, (j, h) in enumerate(pairs)]
    gk = [_dot_nt(x[i], k_s[j][:, hsl[h]]) for i, (j, h) in enumerate(pairs)]
    m_ba = [jnp.where(tri_strict, t[:L], 0.0).astype(BF16) for t in ga]
    m_ra = [jnp.where(tri_incl, t[L:], 0.0).astype(BF16) for t in ga]
    m_bk = [jnp.where(tri_strict, t[:L], 0.0).astype(BF16) for t in gk]
    m_rk = [jnp.where(tri_incl, t[L:], 0.0).astype(BF16) for t in gk]
    v_h = [v_c[j][:, hsl[h]] for j, h in pairs]
    wn = [jnp.concatenate([bt[j][:, hsl[h]], _dot(m_bk[i], v_h[i])], axis=1) for i, (j, h) in enumerate(pairs)]
    n_pow = m_ba
    wn = [w_ + _dot(n_, w_.astype(BF16)) for w_, n_ in zip(wn, n_pow)]
    for _ in range(5):
        n_pow = [_dot(n_, n_).astype(BF16) for n_ in n_pow]
        wn = [w_ + _dot(n_, w_.astype(BF16)) for w_, n_ in zip(wn, n_pow)]
    wn_b = [w_.astype(BF16) for w_ in wn]
    mw = [_dot(m_, w_) for m_, w_ in zip(m_ra, wn_b)]
    r2 = [(rt[j][:, hsl[h]] + mw[i][:, :RWKV_HEAD]).astype(BF16) for i, (j, h) in enumerate(pairs)]
    y0 = [mw[i][:, RWKV_HEAD:] + _dot(m_rk[i], v_h[i]) for i in range(len(pairs))]
    b2_t = [_dot_nt(eye, w_[:, :RWKV_HEAD]).astype(BF16) for w_ in wn_b]
    c_m = [_dot(b2_t[i], a_e[j][:, hsl[h]]).astype(BF16) for i, (j, h) in enumerate(pairs)]
    uv_t = [_dot_nt(eye, jnp.concatenate([wn_b[i][:, RWKV_HEAD:], v_h[i]], axis=0)).astype(BF16)
            for i in range(len(pairs))]
    d_t = [_dot(uv_t[i], jnp.concatenate([a_e[j][:, hsl[h]], k_e[j][:, hsl[h]]], axis=0))
           for i, (j, h) in enumerate(pairs)]
    for i, (j, h) in enumerate(pairs):
        g0 = g_ref[h]
        g_hi, g_lo = _split_bf16(g0, 2)
        y_ref[csl[j], hsl[h]] = _dot_nt(r2[i], g_hi) + _dot_nt(r2[i], g_lo) + y0[i]
        g_ref[h] = g0 * p_l[j][:, hsl[h]] + _dot(g_hi, c_m[i]) + _dot(g_lo, c_m[i]) + d_t[i]

    y = y_ref[...]
    inv_n = 1.0 / RWKV_HEAD
    mean = head_sum(y) * inv_n
    yc = y - mean
    var = head_sum(yc * yc) * inv_n
    yn = yc * lax.rsqrt(var + RWKV_LN_EPS) * lnw_ref[...] + lnb_ref[...]
    bonus = head_sum(r * k2 * rk_ref[...]) * v
    o_ref[0] = ((yn + bonus) * g).astype(o_ref.dtype)


def _rwkv(z, mu, w0, w2, a0, a2, g2, k_k, k_a, r_k, ln_w, ln_b, tl):
    B, S, _ = z.shape
    W = 256
    L = RWKV_CHUNK
    wl = jnp.zeros((LANE, 3 * W), F32)
    wl = wl.at[0:RWKV_W_RANK, 0:W].set(w2)
    wl = wl.at[RWKV_W_RANK:RWKV_W_RANK + RWKV_A_RANK, W:2 * W].set(a2)
    wl = wl.at[RWKV_W_RANK + RWKV_A_RANK:LANE, 2 * W:3 * W].set(g2).astype(BF16)
    hid = np.arange(W) // RWKV_HEAD
    hsum = jnp.asarray((hid[:, None] == hid[None, :]).astype(np.float32), dtype=BF16)
    t = np.arange(L)
    tri = jnp.asarray(np.stack([(t[None, :] <= t[:, None]), (t[None, :] < t[:, None])]).astype(np.float32), dtype=BF16)
    row2 = lambda v_: v_.reshape(1, -1)
    const = lambda shape: pl.BlockSpec(shape, lambda b, s: (0,) * len(shape))
    return pl.pallas_call(
        _rwkv_kernel,
        grid=(B, S // tl),
        in_specs=[pl.BlockSpec((1, tl, _RWKV_IN), lambda b, s: (b, s, 0)),
                  const((1, _RWKV_IN)), const((1, W)), const((1, W)), const((LANE, 3 * W)),
                  const((1, W)), const((1, W)), const((1, W)), const((1, W)), const((1, W)),
                  const((W, W)), const((2, L, L))],
        out_specs=pl.BlockSpec((1, tl, W), lambda b, s: (b, s, 0)),
        out_shape=jax.ShapeDtypeStruct((B, S, W), BF16),
        scratch_shapes=[pltpu.VMEM((1, _RWKV_IN), F32),
                        pltpu.VMEM((W // RWKV_HEAD, RWKV_HEAD, RWKV_HEAD), F32),
                        pltpu.VMEM((tl, W), F32)],
        compiler_params=_cp(("parallel", "arbitrary")),
        name="rwkv7",
    )(z, row2(mu), row2(w0), row2(a0), wl, row2(k_k), row2(k_a), row2(r_k), row2(ln_w), row2(ln_b), hsum, tri)


def _s5_kernel(u_ref, wb_ref, are_ref, aim_ref, wc_ref, d_ref, gw_ref, gb_ref, ng_ref, o_ref,
               hre_ref, him_ref, sre_ref, sim_ref):
    nb = SUBLANE
    rows = u_ref.shape[1]
    P = are_ref.shape[1]

    @pl.when(pl.program_id(0) == 0)
    def _():
        sre_ref[...] = jnp.zeros_like(sre_ref)
        sim_ref[...] = jnp.zeros_like(sim_ref)

    u = _lanes(u_ref)
    bu = _dot(u.astype(BF16), wb_ref[...])
    hre_ref[...] = bu[:, :P]
    him_ref[...] = bu[:, P:]
    a_re = jnp.broadcast_to(are_ref[...], (nb, P))
    a_im = jnp.broadcast_to(aim_ref[...], (nb, P))

    def step(t, carry):
        h_re, h_im = carry
        i = pl.multiple_of(t * nb, nb)
        n_re = a_re * h_re - a_im * h_im + hre_ref[pl.ds(i, nb), :]
        n_im = a_re * h_im + a_im * h_re + him_ref[pl.ds(i, nb), :]
        hre_ref[pl.ds(i, nb), :] = n_re
        him_ref[pl.ds(i, nb), :] = n_im
        return n_re, n_im

    h_re, h_im = lax.fori_loop(0, rows // nb, step, (sre_ref[...], sim_ref[...]))
    sre_ref[...] = h_re
    sim_ref[...] = h_im
    y = _dot(hre_ref[...].astype(BF16), wc_ref[0]) + _dot(him_ref[...].astype(BF16), wc_ref[1])
    y = jax.nn.gelu(y + d_ref[...] * u)
    y = y * jax.nn.sigmoid(_dot(y.astype(BF16), gw_ref[...]) + gb_ref[...])
    _store_lanes(o_ref, _rms(y, ng_ref[...]))


def _s5(u_tm, lam_re, lam_im, b_re, b_im, c_re, c_im, d, log_dt, glu_w, glu_b, norm_g, ts):
    nl, rows_total, _ = u_tm.shape
    W = nl * LANE
    tmaj = lambda r: pl.BlockSpec((nl, r, LANE), lambda s: (0, s, 0))
    G, P = lam_re.shape
    dt = jnp.exp(log_dt)[:, None]
    mag = jnp.exp(lam_re * dt)
    a_re = mag * jnp.cos(lam_im * dt)
    a_im = mag * jnp.sin(lam_im * dt)
    den = lam_re * lam_re + lam_im * lam_im
    q_re = ((a_re - 1.0) * lam_re + a_im * lam_im) / den
    q_im = (a_im * lam_re - (a_re - 1.0) * lam_im) / den
    bb_re = q_re[..., None] * b_re - q_im[..., None] * b_im
    bb_im = q_re[..., None] * b_im + q_im[..., None] * b_re
    eye = jnp.eye(G, dtype=F32)
    wb_re = jnp.einsum('gpi,gh->gihp', bb_re, eye).reshape(W, G * P)
    wb_im = jnp.einsum('gpi,gh->gihp', bb_im, eye).reshape(W, G * P)
    wb = jnp.concatenate([wb_re, wb_im], axis=1).astype(BF16)
    wc_re = jnp.einsum('gip,gh->gphi', c_re, eye).reshape(G * P, W)
    wc_im = jnp.einsum('gip,gh->gphi', -c_im, eye).reshape(G * P, W)
    wc = jnp.stack([wc_re, wc_im]).astype(BF16)
    GP = G * P
    rows = ts * SUBLANE
    const = lambda shape: pl.BlockSpec(shape, lambda s: (0,) * len(shape))
    return pl.pallas_call(
        _s5_kernel,
        grid=(rows_total // rows,),
        in_specs=[tmaj(rows),
                  const((W, 2 * GP)), const((1, GP)), const((1, GP)), const((2, GP, W)),
                  const((1, W)), const((W, W)), const((1, W)), const((1, W))],
        out_specs=tmaj(rows),
        out_shape=jax.ShapeDtypeStruct((nl, rows_total, LANE), F32),
        scratch_shapes=[pltpu.VMEM((rows, GP), F32), pltpu.VMEM((rows, GP), F32),
                        pltpu.VMEM((SUBLANE, GP), F32), pltpu.VMEM((SUBLANE, GP), F32)],
        compiler_params=_cp(("arbitrary",)),
        name="s5",
    )(u_tm, wb, a_re.reshape(1, GP), a_im.reshape(1, GP), wc, d.reshape(1, W), glu_w.astype(BF16),
      glu_b.reshape(1, W), norm_g.reshape(1, W))


def _lru_kernel(x_ref, gate_ref, cw_ref, cb_ref, wax_ref, bax_ref, lam_ref, ng_ref, o_ref,
                xc_ref, a_ref, b_ref, hs_ref):
    nb = SUBLANE
    rows = x_ref.shape[1]
    W = x_ref.shape[0] * LANE
    halo = (CONV_WIDTH - 1) * nb

    @pl.when(pl.program_id(0) == 0)
    def _():
        xc_ref[...] = jnp.zeros_like(xc_ref)
        hs_ref[...] = jnp.zeros_like(hs_ref)

    x = _lanes(x_ref)
    xf = jnp.concatenate([xc_ref[...], x], axis=0)
    xc_ref[...] = x[rows - halo:, :]
    xc = cb_ref[...]
    for j in range(CONV_WIDTH):
        xc = xc + cw_ref[j:j + 1, :] * xf[j * nb:j * nb + rows, :]
    gates = _dot(xc.astype(BF16), wax_ref[...]) + bax_ref[...]
    r = jax.nn.sigmoid(gates[:, :W])
    i = jax.nn.sigmoid(gates[:, W:])
    log_a = -LRU_C * r * _softplus(-lam_ref[...])
    a = jnp.exp(log_a)
    a_ref[...] = a
    b_ref[...] = jnp.sqrt(1.0 - jnp.exp(2.0 * log_a)) * (i * xc)

    def step(t, h):
        i0 = pl.multiple_of(t * nb, nb)
        h = a_ref[pl.ds(i0, nb), :] * h + b_ref[pl.ds(i0, nb), :]
        b_ref[pl.ds(i0, nb), :] = h
        return h

    hs_ref[...] = lax.fori_loop(0, rows // nb, step, hs_ref[...])
    y = b_ref[...] * jax.nn.gelu(_lanes(gate_ref))
    _store_lanes(o_ref, _rms(y, ng_ref[...]))


def _lru(x_tm, gate_tm, conv_w, conv_b, w_a, b_a, w_x, b_x, lam, norm_g, ts):
    nl, rows_total, _ = x_tm.shape
    W = nl * LANE
    tmaj = lambda r: pl.BlockSpec((nl, r, LANE), lambda s: (0, s, 0))
    nblk, bs, _ = w_a.shape
    eye = jnp.eye(nblk, dtype=F32)
    bd = lambda w_: jnp.einsum('nkj,nm->nkmj', w_, eye).reshape(W, W)
    wax = jnp.concatenate([bd(w_a), bd(w_x)], axis=1).astype(BF16)
    bax = jnp.concatenate([b_a, b_x]).reshape(1, 2 * W)
    rows = ts * SUBLANE
    const = lambda shape: pl.BlockSpec(shape, lambda s: (0,) * len(shape))
    return pl.pallas_call(
        _lru_kernel,
        grid=(rows_total // rows,),
        in_specs=[tmaj(rows), tmaj(rows),
                  const((CONV_WIDTH, W)), const((1, W)), const((W, 2 * W)), const((1, 2 * W)),
                  const((1, W)), const((1, W))],
        out_specs=tmaj(rows),
        out_shape=jax.ShapeDtypeStruct((nl, rows_total, LANE), F32),
        scratch_shapes=[pltpu.VMEM(((CONV_WIDTH - 1) * SUBLANE, W), F32), pltpu.VMEM((rows, W), F32),
                        pltpu.VMEM((rows, W), F32), pltpu.VMEM((SUBLANE, W), F32)],
        compiler_params=_cp(("arbitrary",)),
        name="rglru",
    )(x_tm, gate_tm, conv_w, conv_b.reshape(1, W), wax, bax, lam.reshape(1, W), norm_g.reshape(1, W))


_VT_ROWS = MLA_V + 2 * SUBLANE


def _mla_prep_tile(pos0, q_a, kv_a, kpe, gq_ref, gkv_ref, wq_ref, wk_ref, wv_ref,
                   gqh_ref, gkh_ref, freq_ref, q_ref, k_ref, v_ref):
    tm = q_a.shape[0]
    pos = (pos0 + lax.broadcasted_iota(jnp.int32, (tm, LANE), 0)).astype(F32)
    ang = pos * freq_ref[...]
    cos_t = jnp.cos(ang)
    sin_t = jnp.sin(ang)
    lane = lax.broadcasted_iota(jnp.int32, (tm, LANE), 1)
    half = MLA_ROPE // 2
    first = (lane >= MLA_NOPE) & (lane < MLA_NOPE + half)
    second = (lane >= MLA_NOPE + half) & (lane < MLA_QK)

    def rope(t):
        rot = jnp.where(first, -pltpu.roll(t, LANE - half, axis=1),
                        jnp.where(second, pltpu.roll(t, half, axis=1), 0.0))
        return t * cos_t + rot * sin_t

    def head_norm(t, g):
        ms = jnp.sum(t * t, axis=-1, keepdims=True) * (1.0 / MLA_QK)
        return t * lax.rsqrt(ms + NORM_EPS) * g

    qn = _rms(q_a, gq_ref[...]).astype(BF16)
    kvn = _rms(kv_a, gkv_ref[...]).astype(BF16)
    q_all = _dot(qn, wq_ref[...])
    k_all = _dot(kvn, wk_ref[...])
    v_all = _dot(kvn, wv_ref[...])
    scale = MLA_QK ** -0.5 * math.log2(math.e)
    ones_row = lax.broadcasted_iota(jnp.int32, (_VT_ROWS, tm), 0) == MLA_V
    for h in range(MLA_HEADS):
        hs = slice(h * LANE, (h + 1) * LANE)
        q_ref[0, h] = (rope(head_norm(q_all[:, hs], gqh_ref[...])) * scale).astype(q_ref.dtype)
        k_ref[0, h] = rope(head_norm(k_all[:, hs] + kpe, gkh_ref[...])).astype(k_ref.dtype)
        v_ref[0, h] = jnp.where(ones_row, 1.0, v_all[:, hs].T[:_VT_ROWS]).astype(v_ref.dtype)


def _mla_prep_kernel(pos_ref, qa_ref, kva_ref, kpe_ref, *rest):
    pos0 = pos_ref[pl.program_id(0)] + pl.program_id(1) * qa_ref.shape[1]
    _mla_prep_tile(pos0, qa_ref[0], kva_ref[0], kpe_ref[0], *rest)


def _mla_prep(q_a, kv_a, kpe, pos_offset, q_norm_g, w_uq, kv_norm_g, w_ukv, q_head_g, k_head_g, tm):
    B, S, QR = q_a.shape
    KVR = kv_a.shape[2]
    H = MLA_HEADS
    wq = jnp.zeros((QR, H, LANE), F32).at[:, :, :MLA_QK].set(w_uq.reshape(QR, H, MLA_QK)).reshape(QR, H * LANE)
    wkv = w_ukv.reshape(KVR, H, MLA_NOPE + MLA_V)
    wk = jnp.zeros((KVR, H, LANE), F32).at[:, :, :MLA_NOPE].set(wkv[:, :, :MLA_NOPE]).reshape(KVR, H * LANE)
    wv = jnp.zeros((KVR, H, LANE), F32).at[:, :, :MLA_V].set(wkv[:, :, MLA_NOPE:]).reshape(KVR, H * LANE)
    pad_g = lambda g: jnp.zeros((1, LANE), F32).at[0, :MLA_QK].set(g)
    half = MLA_ROPE // 2
    inv_freq = np.power(np.float32(ROPE_THETA), -np.arange(half, dtype=np.float32) * np.float32(2.0) / np.float32(MLA_ROPE))
    freq = np.zeros((1, LANE), np.float32)
    freq[0, MLA_NOPE:MLA_NOPE + half] = inv_freq
    freq[0, MLA_NOPE + half:MLA_QK] = inv_freq
    const = lambda shape: pl.BlockSpec(shape, lambda b, s, p: (0,) * len(shape))
    row = lambda n: pl.BlockSpec((1, tm, n), lambda b, s, p: (b, s, 0))
    hout = pl.BlockSpec((1, H, tm, LANE), lambda b, s, p: (b, 0, s, 0))
    grid_spec = pltpu.PrefetchScalarGridSpec(
        num_scalar_prefetch=1, grid=(B, S // tm),
        in_specs=[row(QR), row(KVR), row(LANE), const((1, QR)), const((1, KVR)),
                  const((QR, H * LANE)), const((KVR, H * LANE)), const((KVR, H * LANE)),
                  const((1, LANE)), const((1, LANE)), const((1, LANE))],
        out_specs=[hout, hout, pl.BlockSpec((1, H, _VT_ROWS, tm), lambda b, s, p: (b, 0, 0, s))])
    return pl.pallas_call(
        _mla_prep_kernel,
        grid_spec=grid_spec,
        out_shape=[jax.ShapeDtypeStruct((B, H, S, LANE), BF16)] * 2 + [jax.ShapeDtypeStruct((B, H, _VT_ROWS, S), BF16)],
        compiler_params=_cp(("parallel", "parallel")),
        name="mla_prep",
    )(pos_offset, q_a, kv_a, kpe, q_norm_g.reshape(1, QR), kv_norm_g.reshape(1, KVR),
      wq.astype(BF16), wk.astype(BF16), wv.astype(BF16), pad_g(q_head_g), pad_g(k_head_g), jnp.asarray(freq))


def _attn_kernel(qtab_ref, ktab_ref, q_ref, k_ref, vt_ref, ng_ref, o_ref, m_ref, acc_ref):
    H = q_ref.shape[1]
    tq = q_ref.shape[2]
    j = pl.program_id(1)
    qi = qtab_ref[j]
    ki = ktab_ref[j]

    @pl.when(ki == 0)
    def _():
        m_ref[...] = jnp.full_like(m_ref, -1e30)
        acc_ref[...] = jnp.zeros_like(acc_ref)

    def update(masked):
        if masked:
            kc = lax.broadcasted_iota(jnp.int32, (tq, tq), 0) // ATTN_CHUNK
            qc = lax.broadcasted_iota(jnp.int32, (tq, tq), 1) // ATTN_CHUNK
            visible = kc <= qc
        nxt = _dot_nt(k_ref[0, 0], q_ref[0, 0])
        for h in range(H):
            st = nxt
            if h + 1 < H:
                nxt = _dot_nt(k_ref[0, h + 1], q_ref[0, h + 1])
            if masked:
                st = jnp.where(visible, st, -1e30)
            m_old = m_ref[h]
            m_new = jnp.maximum(m_old, jnp.max(st, axis=0, keepdims=True))
            alpha = jnp.exp2(m_old - m_new)
            p = jnp.exp2(st - m_new).astype(BF16)
            acc_ref[h] = alpha * acc_ref[h] + _dot(vt_ref[0, h], p)
            m_ref[h] = m_new

    @pl.when(ki < qi)
    def _():
        update(False)

    @pl.when(ki == qi)
    def _():
        update(True)
        ot = jnp.concatenate([acc_ref[h, :MLA_V] / acc_ref[h, MLA_V:MLA_V + 1] for h in range(H)], axis=0)
        o_ref[0] = _rms(ot.T, ng_ref[...]).astype(o_ref.dtype)


def _attention(q, k, vt, norm_g, tq):
    B, H, S, _ = q.shape
    W = H * MLA_V
    nq = S // tq
    pairs = [(i, j) for i in range(nq) for j in range(i + 1)]
    qtab = jnp.asarray([p[0] for p in pairs], jnp.int32)
    ktab = jnp.asarray([p[1] for p in pairs], jnp.int32)
    grid_spec = pltpu.PrefetchScalarGridSpec(
        num_scalar_prefetch=2, grid=(B, len(pairs)),
        in_specs=[pl.BlockSpec((1, H, tq, LANE), lambda b, j, qt, kt: (b, 0, qt[j], 0)),
                  pl.BlockSpec((1, H, tq, LANE), lambda b, j, qt, kt: (b, 0, kt[j], 0)),
                  pl.BlockSpec((1, H, _VT_ROWS, tq), lambda b, j, qt, kt: (b, 0, 0, kt[j])),
                  pl.BlockSpec((1, W), lambda b, j, qt, kt: (0, 0))],
        out_specs=pl.BlockSpec((1, tq, W), lambda b, j, qt, kt: (b, qt[j], 0)),
        scratch_shapes=[pltpu.VMEM((H, 1, tq), F32), pltpu.VMEM((H, _VT_ROWS, tq), F32)])
    return pl.pallas_call(
        _attn_kernel,
        grid_spec=grid_spec,
        out_shape=jax.ShapeDtypeStruct((B, S, W), BF16),
        compiler_params=_cp(("parallel", "arbitrary")),
        name="mla_attn",
    )(qtab, ktab, q, k, vt, norm_g.reshape(1, W))


def _store_token_tiles(ref, val):
    n, d = val.shape
    for s in range(d // LANE):
        ref[pl.ds(s, n, stride=SUBLANE), :] = val[:, s * LANE:(s + 1) * LANE]


def _load_token_tiles(ref, n):
    return jnp.concatenate([ref[pl.ds(s, n, stride=SUBLANE), :] for s in range(SUBLANE)], axis=1)


def _outproj_kernel(x_ref, ya_ref, yb_ref, yc_ref, yd_ref, wo_ref, gt_ref, sc_ref, sh_ref, g_ref,
                    wr_ref, br_ref, xo_ref, h_ref, lg_ref):
    B, ts, D = x_ref.shape
    n = B * ts

    acc = _dot(ya_ref[...].reshape(n, -1), wo_ref[0])
    acc = acc + _dot(_load_batch_major(yb_ref, B).astype(BF16), wo_ref[1])
    acc = acc + _dot(yc_ref[...].reshape(n, -1), wo_ref[2])
    acc = acc + _dot(_load_batch_major(yd_ref, B).astype(BF16), wo_ref[3])
    x = x_ref[...] + gt_ref[...] * acc.reshape(B, ts, D)
    xo_ref[...] = x
    h = (_rms(x, g_ref[...]) * (1.0 + sc_ref[...]) + sh_ref[...]).reshape(n, D)
    for b in range(B):
        _store_token_tiles(h_ref.at[b], h[b * ts:(b + 1) * ts])
    h_hi, h_lo = _split_bf16(h, 2)
    lg = _dot(h_hi, wr_ref[0]) + (_dot(h_lo, wr_ref[0]) + _dot(h_hi, wr_ref[1])) + br_ref[...]
    lg_ref[...] = lg.reshape(B, ts, LANE)


def _outproj(x, ya, yb, yc, yd, w_out, gt, sc, sh, g, w_router, b_router, tm):
    B, S, D = x.shape
    W = D // N_MIX
    ts = tm // B
    bm = lambda n: pl.BlockSpec((B, ts, n), lambda s: (0, s, 0))
    tmaj = pl.BlockSpec((W // LANE, ts * B, LANE), lambda s: (0, s, 0))
    const = lambda shape: pl.BlockSpec(shape, lambda s: (0,) * len(shape))
    vec = const((B, 1, D))
    return pl.pallas_call(
        _outproj_kernel,
        grid=(S // ts,),
        in_specs=[bm(D), bm(W), tmaj, bm(W), tmaj, const((N_MIX, W, D)), vec, vec, vec,
                  const((1, D)), const((2, D, LANE)), const((1, LANE))],
        out_specs=[bm(D), pl.BlockSpec((B, ts * SUBLANE, LANE), lambda s: (0, s, 0)), bm(LANE)],
        out_shape=[jax.ShapeDtypeStruct((B, S, D), F32), jax.ShapeDtypeStruct((B, S * SUBLANE, LANE), F32),
                   jax.ShapeDtypeStruct((B, S, LANE), F32)],
        compiler_params=_cp(("parallel",)),
        name="outproj",
    )(x, ya, yb, yc, yd, w_out.reshape(N_MIX, W, D).astype(BF16), gt, sc, sh, g,
      jnp.stack(_split_bf16(w_router, 2)), b_router)


_MOE_PAD_ROWS = MOE_BLOCK // TOP_K


def _moe_kernel(be_ref, nused_ref, tok_ref, tokn_ref, dst_ref, w_ref, h_hbm, w1_ref, w3_ref, w2_ref, o_hbm,
                xbuf, ybuf, gsem, ssem):
    i = pl.program_id(0)
    R = MOE_BLOCK
    TR = SUBLANE
    nused = nused_ref[0]
    slot = lax.rem(i, 2)
    other = 1 - slot

    def gather_rows(idx_ref, s):
        def body(r, c):
            src = pl.multiple_of(idx_ref[0, 0, r], TR)
            pltpu.make_async_copy(h_hbm.at[pl.ds(src, TR)], xbuf.at[s, pl.ds(pl.multiple_of(r * TR, TR), TR)],
                                  gsem.at[s]).start()
            return c
        lax.fori_loop(0, R, body, 0, unroll=8)

    def scatter_rows(s):
        def body(r, c):
            dst = pl.multiple_of(dst_ref[0, 0, r], TR)
            pltpu.make_async_copy(ybuf.at[s, pl.ds(pl.multiple_of(r * TR, TR), TR)], o_hbm.at[pl.ds(dst, TR)],
                                  ssem.at[s]).start(priority=1)
            return c
        lax.fori_loop(0, R, body, 0, unroll=8)

    def wait_gather(s):
        pltpu.make_async_copy(h_hbm.at[pl.ds(0, R * TR)], xbuf.at[s], gsem.at[s]).wait()

    def wait_scatter(s):
        pltpu.make_async_copy(ybuf.at[s], o_hbm.at[pl.ds(0, R * TR)], ssem.at[s]).wait()

    @pl.when(i == 0)
    def _():
        plane = o_hbm.shape[0] // TOP_K
        npad = _MOE_PAD_ROWS * TR
        ybuf[1] = jnp.zeros(ybuf.shape[1:], ybuf.dtype)
        for j in range(TOP_K):
            pad = pltpu.make_async_copy(ybuf.at[1, pl.ds(0, npad)],
                                        o_hbm.at[pl.ds((j + 1) * plane - npad, npad)], ssem.at[1])
            pad.start()
            pad.wait()
        gather_rows(tok_ref, 0)

    @pl.when(i < nused)
    def _():
        @pl.when(i + 1 < nused)
        def _():
            gather_rows(tokn_ref, other)

        wait_gather(slot)

        @pl.when(i >= 2)
        def _():
            wait_scatter(slot)

        xb = _load_token_tiles(xbuf.at[slot], R).astype(BF16)
        hid = _dot(xb, w1_ref[0])
        hid = hid * jax.nn.sigmoid(hid) * _dot(xb, w3_ref[0])
        y = _dot(hid.astype(BF16), w2_ref[0])
        _store_token_tiles(ybuf.at[slot], y * w_ref[:, 0:1])
        scatter_rows(slot)

        @pl.when(i == nused - 1)
        def _():
            wait_scatter(slot)

            @pl.when(i >= 1)
            def _():
                wait_scatter(other)


def _moe_experts(h2, blk_expert, n_used, row_tok, row_dst, row_w, w1, w3, w2):
    E, D, F = w1.shape
    T = h2.shape[0] // SUBLANE
    nb = blk_expert.shape[0]
    R = MOE_BLOCK
    blk = lambda i, be, nu: (i, 0, 0)
    nxt = lambda i, be, nu: (jnp.minimum(i + 1, nb - 1), 0, 0)
    wsel = lambda i, be, nu: (be[i], 0, 0)
    plane = (T + _MOE_PAD_ROWS) * SUBLANE
    row_tok = (row_tok * SUBLANE).reshape(nb, 1, R)
    row_dst = ((row_dst % TOP_K) * plane + (row_dst // TOP_K) * SUBLANE).reshape(nb, 1, R)
    grid_spec = pltpu.PrefetchScalarGridSpec(
        num_scalar_prefetch=2, grid=(nb,),
        in_specs=[pl.BlockSpec((1, 1, R), blk, memory_space=pltpu.SMEM),
                  pl.BlockSpec((1, 1, R), nxt, memory_space=pltpu.SMEM),
                  pl.BlockSpec((1, 1, R), blk, memory_space=pltpu.SMEM),
                  pl.BlockSpec((R, LANE), lambda i, be, nu: (i, 0)),
                  pl.BlockSpec(memory_space=pl.ANY),
                  pl.BlockSpec((1, D, F), wsel), pl.BlockSpec((1, D, F), wsel), pl.BlockSpec((1, F, D), wsel)],
        out_specs=pl.BlockSpec(memory_space=pl.ANY),
        scratch_shapes=[pltpu.VMEM((2, R * SUBLANE, LANE), F32), pltpu.VMEM((2, R * SUBLANE, LANE), F32),
                        pltpu.SemaphoreType.DMA((2,)), pltpu.SemaphoreType.DMA((2,))])
    return pl.pallas_call(
        _moe_kernel,
        grid_spec=grid_spec,
        out_shape=jax.ShapeDtypeStruct((TOP_K * plane, LANE), F32),
        compiler_params=_cp(("arbitrary",)),
        name="moe_experts",
    )(blk_expert, n_used, row_tok, row_tok, row_dst, jnp.broadcast_to(row_w.reshape(nb * R, 1), (nb * R, LANE)),
      h2, w1, w3, w2)


def _route(logits, T):
    p_group = jax.nn.softmax(logits[:, :N_GROUPS], axis=-1)
    g_sel = jnp.argmax(p_group, axis=-1).astype(jnp.int32)
    g_prob = jnp.max(p_group, axis=-1)
    logit_e = logits[:, N_GROUPS:N_GROUPS + N_EXPERTS].reshape(T, N_GROUPS, EXPERTS_PER_GROUP)
    logit_e = jnp.take_along_axis(logit_e, g_sel[:, None, None], axis=1)[:, 0]
    p_e = jax.nn.softmax(logit_e, axis=-1)
    top_p, top_i = lax.top_k(p_e, TOP_K)
    gate = (g_prob[:, None] * top_p / jnp.sum(top_p, axis=-1, keepdims=True)).reshape(-1)
    expert = (g_sel[:, None] * EXPERTS_PER_GROUP + top_i.astype(jnp.int32)).reshape(-1)
    n_assign = T * TOP_K
    n_blocks = -(-n_assign // MOE_BLOCK) + N_EXPERTS
    order = jnp.argsort(expert).astype(jnp.int32)
    counts = jnp.sum((expert[:, None] == jnp.arange(N_EXPERTS, dtype=jnp.int32)[None, :]).astype(jnp.int32), axis=0)
    start = jnp.cumsum(counts) - counts
    pcounts = (counts + MOE_BLOCK - 1) // MOE_BLOCK * MOE_BLOCK
    pend = jnp.cumsum(pcounts)
    pstart = pend - pcounts
    blk_start = jnp.arange(n_blocks, dtype=jnp.int32) * MOE_BLOCK
    blk_expert = jnp.minimum(jnp.sum((pend[None, :] <= blk_start[:, None]).astype(jnp.int32), axis=1), N_EXPERTS - 1)
    n_used = (pend[-1] // MOE_BLOCK).astype(jnp.int32).reshape(1)
    lane = jnp.arange(MOE_BLOCK, dtype=jnp.int32)[None, :]
    off = blk_start[:, None] + lane - pstart[blk_expert][:, None]
    valid = off < counts[blk_expert][:, None]
    pos = jnp.clip(start[blk_expert][:, None] + off, 0, n_assign - 1)
    asg = order[pos]
    row_tok = jnp.where(valid, asg // TOP_K, 0)
    row_dst = jnp.where(valid, asg, n_assign + lane)
    row_w = jnp.where(valid, gate[asg], 0.0)
    return blk_expert.astype(jnp.int32), n_used, row_tok, row_dst, row_w


def _combine_kernel(x_ref, y0_ref, y1_ref, gt_ref, o_ref):
    tm = x_ref.shape[1]
    y = _load_token_tiles(y0_ref.at[0], tm) + _load_token_tiles(y1_ref.at[0], tm)
    o_ref[0] = x_ref[0] + gt_ref[0] * y


def _combine(x, y2, gt, tm):
    B, S, D = x.shape
    nt = S // tm
    return pl.pallas_call(
        _combine_kernel,
        grid=(B, nt),
        in_specs=[pl.BlockSpec((1, tm, D), lambda b, s: (b, s, 0)),
                  pl.BlockSpec((1, tm * SUBLANE, LANE), lambda b, s: (0, b * nt + s, 0)),
                  pl.BlockSpec((1, tm * SUBLANE, LANE), lambda b, s: (1, b * nt + s, 0)),
                  pl.BlockSpec((1, 1, D), lambda b, s: (b, 0, 0))],
        out_specs=pl.BlockSpec((1, tm, D), lambda b, s: (b, s, 0)),
        out_shape=jax.ShapeDtypeStruct((B, S, D), F32),
        compiler_params=_cp(("parallel", "parallel")),
        name="moe_combine",
    )(x, y2, y2, gt)


def _tiles(S):
    big = S >= 2048
    return dict(tm=1024 if big else 256, tl=1024 if big else 256, ts_s5=64, ts_lru=128 if big else 64,
                tq=1024 if big else 256)


def kernel(x, c, pos_offset, ada_w, ada_b, norm1_g, w_in, rwkv_mu, rwkv_w0, rwkv_w2, rwkv_a0, rwkv_a2, rwkv_g2, rwkv_k_k, rwkv_k_a, rwkv_r_k, rwkv_ln_w, rwkv_ln_b, s5_lambda_re, s5_lambda_im, s5_b_re, s5_b_im, s5_c_re, s5_c_im, s5_d, s5_log_dt, s5_glu_w, s5_glu_b, mla_q_norm_g, mla_w_uq, mla_kv_norm_g, mla_w_ukv, mla_q_head_g, mla_k_head_g, lru_conv_w, lru_conv_b, lru_w_a, lru_b_a, lru_w_x, lru_b_x, lru_lambda, branch_norm_g, w_out, norm2_g, moe_w_group, moe_b_group, moe_w_expert, moe_b_expert, moe_w1, moe_w3, moe_w2):
    B, S, D = x.shape
    depth = ada_w.shape[0]
    T = B * S
    W = D // N_MIX
    assert B == SUBLANE, "time-major scans put the batch on the 8 sublanes"
    assert D == SUBLANE * LANE, "token-tile layout: one (8, 128) tile per token row"
    tiles = _tiles(S)
    mod = _adaln_mod(c, ada_w, ada_b)

    for l in range(depth):
        sh1, sc1, gt1, sh2, sc2, gt2 = [mod[l, :, None, i * D:(i + 1) * D] for i in range(6)]
        z_rwkv, z_s5, q_a, kv_a, kpe, z_lru, z_gate = _inproj(
            x, sc1, sh1, norm1_g[l].reshape(1, D), _pad_w_in(w_in[l]), tiles['tm'])
        y_a = _rwkv(z_rwkv, rwkv_mu[l], rwkv_w0[l], rwkv_w2[l], rwkv_a0[l], rwkv_a2[l], rwkv_g2[l],
                    rwkv_k_k[l], rwkv_k_a[l], rwkv_r_k[l].reshape(-1), rwkv_ln_w[l], rwkv_ln_b[l], tiles['tl'])
        y_b = _s5(z_s5, s5_lambda_re[l], s5_lambda_im[l], s5_b_re[l], s5_b_im[l], s5_c_re[l],
                  s5_c_im[l], s5_d[l], s5_log_dt[l], s5_glu_w[l], s5_glu_b[l], branch_norm_g[l, 0],
                  tiles['ts_s5'])
        q, k, vt = _mla_prep(q_a, kv_a, kpe, pos_offset, mla_q_norm_g[l], mla_w_uq[l], mla_kv_norm_g[l],
                             mla_w_ukv[l], mla_q_head_g[l], mla_k_head_g[l], tiles['tm'])
        y_c = _attention(q, k, vt, branch_norm_g[l, 1], tiles['tq'])
        y_d = _lru(z_lru, z_gate, lru_conv_w[l], lru_conv_b[l], lru_w_a[l], lru_b_a[l],
                   lru_w_x[l], lru_b_x[l], lru_lambda[l], branch_norm_g[l, 2], tiles['ts_lru'])
        w_router = jnp.zeros((D, LANE), F32).at[:, :N_GROUPS].set(moe_w_group[l])
        w_router = w_router.at[:, N_GROUPS:N_GROUPS + N_EXPERTS].set(moe_w_expert[l])
        b_router = jnp.zeros((1, LANE), F32).at[0, :N_GROUPS].set(moe_b_group[l])
        b_router = b_router.at[0, N_GROUPS:N_GROUPS + N_EXPERTS].set(moe_b_expert[l])
        x, h2, logits = _outproj(x, y_a, y_b, y_c, y_d, w_out[l], gt1, sc2, sh2, norm2_g[l].reshape(1, D),
                                 w_router, b_router, tiles['tm'])
        blk_expert, n_used, row_tok, row_dst, row_w = _route(logits.reshape(T, LANE), T)
        y2 = _moe_experts(h2.reshape(T * SUBLANE, LANE), blk_expert, n_used, row_tok, row_dst, row_w,
                          moe_w1[l].astype(BF16), moe_w3[l].astype(BF16), moe_w2[l].astype(BF16))
        x = _combine(x, y2.reshape(TOP_K, -1, LANE), gt2, tiles['tm'])
    return x
```

```python
import functools
import math

import numpy as np
import jax
import jax.numpy as jnp
from jax import lax
from jax.experimental import pallas as pl
from jax.experimental.pallas import tpu as pltpu

F32 = jnp.float32
BF16 = jnp.bfloat16
HI = lax.Precision.HIGHEST

N_MIX = 4
RWKV_HEAD = 64
RWKV_W_RANK, RWKV_A_RANK, RWKV_G_RANK = 32, 32, 64
RWKV_LN_EPS = 64e-5
S5_GROUP, S5_STATE = 16, 64
MLA_HEADS, MLA_NOPE, MLA_ROPE, MLA_V = 4, 64, 32, 64
MLA_QK = MLA_NOPE + MLA_ROPE
ROPE_THETA = 10000.0
LRU_C = 8.0
CONV_WIDTH = 4
ATTN_CHUNK = 64
N_GROUPS, EXPERTS_PER_GROUP, TOP_K = 4, 8, 2
N_EXPERTS = N_GROUPS * EXPERTS_PER_GROUP
MOE_BLOCK = 256
NORM_EPS = 1e-6

LANE = 128
SUBLANE = 8
RWKV_CHUNK = 64
VMEM_LIMIT = 48 * 1024 * 1024


def _cp(sem, vmem=VMEM_LIMIT):
    return pltpu.CompilerParams(dimension_semantics=sem, vmem_limit_bytes=vmem)


def _dot(a, b):
    return jnp.dot(a, b, preferred_element_type=F32)


def _dot_hi(a, b):
    return jnp.dot(a, b, precision=HI, preferred_element_type=F32)


def _dot_nt(a, b):
    return lax.dot_general(a, b, (((1,), (1,)), ((), ())), preferred_element_type=F32)


def _dot_nt_hi(a, b):
    return lax.dot_general(a, b, (((1,), (1,)), ((), ())), precision=HI, preferred_element_type=F32)


def _softplus(x):
    return jnp.maximum(x, 0.0) + jnp.log1p(jnp.exp(-jnp.abs(x)))


def _rms(x, g, eps=NORM_EPS):
    return x * lax.rsqrt(jnp.mean(x * x, axis=-1, keepdims=True) + eps) * g


def _mod_kernel(c_ref, w_ref, b_ref, o_ref):
    c = c_ref[...]
    cond = c * jax.nn.sigmoid(c)
    o_ref[0] = _dot_hi(cond, w_ref[0]) + b_ref[0]


def _adaln_mod(c, ada_w, ada_b):
    L, D, D6 = ada_w.shape
    B = c.shape[0]
    nj = D6 // D
    return pl.pallas_call(
        _mod_kernel,
        grid=(L, nj),
        in_specs=[pl.BlockSpec((B, D), lambda l, j: (0, 0)),
                  pl.BlockSpec((1, D, D), lambda l, j: (l, 0, j)),
                  pl.BlockSpec((1, 1, D), lambda l, j: (l, 0, j))],
        out_specs=pl.BlockSpec((1, B, D), lambda l, j: (l, 0, j)),
        out_shape=jax.ShapeDtypeStruct((L, B, D6), F32),
        compiler_params=_cp(("parallel", "parallel")),
        name="adaln_mod",
    )(c, ada_w, ada_b.reshape(L, 1, D6))


_RWKV_IN = 3 * 256 + RWKV_W_RANK + RWKV_A_RANK + RWKV_G_RANK
_IN_COLS = (_RWKV_IN, 256, 256, 128, LANE, 256, 256)
_IN_OFFS = tuple(int(v) for v in np.cumsum((0,) + _IN_COLS))
_KPE_LANE0 = MLA_NOPE


def _pad_w_in(w_in):
    D = w_in.shape[0]
    o = np.cumsum((0, _RWKV_IN, 256, 256, 128, MLA_ROPE, 256, 256))
    pieces = [w_in[:, o[i]:o[i + 1]] for i in range(7)]
    kpe = jnp.zeros((D, LANE), w_in.dtype).at[:, _KPE_LANE0:_KPE_LANE0 + MLA_ROPE].set(pieces[4])
    pieces[4] = kpe
    return jnp.concatenate(pieces, axis=1).astype(BF16)


_IN_TIME_MAJOR = (1, 5, 6)


def _store_time_major(ref, val, nb):
    ts = val.shape[0] // nb
    for j in range(val.shape[1] // LANE):
        for b in range(nb):
            ref[j, pl.ds(b, ts, stride=nb), :] = val[b * ts:(b + 1) * ts, j * LANE:(j + 1) * LANE]


def _load_batch_major(ref, nb):
    ts = ref.shape[1] // nb
    return jnp.concatenate(
        [jnp.concatenate([ref[j, pl.ds(b, ts, stride=nb), :] for b in range(nb)], axis=0)
         for j in range(ref.shape[0])], axis=1)


def _lanes(ref):
    return jnp.concatenate([ref[j] for j in range(ref.shape[0])], axis=1)


def _store_lanes(ref, val):
    for j in range(ref.shape[0]):
        ref[j] = val[:, j * LANE:(j + 1) * LANE].astype(ref.dtype)


def _inproj_kernel(x_ref, sc_ref, sh_ref, g_ref, w_ref, *outs):
    B, ts, D = x_ref.shape
    h = _rms(x_ref[...], g_ref[...]) * (1.0 + sc_ref[...]) + sh_ref[...]
    z = _dot(h.reshape(B * ts, D).astype(BF16), w_ref[...])
    for i, o_ref in enumerate(outs):
        piece = z[:, _IN_OFFS[i]:_IN_OFFS[i + 1]]
        if i in _IN_TIME_MAJOR:
            _store_time_major(o_ref, piece, B)
        else:
            o_ref[...] = piece.reshape(B, ts, piece.shape[1])


def _inproj(x, sc, sh, g, w_pad, tm):
    B, S, D = x.shape
    ts = tm // B
    bm = lambda n: pl.BlockSpec((B, ts, n), lambda s: (0, s, 0))
    const = lambda shape: pl.BlockSpec(shape, lambda s: (0,) * len(shape))
    return pl.pallas_call(
        _inproj_kernel,
        grid=(S // ts,),
        in_specs=[bm(D), const((B, 1, D)), const((B, 1, D)), const((1, D)), const(w_pad.shape)],
        out_specs=[pl.BlockSpec((n // LANE, ts * B, LANE), lambda s: (0, s, 0)) if i in _IN_TIME_MAJOR else bm(n)
                   for i, n in enumerate(_IN_COLS)],
        out_shape=[jax.ShapeDtypeStruct((n // LANE, S * B, LANE) if i in _IN_TIME_MAJOR else (B, S, n), F32)
                   for i, n in enumerate(_IN_COLS)],
        compiler_params=_cp(("parallel",)),
        name="inproj",
    )(x, sc, sh, g, w_pad)


def _split_bf16(x, parts):
    out = []
    for _ in range(parts - 1):
        hi = x.astype(BF16)
        out.append(hi)
        x = x - hi.astype(F32)
    out.append(x.astype(BF16))
    return out


def _rwkv_kernel(z_ref, mu_ref, w0_ref, a0_ref, wl_ref, kk_ref, ka_ref, rk_ref, lnw_ref, lnb_ref,
                 hsum_ref, tri_ref, o_ref, zprev_ref, g_ref, y_ref):
    W = 256
    NH = W // RWKV_HEAD
    L = RWKV_CHUNK
    TL = z_ref.shape[1]

    @pl.when(pl.program_id(1) == 0)
    def _():
        zprev_ref[...] = jnp.zeros_like(zprev_ref)
        g_ref[...] = jnp.zeros_like(g_ref)

    hsum = hsum_ref[...]

    def head_sum(t):
        hi, lo_ = _split_bf16(t, 2)
        return _dot(hi, hsum) + _dot(lo_, hsum)

    z = z_ref[0]
    rows = lax.broadcasted_iota(jnp.int32, z.shape, 0)
    zp = jnp.where(rows == 0, zprev_ref[...], pltpu.roll(z, 1, axis=0))
    zprev_ref[...] = z[TL - 1:TL, :]
    zs = z + (zp - z) * mu_ref[...]
    r, k, v, lo = zs[:, 0:W], zs[:, W:2 * W], zs[:, 2 * W:3 * W], zs[:, 3 * W:3 * W + LANE]
    lane = lax.broadcasted_iota(jnp.int32, lo.shape, 1)
    act = jnp.where(lane < RWKV_W_RANK, jnp.tanh(lo),
                    jnp.where(lane < RWKV_W_RANK + RWKV_A_RANK, lo, jax.nn.sigmoid(lo)))
    up = _dot(act.astype(BF16), wl_ref[...])
    w = -_softplus(-(w0_ref[...] + up[:, 0:W])) - 0.5
    ld = -jnp.exp(w)
    a = jax.nn.sigmoid(a0_ref[...] + up[:, W:2 * W])
    g = up[:, 2 * W:3 * W]
    kk = k * kk_ref[...]
    kk = kk / jnp.maximum(jnp.sqrt(head_sum(kk * kk)), 1e-12)
    k2 = k * (1.0 + (a - 1.0) * ka_ref[...])
    kka = kk * a

    tri_b = tri_ref[0]
    tri_incl = tri_b > 0
    tri_strict = tri_ref[1] > 0
    eye = (lax.broadcasted_iota(jnp.int32, (RWKV_HEAD, RWKV_HEAD), 0)
           == lax.broadcasted_iota(jnp.int32, (RWKV_HEAD, RWKV_HEAD), 1)).astype(BF16)

    NC = TL // L
    pairs = [(j, h) for j in range(NC) for h in range(NH)]
    hsl = [slice(h * RWKV_HEAD, (h + 1) * RWKV_HEAD) for h in range(NH)]
    csl = [slice(j * L, (j + 1) * L) for j in range(NC)]
    bt, rt, a_s, k_s, a_e, k_e, v_c, p_l = [], [], [], [], [], [], [], []
    for j in range(NC):
        ld_c = ld[csl[j]]
        cs = sum(_dot(tri_b, p) for p in _split_bf16(ld_c, 3))
        cs_l = cs[L - 1:L, :]
        p_inv = jnp.exp(-cs)
        p_end = jnp.exp(cs_l - cs)
        p_l.append(jnp.exp(cs_l))
        bt.append(-kk[csl[j]] * jnp.exp(cs - ld_c))
        rt.append(r[csl[j]] * jnp.exp(cs))
        a_s.append((kka[csl[j]] * p_inv).astype(BF16))
        k_s.append((k2[csl[j]] * p_inv).astype(BF16))
        a_e.append((kka[csl[j]] * p_end).astype(BF16))
        k_e.append((k2[csl[j]] * p_end).astype(BF16))
        v_c.append(v[csl[j]].astype(BF16))

    x = [jnp.concatenate([bt[j][:, hsl[h]], rt[j][:, hsl[h]]], axis=0).astype(BF16) for j, h in pairs]
    ga = [_dot_nt(x[i], a_s[j][:, hsl[h]]) for i, (j, h) in enumerate(pairs)]
    gk = [_dot_nt(x[i], k_s[j][:, hsl[h]]) for i, (j, h) in enumerate(pairs)]
    m_ba = [jnp.where(tri_strict, t[:L], 0.0).astype(BF16) for t in ga]
    m_ra = [jnp.where(tri_incl, t[L:], 0.0).astype(BF16) for t in ga]
    m_bk = [jnp.where(tri_strict, t[:L], 0.0).astype(BF16) for t in gk]
    m_rk = [jnp.where(tri_incl, t[L:], 0.0).astype(BF16) for t in gk]
    v_h = [v_c[j][:, hsl[h]] for j, h in pairs]
    wn = [jnp.concatenate([bt[j][:, hsl[h]], _dot(m_bk[i], v_h[i])], axis=1) for i, (j, h) in enumerate(pairs)]
    n_pow = m_ba
    wn = [w_ + _dot(n_, w_.astype(BF16)) for w_, n_ in zip(wn, n_pow)]
    for _ in range(5):
        n_pow = [_dot(n_, n_).astype(BF16) for n_ in n_pow]
        wn = [w_ + _dot(n_, w_.astype(BF16)) for w_, n_ in zip(wn, n_pow)]
    wn_b = [w_.astype(BF16) for w_ in wn]
    mw = [_dot(m_, w_) for m_, w_ in zip(m_ra, wn_b)]
    r2 = [(rt[j][:, hsl[h]] + mw[i][:, :RWKV_HEAD]).astype(BF16) for i, (j, h) in enumerate(pairs)]
    y0 = [mw[i][:, RWKV_HEAD:] + _dot(m_rk[i], v_h[i]) for i in range(len(pairs))]
    b2_t = [_dot_nt(eye, w_[:, :RWKV_HEAD]).astype(BF16) for w_ in wn_b]
    c_m = [_dot(b2_t[i], a_e[j][:, hsl[h]]).astype(BF16) for i, (j, h) in enumerate(pairs)]
    uv_t = [_dot_nt(eye, jnp.concatenate([wn_b[i][:, RWKV_HEAD:], v_h[i]], axis=0)).astype(BF16)
            for i in range(len(pairs))]
    d_t = [_dot(uv_t[i], jnp.concatenate([a_e[j][:, hsl[h]], k_e[j][:, hsl[h]]], axis=0))
           for i, (j, h) in enumerate(pairs)]
    for i, (j, h) in enumerate(pairs):
        g0 = g_ref[h]
        g_hi, g_lo = _split_bf16(g0, 2)
        y_ref[csl[j], hsl[h]] = _dot_nt(r2[i], g_hi) + _dot_nt(r2[i], g_lo) + y0[i]
        g_ref[h] = g0 * p_l[j][:, hsl[h]] + _dot(g_hi, c_m[i]) + _dot(g_lo, c_m[i]) + d_t[i]

    y = y_ref[...]
    inv_n = 1.0 / RWKV_HEAD
    mean = head_sum(y) * inv_n
    yc = y - mean
    var = head_sum(yc * yc) * inv_n
    yn = yc * lax.rsqrt(var + RWKV_LN_EPS) * lnw_ref[...] + lnb_ref[...]
    bonus = head_sum(r * k2 * rk_ref[...]) * v
    o_ref[0] = ((yn + bonus) * g).astype(o_ref.dtype)


def _rwkv(z, mu, w0, w2, a0, a2, g2, k_k, k_a, r_k, ln_w, ln_b, tl):
    B, S, _ = z.shape
    W = 256
    L = RWKV_CHUNK
    wl = jnp.zeros((LANE, 3 * W), F32)
    wl = wl.at[0:RWKV_W_RANK, 0:W].set(w2)
    wl = wl.at[RWKV_W_RANK:RWKV_W_RANK + RWKV_A_RANK, W:2 * W].set(a2)
    wl = wl.at[RWKV_W_RANK + RWKV_A_RANK:LANE, 2 * W:3 * W].set(g2).astype(BF16)
    hid = np.arange(W) // RWKV_HEAD
    hsum = jnp.asarray((hid[:, None] == hid[None, :]).astype(np.float32), dtype=BF16)
    t = np.arange(L)
    tri = jnp.asarray(np.stack([(t[None, :] <= t[:, None]), (t[None, :] < t[:, None])]).astype(np.float32), dtype=BF16)
    row2 = lambda v_: v_.reshape(1, -1)
    const = lambda shape: pl.BlockSpec(shape, lambda b, s: (0,) * len(shape))
    return pl.pallas_call(
        _rwkv_kernel,
        grid=(B, S // tl),
        in_specs=[pl.BlockSpec((1, tl, _RWKV_IN), lambda b, s: (b, s, 0)),
                  const((1, _RWKV_IN)), const((1, W)), const((1, W)), const((LANE, 3 * W)),
                  const((1, W)), const((1, W)), const((1, W)), const((1, W)), const((1, W)),
                  const((W, W)), const((2, L, L))],
        out_specs=pl.BlockSpec((1, tl, W), lambda b, s: (b, s, 0)),
        out_shape=jax.ShapeDtypeStruct((B, S, W), BF16),
        scratch_shapes=[pltpu.VMEM((1, _RWKV_IN), F32),
                        pltpu.VMEM((W // RWKV_HEAD, RWKV_HEAD, RWKV_HEAD), F32),
                        pltpu.VMEM((tl, W), F32)],
        compiler_params=_cp(("parallel", "arbitrary")),
        name="rwkv7",
    )(z, row2(mu), row2(w0), row2(a0), wl, row2(k_k), row2(k_a), row2(r_k), row2(ln_w), row2(ln_b), hsum, tri)


def _s5_kernel(u_ref, wb_ref, are_ref, aim_ref, wc_ref, d_ref, gw_ref, gb_ref, ng_ref, o_ref,
               hre_ref, him_ref, sre_ref, sim_ref):
    nb = SUBLANE
    rows = u_ref.shape[1]
    P = are_ref.shape[1]

    @pl.when(pl.program_id(0) == 0)
    def _():
        sre_ref[...] = jnp.zeros_like(sre_ref)
        sim_ref[...] = jnp.zeros_like(sim_ref)

    u = _lanes(u_ref)
    bu = _dot(u.astype(BF16), wb_ref[...])
    hre_ref[...] = bu[:, :P]
    him_ref[...] = bu[:, P:]
    a_re = jnp.broadcast_to(are_ref[...], (nb, P))
    a_im = jnp.broadcast_to(aim_ref[...], (nb, P))

    def step(t, carry):
        h_re, h_im = carry
        i = pl.multiple_of(t * nb, nb)
        n_re = a_re * h_re - a_im * h_im + hre_ref[pl.ds(i, nb), :]
        n_im = a_re * h_im + a_im * h_re + him_ref[pl.ds(i, nb), :]
        hre_ref[pl.ds(i, nb), :] = n_re
        him_ref[pl.ds(i, nb), :] = n_im
        return n_re, n_im

    h_re, h_im = lax.fori_loop(0, rows // nb, step, (sre_ref[...], sim_ref[...]))
    sre_ref[...] = h_re
    sim_ref[...] = h_im
    y = _dot(hre_ref[...].astype(BF16), wc_ref[0]) + _dot(him_ref[...].astype(BF16), wc_ref[1])
    y = jax.nn.gelu(y + d_ref[...] * u)
    y = y * jax.nn.sigmoid(_dot(y.astype(BF16), gw_ref[...]) + gb_ref[...])
    _store_lanes(o_ref, _rms(y, ng_ref[...]))


def _s5(u_tm, lam_re, lam_im, b_re, b_im, c_re, c_im, d, log_dt, glu_w, glu_b, norm_g, ts):
    nl, rows_total, _ = u_tm.shape
    W = nl * LANE
    tmaj = lambda r: pl.BlockSpec((nl, r, LANE), lambda s: (0, s, 0))
    G, P = lam_re.shape
    dt = jnp.exp(log_dt)[:, None]
    mag = jnp.exp(lam_re * dt)
    a_re = mag * jnp.cos(lam_im * dt)
    a_im = mag * jnp.sin(lam_im * dt)
    den = lam_re * lam_re + lam_im * lam_im
    q_re = ((a_re - 1.0) * lam_re + a_im * lam_im) / den
    q_im = (a_im * lam_re - (a_re - 1.0) * lam_im) / den
    bb_re = q_re[..., None] * b_re - q_im[..., None] * b_im
    bb_im = q_re[..., None] * b_im + q_im[..., None] * b_re
    eye = jnp.eye(G, dtype=F32)
    wb_re = jnp.einsum('gpi,gh->gihp', bb_re, eye).reshape(W, G * P)
    wb_im = jnp.einsum('gpi,gh->gihp', bb_im, eye).reshape(W, G * P)
    wb = jnp.concatenate([wb_re, wb_im], axis=1).astype(BF16)
    wc_re = jnp.einsum('gip,gh->gphi', c_re, eye).reshape(G * P, W)
    wc_im = jnp.einsum('gip,gh->gphi', -c_im, eye).reshape(G * P, W)
    wc = jnp.stack([wc_re, wc_im]).astype(BF16)
    GP = G * P
    rows = ts * SUBLANE
    const = lambda shape: pl.BlockSpec(shape, lambda s: (0,) * len(shape))
    return pl.pallas_call(
        _s5_kernel,
        grid=(rows_total // rows,),
        in_specs=[tmaj(rows),
                  const((W, 2 * GP)), const((1, GP)), const((1, GP)), const((2, GP, W)),
                  const((1, W)), const((W, W)), const((1, W)), const((1, W))],
        out_specs=tmaj(rows),
        out_shape=jax.ShapeDtypeStruct((nl, rows_total, LANE), F32),
        scratch_shapes=[pltpu.VMEM((rows, GP), F32), pltpu.VMEM((rows, GP), F32),
                        pltpu.VMEM((SUBLANE, GP), F32), pltpu.VMEM((SUBLANE, GP), F32)],
        compiler_params=_cp(("arbitrary",)),
        name="s5",
    )(u_tm, wb, a_re.reshape(1, GP), a_im.reshape(1, GP), wc, d.reshape(1, W), glu_w.astype(BF16),
      glu_b.reshape(1, W), norm_g.reshape(1, W))


def _lru_kernel(x_ref, gate_ref, cw_ref, cb_ref, wax_ref, bax_ref, lam_ref, ng_ref, o_ref,
                xc_ref, a_ref, b_ref, hs_ref):
    nb = SUBLANE
    rows = x_ref.shape[1]
    W = x_ref.shape[0] * LANE
    halo = (CONV_WIDTH - 1) * nb

    @pl.when(pl.program_id(0) == 0)
    def _():
        xc_ref[...] = jnp.zeros_like(xc_ref)
        hs_ref[...] = jnp.zeros_like(hs_ref)

    x = _lanes(x_ref)
    xf = jnp.concatenate([xc_ref[...], x], axis=0)
    xc_ref[...] = x[rows - halo:, :]
    xc = cb_ref[...]
    for j in range(CONV_WIDTH):
        xc = xc + cw_ref[j:j + 1, :] * xf[j * nb:j * nb + rows, :]
    gates = _dot(xc.astype(BF16), wax_ref[...]) + bax_ref[...]
    r = jax.nn.sigmoid(gates[:, :W])
    i = jax.nn.sigmoid(gates[:, W:])
    log_a = -LRU_C * r * _softplus(-lam_ref[...])
    a = jnp.exp(log_a)
    a_ref[...] = a
    b_ref[...] = jnp.sqrt(1.0 - jnp.exp(2.0 * log_a)) * (i * xc)

    def step(t, h):
        i0 = pl.multiple_of(t * nb, nb)
        h = a_ref[pl.ds(i0, nb), :] * h + b_ref[pl.ds(i0, nb), :]
        b_ref[pl.ds(i0, nb), :] = h
        return h

    hs_ref[...] = lax.fori_loop(0, rows // nb, step, hs_ref[...])
    y = b_ref[...] * jax.nn.gelu(_lanes(gate_ref))
    _store_lanes(o_ref, _rms(y, ng_ref[...]))


def _lru(x_tm, gate_tm, conv_w, conv_b, w_a, b_a, w_x, b_x, lam, norm_g, ts):
    nl, rows_total, _ = x_tm.shape
    W = nl * LANE
    tmaj = lambda r: pl.BlockSpec((nl, r, LANE), lambda s: (0, s, 0))
    nblk, bs, _ = w_a.shape
    eye = jnp.eye(nblk, dtype=F32)
    bd = lambda w_: jnp.einsum('nkj,nm->nkmj', w_, eye).reshape(W, W)
    wax = jnp.concatenate([bd(w_a), bd(w_x)], axis=1).astype(BF16)
    bax = jnp.concatenate([b_a, b_x]).reshape(1, 2 * W)
    rows = ts * SUBLANE
    const = lambda shape: pl.BlockSpec(shape, lambda s: (0,) * len(shape))
    return pl.pallas_call(
        _lru_kernel,
        grid=(rows_total // rows,),
        in_specs=[tmaj(rows), tmaj(rows),
                  const((CONV_WIDTH, W)), const((1, W)), const((W, 2 * W)), const((1, 2 * W)),
                  const((1, W)), const((1, W))],
        out_specs=tmaj(rows),
        out_shape=jax.ShapeDtypeStruct((nl, rows_total, LANE), F32),
        scratch_shapes=[pltpu.VMEM(((CONV_WIDTH - 1) * SUBLANE, W), F32), pltpu.VMEM((rows, W), F32),
                        pltpu.VMEM((rows, W), F32), pltpu.VMEM((SUBLANE, W), F32)],
        compiler_params=_cp(("arbitrary",)),
        name="rglru",
    )(x_tm, gate_tm, conv_w, conv_b.reshape(1, W), wax, bax, lam.reshape(1, W), norm_g.reshape(1, W))


_VT_ROWS = MLA_V + 2 * SUBLANE


def _mla_prep_tile(pos0, q_a, kv_a, kpe, gq_ref, gkv_ref, wq_ref, wk_ref, wv_ref,
                   gqh_ref, gkh_ref, freq_ref, q_ref, k_ref, v_ref):
    tm = q_a.shape[0]
    pos = (pos0 + lax.broadcasted_iota(jnp.int32, (tm, LANE), 0)).astype(F32)
    ang = pos * freq_ref[...]
    cos_t = jnp.cos(ang)
    sin_t = jnp.sin(ang)
    lane = lax.broadcasted_iota(jnp.int32, (tm, LANE), 1)
    half = MLA_ROPE // 2
    first = (lane >= MLA_NOPE) & (lane < MLA_NOPE + half)
    second = (lane >= MLA_NOPE + half) & (lane < MLA_QK)

    def rope(t):
        rot = jnp.where(first, -pltpu.roll(t, LANE - half, axis=1),
                        jnp.where(second, pltpu.roll(t, half, axis=1), 0.0))
        return t * cos_t + rot * sin_t

    def head_norm(t, g):
        ms = jnp.sum(t * t, axis=-1, keepdims=True) * (1.0 / MLA_QK)
        return t * lax.rsqrt(ms + NORM_EPS) * g

    qn = _rms(q_a, gq_ref[...]).astype(BF16)
    kvn = _rms(kv_a, gkv_ref[...]).astype(BF16)
    q_all = _dot(qn, wq_ref[...])
    k_all = _dot(kvn, wk_ref[...])
    v_all = _dot(kvn, wv_ref[...])
    scale = MLA_QK ** -0.5 * math.log2(math.e)
    ones_row = lax.broadcasted_iota(jnp.int32, (_VT_ROWS, tm), 0) == MLA_V
    for h in range(MLA_HEADS):
        hs = slice(h * LANE, (h + 1) * LANE)
        q_ref[0, h] = (rope(head_norm(q_all[:, hs], gqh_ref[...])) * scale).astype(q_ref.dtype)
        k_ref[0, h] = rope(head_norm(k_all[:, hs] + kpe, gkh_ref[...])).astype(k_ref.dtype)
        v_ref[0, h] = jnp.where(ones_row, 1.0, v_all[:, hs].T[:_VT_ROWS]).astype(v_ref.dtype)


def _mla_prep_kernel(pos_ref, qa_ref, kva_ref, kpe_ref, *rest):
    pos0 = pos_ref[pl.program_id(0)] + pl.program_id(1) * qa_ref.shape[1]
    _mla_prep_tile(pos0, qa_ref[0], kva_ref[0], kpe_ref[0], *rest)


def _mla_prep(q_a, kv_a, kpe, pos_offset, q_norm_g, w_uq, kv_norm_g, w_ukv, q_head_g, k_head_g, tm):
    B, S, QR = q_a.shape
    KVR = kv_a.shape[2]
    H = MLA_HEADS
    wq = jnp.zeros((QR, H, LANE), F32).at[:, :, :MLA_QK].set(w_uq.reshape(QR, H, MLA_QK)).reshape(QR, H * LANE)
    wkv = w_ukv.reshape(KVR, H, MLA_NOPE + MLA_V)
    wk = jnp.zeros((KVR, H, LANE), F32).at[:, :, :MLA_NOPE].set(wkv[:, :, :MLA_NOPE]).reshape(KVR, H * LANE)
    wv = jnp.zeros((KVR, H, LANE), F32).at[:, :, :MLA_V].set(wkv[:, :, MLA_NOPE:]).reshape(KVR, H * LANE)
    pad_g = lambda g: jnp.zeros((1, LANE), F32).at[0, :MLA_QK].set(g)
    half = MLA_ROPE // 2
    inv_freq = np.power(np.float32(ROPE_THETA), -np.arange(half, dtype=np.float32) * np.float32(2.0) / np.float32(MLA_ROPE))
    freq = np.zeros((1, LANE), np.float32)
    freq[0, MLA_NOPE:MLA_NOPE + half] = inv_freq
    freq[0, MLA_NOPE + half:MLA_QK] = inv_freq
    const = lambda shape: pl.BlockSpec(shape, lambda b, s, p: (0,) * len(shape))
    row = lambda n: pl.BlockSpec((1, tm, n), lambda b, s, p: (b, s, 0))
    hout = pl.BlockSpec((1, H, tm, LANE), lambda b, s, p: (b, 0, s, 0))
    grid_spec = pltpu.PrefetchScalarGridSpec(
        num_scalar_prefetch=1, grid=(B, S // tm),
        in_specs=[row(QR), row(KVR), row(LANE), const((1, QR)), const((1, KVR)),
                  const((QR, H * LANE)), const((KVR, H * LANE)), const((KVR, H * LANE)),
                  const((1, LANE)), const((1, LANE)), const((1, LANE))],
        out_specs=[hout, hout, pl.BlockSpec((1, H, _VT_ROWS, tm), lambda b, s, p: (b, 0, 0, s))])
    return pl.pallas_call(
        _mla_prep_kernel,
        grid_spec=grid_spec,
        out_shape=[jax.ShapeDtypeStruct((B, H, S, LANE), BF16)] * 2 + [jax.ShapeDtypeStruct((B, H, _VT_ROWS, S), BF16)],
        compiler_params=_cp(("parallel", "parallel")),
        name="mla_prep",
    )(pos_offset, q_a, kv_a, kpe, q_norm_g.reshape(1, QR), kv_norm_g.reshape(1, KVR),
      wq.astype(BF16), wk.astype(BF16), wv.astype(BF16), pad_g(q_head_g), pad_g(k_head_g), jnp.asarray(freq))


def _attn_kernel(qtab_ref, ktab_ref, q_ref, k_ref, vt_ref, ng_ref, o_ref, m_ref, acc_ref):
    H = q_ref.shape[1]
    tq = q_ref.shape[2]
    j = pl.program_id(1)
    qi = qtab_ref[j]
    ki = ktab_ref[j]

    @pl.when(ki == 0)
    def _():
        m_ref[...] = jnp.full_like(m_ref, -1e30)
        acc_ref[...] = jnp.zeros_like(acc_ref)

    def update(masked):
        if masked:
            kc = lax.broadcasted_iota(jnp.int32, (tq, tq), 0) // ATTN_CHUNK
            qc = lax.broadcasted_iota(jnp.int32, (tq, tq), 1) // ATTN_CHUNK
            visible = kc <= qc
        nxt = _dot_nt(k_ref[0, 0], q_ref[0, 0])
        for h in range(H):
            st = nxt
            if h + 1 < H:
                nxt = _dot_nt(k_ref[0, h + 1], q_ref[0, h + 1])
            if masked:
                st = jnp.where(visible, st, -1e30)
            m_old = m_ref[h]
            m_new = jnp.maximum(m_old, jnp.max(st, axis=0, keepdims=True))
            alpha = jnp.exp2(m_old - m_new)
            p = jnp.exp2(st - m_new).astype(BF16)
            acc_ref[h] = alpha * acc_ref[h] + _dot(vt_ref[0, h], p)
            m_ref[h] = m_new

    @pl.when(ki < qi)
    def _():
        update(False)

    @pl.when(ki == qi)
    def _():
        update(True)
        ot = jnp.concatenate([acc_ref[h, :MLA_V] / acc_ref[h, MLA_V:MLA_V + 1] for h in range(H)], axis=0)
        o_ref[0] = _rms(ot.T, ng_ref[...]).astype(o_ref.dtype)


def _attention(q, k, vt, norm_g, tq):
    B, H, S, _ = q.shape
    W = H * MLA_V
    nq = S // tq
    pairs = [(i, j) for i in range(nq) for j in range(i + 1)]
    qtab = jnp.asarray([p[0] for p in pairs], jnp.int32)
    ktab = jnp.asarray([p[1] for p in pairs], jnp.int32)
    grid_spec = pltpu.PrefetchScalarGridSpec(
        num_scalar_prefetch=2, grid=(B, len(pairs)),
        in_specs=[pl.BlockSpec((1, H, tq, LANE), lambda b, j, qt, kt: (b, 0, qt[j], 0)),
                  pl.BlockSpec((1, H, tq, LANE), lambda b, j, qt, kt: (b, 0, kt[j], 0)),
                  pl.BlockSpec((1, H, _VT_ROWS, tq), lambda b, j, qt, kt: (b, 0, 0, kt[j])),
                  pl.BlockSpec((1, W), lambda b, j, qt, kt: (0, 0))],
        out_specs=pl.BlockSpec((1, tq, W), lambda b, j, qt, kt: (b, qt[j], 0)),
        scratch_shapes=[pltpu.VMEM((H, 1, tq), F32), pltpu.VMEM((H, _VT_ROWS, tq), F32)])
    return pl.pallas_call(
        _attn_kernel,
        grid_spec=grid_spec,
        out_shape=jax.ShapeDtypeStruct((B, S, W), BF16),
        compiler_params=_cp(("parallel", "arbitrary")),
        name="mla_attn",
    )(qtab, ktab, q, k, vt, norm_g.reshape(1, W))


def _store_token_tiles(ref, val):
    n, d = val.shape
    for s in range(d // LANE):
        ref[pl.ds(s, n, stride=SUBLANE), :] = val[:, s * LANE:(s + 1) * LANE]


def _load_token_tiles(ref, n):
    return jnp.concatenate([ref[pl.ds(s, n, stride=SUBLANE), :] for s in range(SUBLANE)], axis=1)


def _outproj_kernel(x_ref, ya_ref, yb_ref, yc_ref, yd_ref, wo_ref, gt_ref, sc_ref, sh_ref, g_ref,
                    wr_ref, br_ref, xo_ref, h_ref, lg_ref):
    B, ts, D = x_ref.shape
    n = B * ts

    acc = _dot(ya_ref[...].reshape(n, -1), wo_ref[0])
    acc = acc + _dot(_load_batch_major(yb_ref, B).astype(BF16), wo_ref[1])
    acc = acc + _dot(yc_ref[...].reshape(n, -1), wo_ref[2])
    acc = acc + _dot(_load_batch_major(yd_ref, B).astype(BF16), wo_ref[3])
    x = x_ref[...] + gt_ref[...] * acc.reshape(B, ts, D)
    xo_ref[...] = x
    h = (_rms(x, g_ref[...]) * (1.0 + sc_ref[...]) + sh_ref[...]).reshape(n, D)
    for b in range(B):
        _store_token_tiles(h_ref.at[b], h[b * ts:(b + 1) * ts])
    h_hi, h_lo = _split_bf16(h, 2)
    lg = _dot(h_hi, wr_ref[0]) + (_dot(h_lo, wr_ref[0]) + _dot(h_hi, wr_ref[1])) + br_ref[...]
    lg_ref[...] = lg.reshape(B, ts, LANE)


def _outproj(x, ya, yb, yc, yd, w_out, gt, sc, sh, g, w_router, b_router, tm):
    B, S, D = x.shape
    W = D // N_MIX
    ts = tm // B
    bm = lambda n: pl.BlockSpec((B, ts, n), lambda s: (0, s, 0))
    tmaj = pl.BlockSpec((W // LANE, ts * B, LANE), lambda s: (0, s, 0))
    const = lambda shape: pl.BlockSpec(shape, lambda s: (0,) * len(shape))
    vec = const((B, 1, D))
    return pl.pallas_call(
        _outproj_kernel,
        grid=(S // ts,),
        in_specs=[bm(D), bm(W), tmaj, bm(W), tmaj, const((N_MIX, W, D)), vec, vec, vec,
                  const((1, D)), const((2, D, LANE)), const((1, LANE))],
        out_specs=[bm(D), pl.BlockSpec((B, ts * SUBLANE, LANE), lambda s: (0, s, 0)), bm(LANE)],
        out_shape=[jax.ShapeDtypeStruct((B, S, D), F32), jax.ShapeDtypeStruct((B, S * SUBLANE, LANE), F32),
                   jax.ShapeDtypeStruct((B, S, LANE), F32)],
        compiler_params=_cp(("parallel",)),
        name="outproj",
    )(x, ya, yb, yc, yd, w_out.reshape(N_MIX, W, D).astype(BF16), gt, sc, sh, g,
      jnp.stack(_split_bf16(w_router, 2)), b_router)


_MOE_PAD_ROWS = MOE_BLOCK // TOP_K


def _moe_kernel(be_ref, nused_ref, tok_ref, tokn_ref, dst_ref, w_ref, h_hbm, w1_ref, w3_ref, w2_ref, o_hbm,
                xbuf, ybuf, gsem, ssem):
    i = pl.program_id(0)
    R = MOE_BLOCK
    TR = SUBLANE
    nused = nused_ref[0]
    slot = lax.rem(i, 2)
    other = 1 - slot

    def gather_rows(idx_ref, s):
        def body(r, c):
            src = pl.multiple_of(idx_ref[0, 0, r], TR)
            pltpu.make_async_copy(h_hbm.at[pl.ds(src, TR)], xbuf.at[s, pl.ds(pl.multiple_of(r * TR, TR), TR)],
                                  gsem.at[s]).start()
            return c
        lax.fori_loop(0, R, body, 0, unroll=8)

    def scatter_rows(s):
        def body(r, c):
            dst = pl.multiple_of(dst_ref[0, 0, r], TR)
            pltpu.make_async_copy(ybuf.at[s, pl.ds(pl.multiple_of(r * TR, TR), TR)], o_hbm.at[pl.ds(dst, TR)],
                                  ssem.at[s]).start(priority=1)
            return c
        lax.fori_loop(0, R, body, 0, unroll=8)

    def wait_gather(s):
        pltpu.make_async_copy(h_hbm.at[pl.ds(0, R * TR)], xbuf.at[s], gsem.at[s]).wait()

    def wait_scatter(s):
        pltpu.make_async_copy(ybuf.at[s], o_hbm.at[pl.ds(0, R * TR)], ssem.at[s]).wait()

    @pl.when(i == 0)
    def _():
        plane = o_hbm.shape[0] // TOP_K
        npad = _MOE_PAD_ROWS * TR
        ybuf[1] = jnp.zeros(ybuf.shape[1:], ybuf.dtype)
        for j in range(TOP_K):
            pad = pltpu.make_async_copy(ybuf.at[1, pl.ds(0, npad)],
                                        o_hbm.at[pl.ds((j + 1) * plane - npad, npad)], ssem.at[1])
            pad.start()
            pad.wait()
        gather_rows(tok_ref, 0)

    @pl.when(i < nused)
    def _():
        @pl.when(i + 1 < nused)
        def _():
            gather_rows(tokn_ref, other)

        wait_gather(slot)

        @pl.when(i >= 2)
        def _():
            wait_scatter(slot)

        xb = _load_token_tiles(xbuf.at[slot], R).astype(BF16)
        h1 = _dot(xb, w1_ref[0])
        h3 = _dot(xb, w3_ref[0])
        hid = h1 * jax.nn.sigmoid(h1) * h3
        y = _dot(hid.astype(BF16), w2_ref[0])
        _store_token_tiles(ybuf.at[slot], y * w_ref[:, 0:1])
        scatter_rows(slot)

        @pl.when(i == nused - 1)
        def _():
            wait_scatter(slot)

            @pl.when(i >= 1)
            def _():
                wait_scatter(other)


def _moe_experts(h2, blk_expert, n_used, row_tok, row_dst, row_w, w1, w3, w2):
    E, D, F = w1.shape
    T = h2.shape[0] // SUBLANE
    nb = blk_expert.shape[0]
    R = MOE_BLOCK
    blk = lambda i, be, nu: (i, 0, 0)
    nxt = lambda i, be, nu: (jnp.minimum(i + 1, nb - 1), 0, 0)
    wsel = lambda i, be, nu: (be[i], 0, 0)
    plane = (T + _MOE_PAD_ROWS) * SUBLANE
    row_tok = (row_tok * SUBLANE).reshape(nb, 1, R)
    row_dst = ((row_dst % TOP_K) * plane + (row_dst // TOP_K) * SUBLANE).reshape(nb, 1, R)
    grid_spec = pltpu.PrefetchScalarGridSpec(
        num_scalar_prefetch=2, grid=(nb,),
        in_specs=[pl.BlockSpec((1, 1, R), blk, memory_space=pltpu.SMEM),
                  pl.BlockSpec((1, 1, R), nxt, memory_space=pltpu.SMEM),
                  pl.BlockSpec((1, 1, R), blk, memory_space=pltpu.SMEM),
                  pl.BlockSpec((R, LANE), lambda i, be, nu: (i, 0)),
                  pl.BlockSpec(memory_space=pl.ANY),
                  pl.BlockSpec((1, D, F), wsel), pl.BlockSpec((1, D, F), wsel), pl.BlockSpec((1, F, D), wsel)],
        out_specs=pl.BlockSpec(memory_space=pl.ANY),
        scratch_shapes=[pltpu.VMEM((2, R * SUBLANE, LANE), F32), pltpu.VMEM((2, R * SUBLANE, LANE), F32),
                        pltpu.SemaphoreType.DMA((2,)), pltpu.SemaphoreType.DMA((2,))])
    return pl.pallas_call(
        _moe_kernel,
        grid_spec=grid_spec,
        out_shape=jax.ShapeDtypeStruct((TOP_K * plane, LANE), F32),
        compiler_params=_cp(("arbitrary",)),
        name="moe_experts",
    )(blk_expert, n_used, row_tok, row_tok, row_dst, jnp.broadcast_to(row_w.reshape(nb * R, 1), (nb * R, LANE)),
      h2, w1, w3, w2)


def _route(logits, T):
    p_group = jax.nn.softmax(logits[:, :N_GROUPS], axis=-1)
    g_sel = jnp.argmax(p_group, axis=-1).astype(jnp.int32)
    g_prob = jnp.max(p_group, axis=-1)
    logit_e = logits[:, N_GROUPS:N_GROUPS + N_EXPERTS].reshape(T, N_GROUPS, EXPERTS_PER_GROUP)
    logit_e = jnp.take_along_axis(logit_e, g_sel[:, None, None], axis=1)[:, 0]
    p_e = jax.nn.softmax(logit_e, axis=-1)
    top_p, top_i = lax.top_k(p_e, TOP_K)
    gate = (g_prob[:, None] * top_p / jnp.sum(top_p, axis=-1, keepdims=True)).reshape(-1)
    expert = (g_sel[:, None] * EXPERTS_PER_GROUP + top_i.astype(jnp.int32)).reshape(-1)
    n_assign = T * TOP_K
    n_blocks = -(-n_assign // MOE_BLOCK) + N_EXPERTS
    order = jnp.argsort(expert).astype(jnp.int32)
    counts = jnp.sum((expert[:, None] == jnp.arange(N_EXPERTS, dtype=jnp.int32)[None, :]).astype(jnp.int32), axis=0)
    start = jnp.cumsum(counts) - counts
    pcounts = (counts + MOE_BLOCK - 1) // MOE_BLOCK * MOE_BLOCK
    pend = jnp.cumsum(pcounts)
    pstart = pend - pcounts
    blk_start = jnp.arange(n_blocks, dtype=jnp.int32) * MOE_BLOCK
    blk_expert = jnp.minimum(jnp.sum((pend[None, :] <= blk_start[:, None]).astype(jnp.int32), axis=1), N_EXPERTS - 1)
    n_used = (pend[-1] // MOE_BLOCK).astype(jnp.int32).reshape(1)
    lane = jnp.arange(MOE_BLOCK, dtype=jnp.int32)[None, :]
    off = blk_start[:, None] + lane - pstart[blk_expert][:, None]
    valid = off < counts[blk_expert][:, None]
    pos = jnp.clip(start[blk_expert][:, None] + off, 0, n_assign - 1)
    asg = order[pos]
    row_tok = jnp.where(valid, asg // TOP_K, 0)
    row_dst = jnp.where(valid, asg, n_assign + lane)
    row_w = jnp.where(valid, gate[asg], 0.0)
    return blk_expert.astype(jnp.int32), n_used, row_tok, row_dst, row_w


def _combine_kernel(x_ref, y0_ref, y1_ref, gt_ref, o_ref):
    tm = x_ref.shape[1]
    y = _load_token_tiles(y0_ref.at[0], tm) + _load_token_tiles(y1_ref.at[0], tm)
    o_ref[0] = x_ref[0] + gt_ref[0] * y


def _combine(x, y2, gt, tm):
    B, S, D = x.shape
    nt = S // tm
    return pl.pallas_call(
        _combine_kernel,
        grid=(B, nt),
        in_specs=[pl.BlockSpec((1, tm, D), lambda b, s: (b, s, 0)),
                  pl.BlockSpec((1, tm * SUBLANE, LANE), lambda b, s: (0, b * nt + s, 0)),
                  pl.BlockSpec((1, tm * SUBLANE, LANE), lambda b, s: (1, b * nt + s, 0)),
                  pl.BlockSpec((1, 1, D), lambda b, s: (b, 0, 0))],
        out_specs=pl.BlockSpec((1, tm, D), lambda b, s: (b, s, 0)),
        out_shape=jax.ShapeDtypeStruct((B, S, D), F32),
        compiler_params=_cp(("parallel", "parallel")),
        name="moe_combine",
    )(x, y2, y2, gt)


def _tiles(S):
    big = S >= 2048
    return dict(tm=1024 if big else 256, tl=1024 if big else 256, ts_s5=64, ts_lru=128 if big else 64,
                tq=1024 if big else 256)


def kernel(x, c, pos_offset, ada_w, ada_b, norm1_g, w_in, rwkv_mu, rwkv_w0, rwkv_w2, rwkv_a0, rwkv_a2, rwkv_g2, rwkv_k_k, rwkv_k_a, rwkv_r_k, rwkv_ln_w, rwkv_ln_b, s5_lambda_re, s5_lambda_im, s5_b_re, s5_b_im, s5_c_re, s5_c_im, s5_d, s5_log_dt, s5_glu_w, s5_glu_b, mla_q_norm_g, mla_w_uq, mla_kv_norm_g, mla_w_ukv, mla_q_head_g, mla_k_head_g, lru_conv_w, lru_conv_b, lru_w_a, lru_b_a, lru_w_x, lru_b_x, lru_lambda, branch_norm_g, w_out, norm2_g, moe_w_group, moe_b_group, moe_w_expert, moe_b_expert, moe_w1, moe_w3, moe_w2):
    B, S, D = x.shape
    depth = ada_w.shape[0]
    T = B * S
    W = D // N_MIX
    assert B == SUBLANE, "time-major scans put the batch on the 8 sublanes"
    assert D == SUBLANE * LANE, "token-tile layout: one (8, 128) tile per token row"
    tiles = _tiles(S)
    mod = _adaln_mod(c, ada_w, ada_b)

    for l in range(depth):
        sh1, sc1, gt1, sh2, sc2, gt2 = [mod[l, :, None, i * D:(i + 1) * D] for i in range(6)]
        z_rwkv, z_s5, q_a, kv_a, kpe, z_lru, z_gate = _inproj(
            x, sc1, sh1, norm1_g[l].reshape(1, D), _pad_w_in(w_in[l]), tiles['tm'])
        y_a = _rwkv(z_rwkv, rwkv_mu[l], rwkv_w0[l], rwkv_w2[l], rwkv_a0[l], rwkv_a2[l], rwkv_g2[l],
                    rwkv_k_k[l], rwkv_k_a[l], rwkv_r_k[l].reshape(-1), rwkv_ln_w[l], rwkv_ln_b[l], tiles['tl'])
        y_b = _s5(z_s5, s5_lambda_re[l], s5_lambda_im[l], s5_b_re[l], s5_b_im[l], s5_c_re[l],
                  s5_c_im[l], s5_d[l], s5_log_dt[l], s5_glu_w[l], s5_glu_b[l], branch_norm_g[l, 0],
                  tiles['ts_s5'])
        q, k, vt = _mla_prep(q_a, kv_a, kpe, pos_offset, mla_q_norm_g[l], mla_w_uq[l], mla_kv_norm_g[l],
                             mla_w_ukv[l], mla_q_head_g[l], mla_k_head_g[l], tiles['tm'])
        y_c = _attention(q, k, vt, branch_norm_g[l, 1], tiles['tq'])
        y_d = _lru(z_lru, z_gate, lru_conv_w[l], lru_conv_b[l], lru_w_a[l], lru_b_a[l],
                   lru_w_x[l], lru_b_x[l], lru_lambda[l], branch_norm_g[l, 2], tiles['ts_lru'])
        w_router = jnp.zeros((D, LANE), F32).at[:, :N_GROUPS].set(moe_w_group[l])
        w_router = w_router.at[:, N_GROUPS:N_GROUPS + N_EXPERTS].set(moe_w_expert[l])
        b_router = jnp.zeros((1, LANE), F32).at[0, :N_GROUPS].set(moe_b_group[l])
        b_router = b_router.at[0, N_GROUPS:N_GROUPS + N_EXPERTS].set(moe_b_expert[l])
        x, h2, logits = _outproj(x, y_a, y_b, y_c, y_d, w_out[l], gt1, sc2, sh2, norm2_g[l].reshape(1, D),
                                 w_router, b_router, tiles['tm'])
        blk_expert, n_used, row_tok, row_dst, row_w = _route(logits.reshape(T, LANE), T)
        y2 = _moe_experts(h2.reshape(T * SUBLANE, LANE), blk_expert, n_used, row_tok, row_dst, row_w,
                          moe_w1[l].astype(BF16), moe_w3[l].astype(BF16), moe_w2[l].astype(BF16))
        x = _combine(x, y2.reshape(TOP_K, -1, LANE), gt2, tiles['tm'])
    return x
```
